```python
import jax, jax.numpy as jnp
from jax import lax
import numpy as np

D_MODEL = 2048
BATCH = 4
SEQ = 4096
DEPTH = 1

PLE_DIM = 256
ATTN_HEADS = 8
ATTN_HEAD_DIM = 128
ATTN_BLOCK = 128
MLSTM_HEADS = 4
MLSTM_QK_DIM = 128
MLSTM_V_DIM = 256
MLSTM_CHUNK = 64
CONV_WIDTH = 4
ATTN_WIDTH = ATTN_HEADS * ATTN_HEAD_DIM
MLSTM_QK_WIDTH = MLSTM_HEADS * MLSTM_QK_DIM
MLSTM_WIDTH = MLSTM_HEADS * MLSTM_V_DIM
MIX_WIDTH = ATTN_WIDTH + MLSTM_WIDTH
D_FF = -(-8 * D_MODEL // (3 * 256)) * 256
IN_WIDTHS = (ATTN_WIDTH, ATTN_WIDTH, ATTN_WIDTH, ATTN_HEADS,
             MLSTM_QK_WIDTH, MLSTM_QK_WIDTH, MLSTM_WIDTH, MLSTM_HEADS, MLSTM_HEADS, MLSTM_WIDTH)
IN_COLS = sum(IN_WIDTHS)
EPS = 1e-6

kernel_name = 'fox_mlstm_parallel_hybrid_block'


def rms_norm(u, w):
    uf = u.astype(jnp.float32)
    y = uf * lax.rsqrt(jnp.mean(uf * uf, axis=-1, keepdims=True) + EPS)
    return y.astype(u.dtype) * w


def causal_dwconv(u, w, b):
    K, C = w.shape
    out = lax.conv_general_dilated(u, w[:, None, :], window_strides=(1,), padding=[(K - 1, 0)],
                                   dimension_numbers=('NWC', 'WIO', 'NWC'), feature_group_count=C)
    return out + b


def forgetting_attention(q, k, v, logf):
    B, S, H, d = q.shape
    q = q.transpose(0, 2, 1, 3)
    k = k.transpose(0, 2, 1, 3)
    v = v.transpose(0, 2, 1, 3)
    c = jnp.cumsum(logf, axis=1).transpose(0, 2, 1)
    kpos = jnp.arange(S)
    scale = d ** -0.5

    def block(bi):
        start = bi * ATTN_BLOCK
        qb = lax.dynamic_slice_in_dim(q, start, ATTN_BLOCK, axis=2)
        cb = lax.dynamic_slice_in_dim(c, start, ATTN_BLOCK, axis=2)
        s = jnp.einsum('bhqd,bhkd->bhqk', qb, k).astype(jnp.float32) * scale
        s = s + cb[..., :, None] - c[..., None, :]
        qpos = start + jnp.arange(ATTN_BLOCK)
        s = jnp.where(kpos[None, :] <= qpos[:, None], s, -jnp.inf)
        pr = jax.nn.softmax(s, axis=-1).astype(v.dtype)
        return jnp.einsum('bhqk,bhkd->bhqd', pr, v)

    o = lax.map(block, jnp.arange(S // ATTN_BLOCK))
    return o.transpose(1, 0, 3, 2, 4).reshape(B, S, H * d)


def mlstm_chunkwise(q, k, v, i_pre, logf):
    B, H, S, dk = q.shape
    dv = v.shape[-1]
    L = MLSTM_CHUNK
    NC = S // L
    qc = jnp.moveaxis(q.reshape(B, H, NC, L, dk), 2, 0)
    kc = jnp.moveaxis(k.reshape(B, H, NC, L, dk), 2, 0)
    vc = jnp.moveaxis(v.reshape(B, H, NC, L, dv), 2, 0)
    ic = jnp.moveaxis(i_pre.reshape(B, H, NC, L), 2, 0)
    fc = jnp.moveaxis(logf.reshape(B, H, NC, L), 2, 0)
    causal = jnp.tril(jnp.ones((L, L), dtype=bool))

    def step(carry, xs):
        C, n, m = carry
        qt, kt, vt, it, ft = xs
        b = jnp.cumsum(ft, axis=-1)
        D = b[..., :, None] - b[..., None, :] + it[..., None, :]
        D = jnp.where(causal, D, -jnp.inf)
        inter = b + m[..., None]
        m_t = jnp.maximum(inter, jnp.max(D, axis=-1))
        w_intra = jnp.exp(D - m_t[..., None])
        w_inter = jnp.exp(inter - m_t)
        s_qk = jnp.einsum('bhtd,bhsd->bhts', qt, kt) * w_intra
        num = jnp.einsum('bhts,bhsv->bhtv', s_qk, vt) + w_inter[..., None] * jnp.einsum('bhvd,bhtd->bhtv', C, qt)
        den = jnp.sum(s_qk, axis=-1) + w_inter * jnp.einsum('bhd,bhtd->bht', n, qt)
        h = num / jnp.maximum(jnp.abs(den), jnp.exp(-m_t))[..., None]
        bL = b[..., -1]
        g = bL[..., None] - b + it
        m_new = jnp.maximum(bL + m, jnp.max(g, axis=-1))
        ws = jnp.exp(g - m_new[..., None])
        wc = jnp.exp(bL + m - m_new)
        C_new = wc[..., None, None] * C + jnp.einsum('bhs,bhsv,bhsd->bhvd', ws, vt, kt)
        n_new = wc[..., None] * n + jnp.einsum('bhs,bhsd->bhd', ws, kt)
        return (C_new, n_new, m_new), h

    init = (jnp.zeros((B, H, dv, dk), jnp.float32), jnp.zeros((B, H, dk), jnp.float32), jnp.zeros((B, H), jnp.float32))
    _, hs = lax.scan(step, init, (qc, kc, vc, ic, fc))
    return jnp.moveaxis(hs, 0, 2).reshape(B, H, S, dv)


def setup_inputs(seed: int = 0) -> dict:
    key = jax.random.key(seed)
    ks = jax.random.split(key, 24)
    f32 = jnp.float32

    def nrm(k, shape, scale):
        return jax.random.normal(k, shape, f32) * scale

    def gain(k, shape):
        return 1.0 + 0.01 * jax.random.normal(k, shape, f32)

    return {
        'x': nrm(ks[0], (BATCH, SEQ, D_MODEL), 1.0),
        'p': nrm(ks[1], (DEPTH, BATCH, SEQ, PLE_DIM), 1.0),
        'w_norm_mix': gain(ks[2], (DEPTH, D_MODEL)),
        'w_in': nrm(ks[3], (DEPTH, D_MODEL, IN_COLS), D_MODEL ** -0.5),
        'fox_f_bias': jnp.linspace(1.0, 5.0, ATTN_HEADS, dtype=f32)[None, :] + nrm(ks[4], (DEPTH, ATTN_HEADS), 0.1),
        'q_norm_w': gain(ks[5], (DEPTH, ATTN_HEAD_DIM)),
        'k_norm_w': gain(ks[6], (DEPTH, ATTN_HEAD_DIM)),
        'mlstm_conv_w': nrm(ks[7], (DEPTH, CONV_WIDTH, 2 * MLSTM_QK_WIDTH), CONV_WIDTH ** -0.5),
        'mlstm_conv_b': nrm(ks[8], (DEPTH, 2 * MLSTM_QK_WIDTH), 0.01),
        'mlstm_i_bias': nrm(ks[9], (DEPTH, MLSTM_HEADS), 0.1),
        'mlstm_f_bias': jnp.linspace(3.0, 6.0, MLSTM_HEADS, dtype=f32)[None, :] + nrm(ks[10], (DEPTH, MLSTM_HEADS), 0.1),
        'mlstm_out_norm_w': gain(ks[11], (DEPTH, MLSTM_WIDTH)),
        'w_out': nrm(ks[12], (DEPTH, MIX_WIDTH, D_MODEL), MIX_WIDTH ** -0.5),
        'w_norm_ffn': gain(ks[13], (DEPTH, D_MODEL)),
        'w_ffn_gate': nrm(ks[14], (DEPTH, D_MODEL, D_FF), D_MODEL ** -0.5),
        'w_ffn_up': nrm(ks[15], (DEPTH, D_MODEL, D_FF), D_MODEL ** -0.5),
        'w_ffn_down': nrm(ks[16], (DEPTH, D_FF, D_MODEL), D_FF ** -0.5),
        'w_norm_ple': gain(ks[17], (DEPTH, D_MODEL)),
        'w_ple_gate': nrm(ks[18], (DEPTH, D_MODEL, D_MODEL), D_MODEL ** -0.5),
        'w_ple_proj': nrm(ks[19], (DEPTH, PLE_DIM, D_MODEL), PLE_DIM ** -0.5),
        'w_ple_post_norm': gain(ks[20], (DEPTH, D_MODEL)),
    }


def reference(x, p, w_norm_mix, w_in, fox_f_bias, q_norm_w, k_norm_w, mlstm_conv_w, mlstm_conv_b,
              mlstm_i_bias, mlstm_f_bias, mlstm_out_norm_w, w_out, w_norm_ffn, w_ffn_gate, w_ffn_up,
              w_ffn_down, w_norm_ple, w_ple_gate, w_ple_proj, w_ple_post_norm):
    B, S, _ = x.shape
    f32 = jnp.float32
    split_points = np.cumsum(IN_WIDTHS)[:-1].tolist()
    for i in range(DEPTH):
        h = rms_norm(x, w_norm_mix[i])
        proj = h @ w_in[i]
        aq, ak, av, af, mq, mk, mv, mi, mf, mo = jnp.split(proj, split_points, axis=-1)

        aq = rms_norm(aq.reshape(B, S, ATTN_HEADS, ATTN_HEAD_DIM), q_norm_w[i])
        ak = rms_norm(ak.reshape(B, S, ATTN_HEADS, ATTN_HEAD_DIM), k_norm_w[i])
        av = av.reshape(B, S, ATTN_HEADS, ATTN_HEAD_DIM)
        a_logf = jax.nn.log_sigmoid(af.astype(f32) + fox_f_bias[i].astype(f32))
        attn_out = forgetting_attention(aq, ak, av, a_logf)

        mqk = jax.nn.silu(causal_dwconv(jnp.concatenate([mq, mk], axis=-1), mlstm_conv_w[i], mlstm_conv_b[i]))
        mq, mk = jnp.split(mqk, 2, axis=-1)
        mq = mq.reshape(B, S, MLSTM_HEADS, MLSTM_QK_DIM).transpose(0, 2, 1, 3).astype(f32) * (MLSTM_QK_DIM ** -0.5)
        mk = mk.reshape(B, S, MLSTM_HEADS, MLSTM_QK_DIM).transpose(0, 2, 1, 3).astype(f32)
        mv = mv.reshape(B, S, MLSTM_HEADS, MLSTM_V_DIM).transpose(0, 2, 1, 3).astype(f32)
        m_i = (mi.astype(f32) + mlstm_i_bias[i].astype(f32)).transpose(0, 2, 1)
        m_logf = jax.nn.log_sigmoid(mf.astype(f32) + mlstm_f_bias[i].astype(f32)).transpose(0, 2, 1)
        ht = mlstm_chunkwise(mq, mk, mv, m_i, m_logf)
        ht = rms_norm(ht.transpose(0, 2, 1, 3), jnp.ones((), f32)).reshape(B, S, MLSTM_WIDTH)
        mlstm_out = (ht.astype(x.dtype) * mlstm_out_norm_w[i]) * jax.nn.sigmoid(mo)

        x = x + jnp.concatenate([attn_out, mlstm_out], axis=-1) @ w_out[i]

        h2 = rms_norm(x, w_norm_ffn[i])
        x = x + (jax.nn.silu(h2 @ w_ffn_gate[i]) * (h2 @ w_ffn_up[i])) @ w_ffn_down[i]

        gate = jax.nn.sigmoid(rms_norm(x, w_norm_ple[i]) @ w_ple_gate[i])
        e = rms_norm(p[i] @ w_ple_proj[i], w_ple_post_norm[i])
        x = x + gate * e
    return x
```

```python
import functools

import jax
import jax.numpy as jnp
from jax import lax
from jax.experimental import pallas as pl
from jax.experimental.pallas import tpu as pltpu

F32 = jnp.float32
BF16 = jnp.bfloat16
EPS = 1e-6

ATTN_HEADS = 8
ATTN_HEAD_DIM = 128
MLSTM_HEADS = 4
MLSTM_QK_DIM = 128
MLSTM_V_DIM = 256
CONV_WIDTH = 4
ATTN_WIDTH = ATTN_HEADS * ATTN_HEAD_DIM
MLSTM_QK_WIDTH = MLSTM_HEADS * MLSTM_QK_DIM
MLSTM_WIDTH = MLSTM_HEADS * MLSTM_V_DIM
N_GATES = ATTN_HEADS + 2 * MLSTM_HEADS
GATE_LANES = 128
CONV_HALO = 8

VMEM_LIMIT_BYTES = 56 * 1024 * 1024

ROW_TILE = 512
PROJ_COL_TILE = 512
MLSTM_CHUNK = 256
ATTN_Q_TILE = 512
ATTN_KV_TILE = 512
FFN_COL_TILE = 512

_T_AQ, _T_AK, _T_AV, _T_MQ, _T_MK, _T_MV, _T_MO, _T_END = 0, 2, 4, 6, 7, 8, 10, 12


def _log_sigmoid(z):
    return jnp.minimum(z, 0.0) - jnp.log1p(jnp.exp(-jnp.abs(z)))


def _sigmoid(z):
    return 1.0 / (1.0 + jnp.exp(-z))


def _split3(v):
    hi = v.astype(BF16)
    r1 = v - hi.astype(F32)
    mid = r1.astype(BF16)
    lo = (r1 - mid.astype(F32)).astype(BF16)
    return hi, mid, lo


def _in_proj_kernel(x_ref, wn_ref, w_ref, wg_ref, wgt_ref, gb_ref, gbt_ref, qn_ref, kn_ref, cw_ref, cb_ref,
                    proj_ref, gcol_ref, grow_ref,
                    h_scr, tri_scr, carry_scr, conv_scr, *, tiles_per_seq, chunk):
    i = pl.program_id(0)
    j = pl.program_id(1)
    tm = x_ref.shape[0]
    tn = w_ref.shape[1]
    seq_start = (i % tiles_per_seq) == 0

    @pl.when((i == 0) & (j == 0))
    def _():
        r = lax.broadcasted_iota(jnp.int32, (tm, tm), 0)
        c = lax.broadcasted_iota(jnp.int32, (tm, tm), 1)
        same_chunk = (r & -chunk) == (c & -chunk)
        tri_scr[0] = jnp.where(r <= c, 1.0, 0.0).astype(BF16)
        tri_scr[1] = jnp.where((r <= c) & same_chunk, 1.0, 0.0).astype(BF16)
        tri_scr[2] = jnp.where((c <= r) & same_chunk, 1.0, 0.0).astype(BF16)

    @pl.when(j == 0)
    def _():
        xf = x_ref[...]
        ms = jnp.mean(xf * xf, axis=-1, keepdims=True)
        h = (xf * lax.rsqrt(ms + EPS)) * wn_ref[...]
        hb = h.astype(BF16)
        h_scr[...] = hb

        gc = jnp.dot(hb, wg_ref[...], preferred_element_type=F32) + gb_ref[...]
        lane = lax.broadcasted_iota(jnp.int32, gc.shape, 1)
        is_in_gate = (lane >= ATTN_HEADS) & (lane < ATTN_HEADS + MLSTM_HEADS)
        gc = jnp.where(is_in_gate, gc, _log_sigmoid(gc))
        cs = jnp.zeros_like(gc)
        for part in _split3(gc):
            cs = cs + jnp.dot(tri_scr[2], part, preferred_element_type=F32)
        gcol_ref[...] = jnp.where(lane >= ATTN_HEADS + MLSTM_HEADS, cs, gc)

        gr = lax.dot_general(wgt_ref[...], hb, (((1,), (1,)), ((), ())), preferred_element_type=F32) + gbt_ref[...]
        row = lax.broadcasted_iota(jnp.int32, gr.shape, 0)
        is_in_gate_r = (row >= ATTN_HEADS) & (row < ATTN_HEADS + MLSTM_HEADS)
        gr = jnp.where(is_in_gate_r, gr, _log_sigmoid(gr))
        cs_full = jnp.zeros_like(gr)
        cs_chunk = jnp.zeros_like(gr)
        for part in _split3(gr):
            cs_full = cs_full + jnp.dot(part, tri_scr[0], preferred_element_type=F32)
            cs_chunk = cs_chunk + jnp.dot(part, tri_scr[1], preferred_element_type=F32)
        @pl.when(seq_start)
        def _():
            carry_scr[...] = jnp.zeros_like(carry_scr)

        cs_full = cs_full + carry_scr[...][:, :1]
        carry_scr[...] = jnp.broadcast_to(cs_full[:, tm - 1:tm], carry_scr.shape)
        grow_ref[...] = jnp.where(row < ATTN_HEADS, cs_full, jnp.where(is_in_gate_r, gr, cs_chunk))

    acc = jnp.dot(h_scr[...], w_ref[...], preferred_element_type=F32)

    def head_norm(w):
        outs = []
        for hh in range(tn // ATTN_HEAD_DIM):
            a = acc[:, hh * ATTN_HEAD_DIM:(hh + 1) * ATTN_HEAD_DIM]
            ms = jnp.mean(a * a, axis=-1, keepdims=True)
            outs.append((a * lax.rsqrt(ms + EPS)) * w)
        return jnp.concatenate(outs, axis=-1)

    @pl.when(j < _T_AK)
    def _():
        proj_ref[...] = head_norm(qn_ref[...] * (ATTN_HEAD_DIM ** -0.5)).astype(BF16)

    @pl.when((j >= _T_AK) & (j < _T_AV))
    def _():
        proj_ref[...] = head_norm(kn_ref[...]).astype(BF16)

    @pl.when(((j >= _T_AV) & (j < _T_MQ)) | (j >= _T_MV))
    def _():
        proj_ref[...] = acc.astype(BF16)

    @pl.when((j >= _T_MQ) & (j < _T_MV))
    def _():
        jj = j - _T_MQ
        buf = conv_scr.at[jj]

        @pl.when(seq_start)
        def _():
            buf[0:CONV_HALO, :] = jnp.zeros((CONV_HALO, tn), F32)

        @pl.when(jnp.logical_not(seq_start))
        def _():
            buf[0:CONV_HALO, :] = buf[tm:tm + CONV_HALO, :]

        buf[CONV_HALO:CONV_HALO + tm, :] = acc
        cw = cw_ref[0]
        y = cb_ref[0] + cw[CONV_WIDTH - 1:CONV_WIDTH, :] * acc
        for tap in range(CONV_WIDTH - 1):
            off = CONV_HALO - (CONV_WIDTH - 1) + tap
            y = y + cw[tap:tap + 1, :] * buf[off:off + tm, :]
        y = y * _sigmoid(y)
        y = y * jnp.where(jj == 0, MLSTM_QK_DIM ** -0.5, 1.0)
        proj_ref[...] = y.astype(BF16)


def _in_proj(x2d, wn, w_main, wg, wgt, gb, gbt, qn, kn, cw, cb, *, batch, seq):
    T, D = x2d.shape
    tm, tn = ROW_TILE, PROJ_COL_TILE
    n_cols = w_main.shape[1]
    tiles_per_seq = seq // tm
    grid = (T // tm, n_cols // tn)
    const = lambda i, j: (0, 0)
    conv_idx = lambda i, j: (jnp.clip(j - _T_MQ, 0, 1), 0, 0)
    kern = functools.partial(_in_proj_kernel, tiles_per_seq=tiles_per_seq, chunk=MLSTM_CHUNK)
    return pl.pallas_call(
        kern,
        grid=grid,
        in_specs=[
            pl.BlockSpec((tm, D), lambda i, j: (i, 0)),
            pl.BlockSpec((1, D), const),
            pl.BlockSpec((D, tn), lambda i, j: (0, j)),
            pl.BlockSpec((D, GATE_LANES), const),
            pl.BlockSpec((2 * CONV_HALO, D), const),
            pl.BlockSpec((1, GATE_LANES), const),
            pl.BlockSpec((2 * CONV_HALO, 1), const),
            pl.BlockSpec((1, ATTN_HEAD_DIM), const),
            pl.BlockSpec((1, ATTN_HEAD_DIM), const),
            pl.BlockSpec((1, CONV_WIDTH, tn), conv_idx),
            pl.BlockSpec((1, 1, tn), conv_idx),
        ],
        out_specs=[
            pl.BlockSpec((tm, tn), lambda i, j: (i, j)),
            pl.BlockSpec((tm, GATE_LANES), lambda i, j: (i, 0)),
            pl.BlockSpec((None, 2 * CONV_HALO, tm), lambda i, j: (i // tiles_per_seq, 0, i % tiles_per_seq)),
        ],
        out_shape=[
            jax.ShapeDtypeStruct((T, n_cols), BF16),
            jax.ShapeDtypeStruct((T, GATE_LANES), F32),
            jax.ShapeDtypeStruct((batch, 2 * CONV_HALO, seq), F32),
        ],
        scratch_shapes=[
            pltpu.VMEM((tm, D), BF16),
            pltpu.VMEM((3, tm, tm), BF16),
            pltpu.VMEM((2 * CONV_HALO, GATE_LANES), F32),
            pltpu.VMEM((2, CONV_HALO + tm, tn), F32),
        ],
        compiler_params=pltpu.CompilerParams(
            dimension_semantics=("arbitrary", "arbitrary"), vmem_limit_bytes=VMEM_LIMIT_BYTES),
        name="in_proj",
    )(x2d, wn, w_main, wg, wgt, gb, gbt, qn, kn, cw, cb)


def _fox_kernel(q_ref, k_ref, v_ref, c_ref, o_ref, *, tk):
    h = pl.program_id(1)
    qi = pl.program_id(2)
    tq = q_ref.shape[0]
    d = q_ref.shape[1]
    q = q_ref[...]

    def step(j, carry, masked):
        m, l, acc = carry
        start = pl.multiple_of(j * tk, tk)
        k = k_ref[pl.ds(start, tk), :]
        v = v_ref[pl.ds(start, tk), :]
        cj = c_ref[pl.ds(h, 1), pl.ds(start, tk)]
        s = lax.dot_general(q, k, (((1,), (1,)), ((), ())), preferred_element_type=F32) - cj
        if masked:
            r = lax.broadcasted_iota(jnp.int32, (tq, tk), 0)
            c = lax.broadcasted_iota(jnp.int32, (tq, tk), 1)
            s = jnp.where(c <= r, s, -jnp.inf)
        m_new = jnp.maximum(m, jnp.max(s, axis=-1, keepdims=True))
        alpha = jnp.exp(m - m_new)
        p = jnp.exp(s - m_new)
        l = alpha * l + jnp.sum(p, axis=-1, keepdims=True)
        acc = alpha * acc + jnp.dot(p.astype(BF16), v, preferred_element_type=F32)
        return m_new, l, acc

    init = (jnp.full((tq, 1), -jnp.inf, F32), jnp.zeros((tq, 1), F32), jnp.zeros((tq, d), F32))
    carry = lax.fori_loop(0, qi, lambda j, c: step(j, c, False), init)
    m, l, acc = step(qi, carry, True)
    o_ref[...] = (acc / l).astype(o_ref.dtype)


def _fox_attention(proj3, grow, *, batch, seq):
    tq, tk = ATTN_Q_TILE, ATTN_KV_TILE
    assert tq == tk
    d = ATTN_HEAD_DIM
    grid = (batch, ATTN_HEADS, seq // tq)
    return pl.pallas_call(
        functools.partial(_fox_kernel, tk=tk),
        grid=grid,
        in_specs=[
            pl.BlockSpec((None, tq, d), lambda b, h, qi: (b, qi, h)),
            pl.BlockSpec((None, seq, d), lambda b, h, qi: (b, 0, ATTN_HEADS + h)),
            pl.BlockSpec((None, seq, d), lambda b, h, qi: (b, 0, 2 * ATTN_HEADS + h)),
            pl.BlockSpec((None, 2 * CONV_HALO, seq), lambda b, h, qi: (b, 0, 0)),
        ],
        out_specs=pl.BlockSpec((None, tq, d), lambda b, h, qi: (b, qi, h)),
        out_shape=jax.ShapeDtypeStruct((batch, seq, ATTN_WIDTH), BF16),
        compiler_params=pltpu.CompilerParams(
            dimension_semantics=("arbitrary", "arbitrary", "arbitrary"), vmem_limit_bytes=VMEM_LIMIT_BYTES),
        name="fox_attention",
    )(proj3, proj3, proj3, grow)


def _mlstm_kernel(q_ref, k_ref, v_ref, og_ref, gcol_ref, grow_ref, nw_ref, o_ref, c_scr, n_scr, m_scr):
    ci = pl.program_id(1)
    L = q_ref.shape[0]
    dk, dv = MLSTM_QK_DIM, MLSTM_V_DIM

    @pl.when(ci == 0)
    def _():
        c_scr[...] = jnp.zeros_like(c_scr)
        n_scr[...] = jnp.zeros_like(n_scr)
        m_scr[...] = jnp.zeros_like(m_scr)

    t_idx = lax.broadcasted_iota(jnp.int32, (L, L), 0)
    s_idx = lax.broadcasted_iota(jnp.int32, (L, L), 1)
    causal = s_idx <= t_idx
    gcol = gcol_ref[...]
    grow = grow_ref[...]

    for hh in range(MLSTM_HEADS):
        gi, gf = ATTN_HEADS + hh, ATTN_HEADS + MLSTM_HEADS + hh
        q = q_ref[:, hh * dk:(hh + 1) * dk]
        k = k_ref[:, hh * dk:(hh + 1) * dk]
        v = v_ref[:, hh * dv:(hh + 1) * dv]
        i_row, b_row = grow[gi:gi + 1, :], grow[gf:gf + 1, :]
        i_col, b_col = gcol[:, gi:gi + 1], gcol[:, gf:gf + 1]
        a_row = i_row - b_row
        a_col = i_col - b_col
        m_prev = m_scr[hh][:1, :1]
        ct = c_scr[hh]
        n = n_scr[hh][:1, :]

        big_m = jnp.maximum(jnp.max(jnp.where(causal, a_row, -jnp.inf), axis=-1, keepdims=True), m_prev)
        w_intra = jnp.exp(jnp.where(causal, a_row - big_m, -jnp.inf))
        w_inter = jnp.exp(m_prev - big_m)
        s_qk = lax.dot_general(q, k, (((1,), (1,)), ((), ())), preferred_element_type=F32) * w_intra
        num = jnp.dot(s_qk.astype(BF16), v, preferred_element_type=F32)
        num = num + w_inter * jnp.dot(q, ct.astype(BF16), preferred_element_type=F32)
        qn = jnp.sum(q.astype(F32) * n, axis=-1, keepdims=True)
        den = jnp.sum(s_qk, axis=-1, keepdims=True) + w_inter * qn
        hcell = num / jnp.maximum(jnp.abs(den), jnp.exp(-(b_col + big_m)))

        ms = jnp.mean(hcell * hcell, axis=-1, keepdims=True)
        hn = hcell * lax.rsqrt(ms + EPS)
        og = og_ref[:, hh * dv:(hh + 1) * dv].astype(F32)
        out = (hn * nw_ref[:, hh * dv:(hh + 1) * dv]) * _sigmoid(og)
        o_ref[:, hh * dv:(hh + 1) * dv] = out.astype(o_ref.dtype)

        m_last = jnp.maximum(jnp.max(a_row, axis=-1, keepdims=True), m_prev)
        ws_col = jnp.exp(a_col - m_last)
        wc = jnp.exp(m_prev - m_last)
        kw = k.astype(F32) * ws_col
        upd = lax.dot_general(kw.astype(BF16), v, (((0,), (0,)), ((), ())), preferred_element_type=F32)
        c_scr[hh] = wc * ct + upd
        n_new = wc * n + jnp.sum(kw, axis=0, keepdims=True)
        n_scr[hh] = jnp.broadcast_to(n_new, n_scr.shape[1:])
        m_new = b_row[:, L - 1:L] + m_last
        m_scr[hh] = jnp.broadcast_to(m_new, m_scr.shape[1:])


def _mlstm(proj3, gcol3, grow, nw, *, batch, seq):
    L = MLSTM_CHUNK
    grid = (batch, seq // L)
    qk_w, v_w = MLSTM_QK_WIDTH, MLSTM_WIDTH
    base = 3 * ATTN_WIDTH
    return pl.pallas_call(
        _mlstm_kernel,
        grid=grid,
        in_specs=[
            pl.BlockSpec((None, L, qk_w), lambda b, c: (b, c, base // qk_w)),
            pl.BlockSpec((None, L, qk_w), lambda b, c: (b, c, base // qk_w + 1)),
            pl.BlockSpec((None, L, v_w), lambda b, c: (b, c, (base + 2 * qk_w) // v_w)),
            pl.BlockSpec((None, L, v_w), lambda b, c: (b, c, (base + 2 * qk_w) // v_w + 1)),
            pl.BlockSpec((None, L, GATE_LANES), lambda b, c: (b, c, 0)),
            pl.BlockSpec((None, 2 * CONV_HALO, L), lambda b, c: (b, 0, c)),
            pl.BlockSpec((1, v_w), lambda b, c: (0, 0)),
        ],
        out_specs=pl.BlockSpec((None, L, v_w), lambda b, c: (b, c, 0)),
        out_shape=jax.ShapeDtypeStruct((batch, seq, v_w), BF16),
        scratch_shapes=[
            pltpu.VMEM((MLSTM_HEADS, MLSTM_QK_DIM, MLSTM_V_DIM), F32),
            pltpu.VMEM((MLSTM_HEADS, 8, MLSTM_QK_DIM), F32),
            pltpu.VMEM((MLSTM_HEADS, 8, 128), F32),
        ],
        compiler_params=pltpu.CompilerParams(
            dimension_semantics=("arbitrary", "arbitrary"), vmem_limit_bytes=VMEM_LIMIT_BYTES),
        name="mlstm",
    )(proj3, proj3, proj3, proj3, gcol3, grow, nw)


def _out_proj_kernel(x_ref, a_ref, m_ref, wa_ref, wm_ref, o_ref):
    y = jnp.dot(a_ref[...], wa_ref[...], preferred_element_type=F32)
    y = y + jnp.dot(m_ref[...], wm_ref[...], preferred_element_type=F32)
    o_ref[...] = x_ref[...] + y


def _out_proj(x2d, attn2d, mlstm2d, w_attn, w_mlstm):
    T, D = x2d.shape
    tm = ROW_TILE
    const = lambda i: (0, 0)
    return pl.pallas_call(
        _out_proj_kernel,
        grid=(T // tm,),
        in_specs=[
            pl.BlockSpec((tm, D), lambda i: (i, 0)),
            pl.BlockSpec((tm, attn2d.shape[1]), lambda i: (i, 0)),
            pl.BlockSpec((tm, mlstm2d.shape[1]), lambda i: (i, 0)),
            pl.BlockSpec(w_attn.shape, const),
            pl.BlockSpec(w_mlstm.shape, const),
        ],
        out_specs=pl.BlockSpec((tm, D), lambda i: (i, 0)),
        out_shape=jax.ShapeDtypeStruct((T, D), F32),
        compiler_params=pltpu.CompilerParams(
            dimension_semantics=("arbitrary",), vmem_limit_bytes=VMEM_LIMIT_BYTES),
        name="out_proj",
    )(x2d, attn2d, mlstm2d, w_attn, w_mlstm)


def _ffn_kernel(x_ref, wn_ref, wg_ref, wu_ref, wd_ref, o_ref, h_scr):
    f = pl.program_id(1)

    @pl.when(f == 0)
    def _():
        xf = x_ref[...]
        ms = jnp.mean(xf * xf, axis=-1, keepdims=True)
        h_scr[...] = ((xf * lax.rsqrt(ms + EPS)) * wn_ref[...]).astype(BF16)
        o_ref[...] = xf

    hb = h_scr[...]
    g = jnp.dot(hb, wg_ref[...], preferred_element_type=F32)
    u = jnp.dot(hb, wu_ref[...], preferred_element_type=F32)
    a = (g * _sigmoid(g)) * u
    o_ref[...] += jnp.dot(a.astype(BF16), wd_ref[...], preferred_element_type=F32)


def _ffn(x2d, wn, wg, wu, wd):
    T, D = x2d.shape
    F = wg.shape[1]
    tm, tf = ROW_TILE, FFN_COL_TILE
    return pl.pallas_call(
        _ffn_kernel,
        grid=(T // tm, F // tf),
        in_specs=[
            pl.BlockSpec((tm, D), lambda i, f: (i, 0)),
            pl.BlockSpec((1, D), lambda i, f: (0, 0)),
            pl.BlockSpec((D, tf), lambda i, f: (0, f)),
            pl.BlockSpec((D, tf), lambda i, f: (0, f)),
            pl.BlockSpec((tf, D), lambda i, f: (f, 0)),
        ],
        out_specs=pl.BlockSpec((tm, D), lambda i, f: (i, 0)),
        out_shape=jax.ShapeDtypeStruct((T, D), F32),
        scratch_shapes=[pltpu.VMEM((tm, D), BF16)],
        compiler_params=pltpu.CompilerParams(
            dimension_semantics=("arbitrary", "arbitrary"), vmem_limit_bytes=VMEM_LIMIT_BYTES),
        name="ffn",
    )(x2d, wn, wg, wu, wd)


def _ple_kernel(x_ref, p_ref, wn_ref, wgate_ref, wproj_ref, wpost_ref, o_ref):
    xf = x_ref[...]
    ms = jnp.mean(xf * xf, axis=-1, keepdims=True)
    hb = ((xf * lax.rsqrt(ms + EPS)) * wn_ref[...]).astype(BF16)
    gate = _sigmoid(jnp.dot(hb, wgate_ref[...], preferred_element_type=F32))
    e = jnp.dot(p_ref[...].astype(BF16), wproj_ref[...], preferred_element_type=F32)
    ems = jnp.mean(e * e, axis=-1, keepdims=True)
    e = (e * lax.rsqrt(ems + EPS)) * wpost_ref[...]
    o_ref[...] = xf + gate * e


def _ple(x2d, p2d, wn, wgate, wproj, wpost):
    T, D = x2d.shape
    P = p2d.shape[1]
    tm = ROW_TILE
    const = lambda i: (0, 0)
    return pl.pallas_call(
        _ple_kernel,
        grid=(T // tm,),
        in_specs=[
            pl.BlockSpec((tm, D), lambda i: (i, 0)),
            pl.BlockSpec((tm, P), lambda i: (i, 0)),
            pl.BlockSpec((1, D), const),
            pl.BlockSpec((D, D), const),
            pl.BlockSpec((P, D), const),
            pl.BlockSpec((1, D), const),
        ],
        out_specs=pl.BlockSpec((tm, D), lambda i: (i, 0)),
        out_shape=jax.ShapeDtypeStruct((T, D), F32),
        compiler_params=pltpu.CompilerParams(
            dimension_semantics=("arbitrary",), vmem_limit_bytes=VMEM_LIMIT_BYTES),
        name="ple",
    )(x2d, p2d, wn, wgate, wproj, wpost)


def _layer(x2d, p2d, batch, seq, w_norm_mix, w_in, fox_f_bias, q_norm_w, k_norm_w, mlstm_conv_w, mlstm_conv_b,
           mlstm_i_bias, mlstm_f_bias, mlstm_out_norm_w, w_out, w_norm_ffn, w_ffn_gate, w_ffn_up, w_ffn_down,
           w_norm_ple, w_ple_gate, w_ple_proj, w_ple_post_norm):
    D = x2d.shape[1]
    A, QK, MV = ATTN_WIDTH, MLSTM_QK_WIDTH, MLSTM_WIDTH
    o = 0
    cols = {}
    for name, width in (("aq", A), ("ak", A), ("av", A), ("af", ATTN_HEADS), ("mq", QK), ("mk", QK), ("mv", MV),
                        ("mi", MLSTM_HEADS), ("mf", MLSTM_HEADS), ("mo", MV)):
        cols[name] = w_in[:, o:o + width]
        o += width
    w_main = jnp.concatenate([cols[n] for n in ("aq", "ak", "av", "mq", "mk", "mv", "mo")], axis=1).astype(BF16)
    w_gate = jnp.concatenate([cols["af"], cols["mi"], cols["mf"]], axis=1)
    wg = jnp.pad(w_gate, ((0, 0), (0, GATE_LANES - N_GATES))).astype(BF16)
    wgt = w_gate.T.astype(BF16)
    gate_bias = jnp.concatenate([fox_f_bias, mlstm_i_bias, mlstm_f_bias]).astype(F32)
    gb = jnp.pad(gate_bias, (0, GATE_LANES - N_GATES))[None, :]
    gbt = gate_bias[:, None]
    cw = mlstm_conv_w.reshape(CONV_WIDTH, 2, QK).transpose(1, 0, 2)
    cb = mlstm_conv_b.reshape(2, 1, QK)

    proj, gcol, grow = _in_proj(x2d, w_norm_mix[None, :], w_main, wg, wgt, gb, gbt, q_norm_w[None, :],
                                k_norm_w[None, :], cw, cb, batch=batch, seq=seq)
    proj3 = proj.reshape(batch, seq, proj.shape[1])
    attn = _fox_attention(proj3, grow, batch=batch, seq=seq)
    mlstm = _mlstm(proj3, gcol.reshape(batch, seq, GATE_LANES), grow, mlstm_out_norm_w[None, :],
                   batch=batch, seq=seq)

    x1 = _out_proj(x2d, attn.reshape(batch * seq, A), mlstm.reshape(batch * seq, MV),
                   w_out[:A].astype(BF16), w_out[A:].astype(BF16))
    x2 = _ffn(x1, w_norm_ffn[None, :], w_ffn_gate.astype(BF16), w_ffn_up.astype(BF16), w_ffn_down.astype(BF16))
    x3 = _ple(x2, p2d, w_norm_ple[None, :], w_ple_gate.astype(BF16), w_ple_proj.astype(BF16),
              w_ple_post_norm[None, :])
    return x3


def kernel(x, p, w_norm_mix, w_in, fox_f_bias, q_norm_w, k_norm_w, mlstm_conv_w, mlstm_conv_b, mlstm_i_bias,
           mlstm_f_bias, mlstm_out_norm_w, w_out, w_norm_ffn, w_ffn_gate, w_ffn_up, w_ffn_down, w_norm_ple,
           w_ple_gate, w_ple_proj, w_ple_post_norm):
    B, S, D = x.shape
    depth = w_in.shape[0]
    x2d = x.reshape(B * S, D)
    for i in range(depth):
        x2d = _layer(x2d, p[i].reshape(B * S, p.shape[-1]), B, S, w_norm_mix[i], w_in[i], fox_f_bias[i],
                     q_norm_w[i], k_norm_w[i], mlstm_conv_w[i], mlstm_conv_b[i], mlstm_i_bias[i], mlstm_f_bias[i],
                     mlstm_out_norm_w[i], w_out[i], w_norm_ffn[i], w_ffn_gate[i], w_ffn_up[i], w_ffn_down[i],
                     w_norm_ple[i], w_ple_gate[i], w_ple_proj[i], w_ple_post_norm[i])
    return x2d.reshape(B, S, D)
```

```python
import functools
import math

import jax
import jax.numpy as jnp
from jax import lax
from jax.experimental import pallas as pl
from jax.experimental.pallas import tpu as pltpu

F32 = jnp.float32
BF16 = jnp.bfloat16
EPS = 1e-6

ATTN_HEADS = 8
ATTN_HEAD_DIM = 128
MLSTM_HEADS = 4
MLSTM_QK_DIM = 128
MLSTM_V_DIM = 256
CONV_WIDTH = 4
ATTN_WIDTH = ATTN_HEADS * ATTN_HEAD_DIM
MLSTM_QK_WIDTH = MLSTM_HEADS * MLSTM_QK_DIM
MLSTM_WIDTH = MLSTM_HEADS * MLSTM_V_DIM
N_GATES = ATTN_HEADS + 2 * MLSTM_HEADS
GATE_LANES = 128
CONV_HALO = 8

VMEM_LIMIT_BYTES = 56 * 1024 * 1024

ROW_TILE = 512
PROJ_COL_TILE = 512
MLSTM_CHUNK = 256
ATTN_Q_TILE = 512
ATTN_KV_TILE = 512
ATTN_HEADS_PER_STEP = 2
LOG2E = math.log2(math.e)
FFN_COL_TILE = 512

_T_AQ, _T_AK, _T_AV, _T_MQ, _T_MK, _T_MV, _T_MO, _T_END = 0, 2, 4, 6, 7, 8, 10, 12


def _log_sigmoid(z):
    return jnp.minimum(z, 0.0) - jnp.log1p(jnp.exp(-jnp.abs(z)))


def _sigmoid(z):
    return 1.0 / (1.0 + jnp.exp(-z))


def _split3(v):
    hi = v.astype(BF16)
    r1 = v - hi.astype(F32)
    mid = r1.astype(BF16)
    lo = (r1 - mid.astype(F32)).astype(BF16)
    return hi, mid, lo


def _lane_prefix_sum(v, period):
    axis = v.ndim - 1
    pos = lax.broadcasted_iota(jnp.int32, v.shape, axis) & (period - 1)
    shift = 1
    while shift < period:
        v = v + jnp.where(pos >= shift, pltpu.roll(v, shift, axis=axis), 0.0)
        shift *= 2
    return v


def _in_proj_kernel(x_ref, wn_ref, w_ref, wgt_ref, gbt_ref, qn_ref, kn_ref, cw_ref, cb_ref,
                    proj_ref, gcol_ref, grow_ref, h_scr, carry_scr, conv_scr, *, tiles_per_seq, chunk):
    i = pl.program_id(0)
    j = pl.program_id(1)
    tm = x_ref.shape[0]
    tn = PROJ_COL_TILE
    group = proj_ref.shape[1]
    tiles_per_group = group // tn
    seq_start = (i % tiles_per_seq) == 0

    def tile_acc(t):
        return jnp.dot(h_scr[...], w_ref[:, t * tn:(t + 1) * tn], preferred_element_type=F32)

    def store(t, val):
        c0 = (t % tiles_per_group) * tn
        proj_ref[:, c0:c0 + tn] = val.astype(BF16)

    def head_norm(acc, w):
        outs = []
        for hh in range(tn // ATTN_HEAD_DIM):
            a = acc[:, hh * ATTN_HEAD_DIM:(hh + 1) * ATTN_HEAD_DIM]
            ms = jnp.mean(a * a, axis=-1, keepdims=True)
            outs.append((a * lax.rsqrt(ms + EPS)) * w)
        return jnp.concatenate(outs, axis=-1)

    def conv_silu(acc, which, scale):
        buf = conv_scr.at[which]
        buf[0:CONV_HALO, :] = jnp.where(seq_start, 0.0, buf[tm:tm + CONV_HALO, :])
        buf[CONV_HALO:CONV_HALO + tm, :] = acc
        cw = cw_ref[which]
        y = cb_ref[which] + cw[CONV_WIDTH - 1:CONV_WIDTH, :] * acc
        for tap in range(CONV_WIDTH - 1):
            off = CONV_HALO - (CONV_WIDTH - 1) + tap
            y = y + cw[tap:tap + 1, :] * buf[off:off + tm, :]
        y = y * _sigmoid(y)
        return y * scale if scale != 1.0 else y

    @pl.when((i == 0) & (j == 0))
    def _():
        carry_scr[...] = jnp.zeros_like(carry_scr)
        conv_scr[...] = jnp.zeros_like(conv_scr)

    @pl.when(j == 0)
    def _():
        xf = x_ref[...]
        ms = jnp.mean(xf * xf, axis=-1, keepdims=True)
        hb = ((xf * lax.rsqrt(ms + EPS)) * wn_ref[...]).astype(BF16)
        h_scr[...] = hb

        gr = lax.dot_general(wgt_ref[...], hb, (((1,), (1,)), ((), ())), preferred_element_type=F32) + gbt_ref[...]
        row = lax.broadcasted_iota(jnp.int32, gr.shape, 0)
        is_in_gate = (row >= ATTN_HEADS) & (row < ATTN_HEADS + MLSTM_HEADS)
        gr = jnp.where(is_in_gate, gr, _log_sigmoid(gr))
        cs_chunk = _lane_prefix_sum(gr, chunk)
        cs_full = _lane_prefix_sum(gr, tm)
        carry = jnp.where(seq_start, 0.0, carry_scr[...])
        cs_full = cs_full + carry[:, :1]
        carry_scr[...] = jnp.broadcast_to(cs_full[:, tm - 1:tm], carry_scr.shape)
        gates = jnp.where(row < ATTN_HEADS, cs_full * LOG2E, jnp.where(is_in_gate, gr, cs_chunk))
        grow_ref[...] = gates
        padded = jnp.concatenate([gates, jnp.zeros((GATE_LANES - gates.shape[0], tm), F32)], axis=0)
        gcol_ref[...] = padded.T

        qw = qn_ref[...] * (LOG2E * ATTN_HEAD_DIM ** -0.5)
        for t in range(_T_AQ, _T_AK):
            store(t, head_norm(tile_acc(t), qw))
        for t in range(_T_AK, _T_AV):
            store(t, head_norm(tile_acc(t), kn_ref[...]))

    @pl.when(j == 1)
    def _():
        store(_T_MQ, conv_silu(tile_acc(_T_MQ), 0, MLSTM_QK_DIM ** -0.5))
        store(_T_MK, conv_silu(tile_acc(_T_MK), 1, 1.0))
        for t in range(_T_AV, _T_MQ):
            store(t, tile_acc(t))

    @pl.when(j == 2)
    def _():
        for t in range(_T_MV, _T_END):
            store(t, tile_acc(t))


def _in_proj(x2d, wn, w_main, wgt, gbt, qn, kn, cw, cb, *, batch, seq):
    T, D = x2d.shape
    tm, tn = ROW_TILE, PROJ_COL_TILE
    n_cols = w_main.shape[1]
    n_groups = 3
    group = n_cols // n_groups
    assert group == (_T_AV - _T_AQ) * tn == (_T_MV - _T_AV) * tn == (_T_END - _T_MV) * tn
    tiles_per_seq = seq // tm
    const = lambda i, j: (0, 0)
    const3 = lambda i, j: (0, 0, 0)
    kern = functools.partial(_in_proj_kernel, tiles_per_seq=tiles_per_seq, chunk=MLSTM_CHUNK)
    return pl.pallas_call(
        kern,
        grid=(T // tm, n_groups),
        in_specs=[
            pl.BlockSpec((tm, D), lambda i, j: (i, 0)),
            pl.BlockSpec((1, D), const),
            pl.BlockSpec((D, n_cols), const, pipeline_mode=pl.Buffered(1)),
            pl.BlockSpec((N_GATES, D), const),
            pl.BlockSpec((N_GATES, 1), const),
            pl.BlockSpec((1, ATTN_HEAD_DIM), const),
            pl.BlockSpec((1, ATTN_HEAD_DIM), const),
            pl.BlockSpec((2, CONV_WIDTH, tn), const3),
            pl.BlockSpec((2, 1, tn), const3),
        ],
        out_specs=[
            pl.BlockSpec((tm, group), lambda i, j: (i, j)),
            pl.BlockSpec((tm, GATE_LANES), lambda i, j: (i, 0)),
            pl.BlockSpec((None, N_GATES, tm), lambda i, j: (i // tiles_per_seq, 0, i % tiles_per_seq)),
        ],
        out_shape=[
            jax.ShapeDtypeStruct((T, n_cols), BF16),
            jax.ShapeDtypeStruct((T, GATE_LANES), F32),
            jax.ShapeDtypeStruct((batch, N_GATES, seq), F32),
        ],
        scratch_shapes=[
            pltpu.VMEM((tm, D), BF16),
            pltpu.VMEM((N_GATES, GATE_LANES), F32),
            pltpu.VMEM((2, CONV_HALO + tm, tn), F32),
        ],
        compiler_params=pltpu.CompilerParams(
            dimension_semantics=("arbitrary", "arbitrary"), vmem_limit_bytes=VMEM_LIMIT_BYTES),
        name="in_proj",
    )(x2d, wn, w_main, wgt, gbt, qn, kn, cw, cb)


def _fox_kernel(q_ref, k_ref, v_ref, c_ref, o_ref, *, tk):
    hg = pl.program_id(1)
    qi = pl.program_id(2)
    tq = q_ref.shape[0]
    d = ATTN_HEAD_DIM
    n_heads = q_ref.shape[1] // d
    qs = [q_ref[:, g * d:(g + 1) * d] for g in range(n_heads)]

    def step(j, carry, diag):
        start = pl.multiple_of(j * tk, tk)
        r0 = 0 if diag is None else diag * tk
        out = []
        for g in range(n_heads):
            m, l, acc = carry[g]
            k = k_ref[pl.ds(start, tk), g * d:(g + 1) * d]
            v = v_ref[pl.ds(start, tk), g * d:(g + 1) * d]
            cj = c_ref[pl.ds(hg * n_heads + g, 1), pl.ds(start, tk)]
            s = lax.dot_general(qs[g][r0:], k, (((1,), (1,)), ((), ())), preferred_element_type=F32) - cj
            if diag is not None:
                r = lax.broadcasted_iota(jnp.int32, (tq - r0, tk), 0)
                c = lax.broadcasted_iota(jnp.int32, (tq - r0, tk), 1)
                s = jnp.where(c <= r, s, -jnp.inf)
            m_new = jnp.maximum(m[r0:], jnp.max(s, axis=-1, keepdims=True))
            alpha = jnp.exp2(m[r0:] - m_new)
            p = jnp.exp2(s - m_new)
            l_new = alpha * l[r0:] + jnp.sum(p, axis=-1, keepdims=True)
            acc_new = alpha * acc[r0:] + jnp.dot(p.astype(BF16), v, preferred_element_type=F32)
            if r0:
                m_new = jnp.concatenate([m[:r0], m_new], axis=0)
                l_new = jnp.concatenate([l[:r0], l_new], axis=0)
                acc_new = jnp.concatenate([acc[:r0], acc_new], axis=0)
            out.append((m_new, l_new, acc_new))
        return tuple(out)

    init = tuple((jnp.full((tq, 1), -jnp.inf, F32), jnp.zeros((tq, 1), F32), jnp.zeros((tq, d), F32))
                 for _ in range(n_heads))
    n_diag = tq // tk
    carry = lax.fori_loop(0, qi * n_diag, lambda j, c: step(j, c, None), init)
    for jj in range(n_diag):
        carry = step(qi * n_diag + jj, carry, jj)
    for g in range(n_heads):
        m, l, acc = carry[g]
        o_ref[:, g * d:(g + 1) * d] = (acc / l).astype(o_ref.dtype)


def _fox_attention(proj3, grow, *, batch, seq):
    tq, tk = ATTN_Q_TILE, ATTN_KV_TILE
    assert tq % tk == 0
    dd =ATTN_HEAD_DIM * ATTN_HEADS_PER_STEP
    n_groups = ATTN_HEADS // ATTN_HEADS_PER_STEP
    grid = (batch, n_groups, seq // tq)
    return pl.pallas_call(
        functools.partial(_fox_kernel, tk=tk),
        grid=grid,
        in_specs=[
            pl.BlockSpec((None, tq, dd), lambda b, h, qi: (b, qi, h)),
            pl.BlockSpec((None, seq, dd), lambda b, h, qi: (b, 0, n_groups + h)),
            pl.BlockSpec((None, seq, dd), lambda b, h, qi: (b, 0, 2 * n_groups + h)),
            pl.BlockSpec((None, N_GATES, seq), lambda b, h, qi: (b, 0, 0)),
        ],
        out_specs=pl.BlockSpec((None, tq, dd), lambda b, h, qi: (b, qi, h)),
        out_shape=jax.ShapeDtypeStruct((batch, seq, ATTN_WIDTH), BF16),
        compiler_params=pltpu.CompilerParams(
            dimension_semantics=("arbitrary", "arbitrary", "arbitrary"), vmem_limit_bytes=VMEM_LIMIT_BYTES),
        name="fox_attention",
    )(proj3, proj3, proj3, grow)


def _mlstm_kernel(q_ref, k_ref, v_ref, og_ref, gcol_ref, grow_ref, nw_ref, o_ref, c_scr, n_scr, m_scr):
    ci = pl.program_id(1)
    L = q_ref.shape[0]
    dk, dv = MLSTM_QK_DIM, MLSTM_V_DIM

    @pl.when(ci == 0)
    def _():
        c_scr[...] = jnp.zeros_like(c_scr)
        n_scr[...] = jnp.zeros_like(n_scr)
        m_scr[...] = jnp.zeros_like(m_scr)

    t_idx = lax.broadcasted_iota(jnp.int32, (L, L), 0)
    s_idx = lax.broadcasted_iota(jnp.int32, (L, L), 1)
    causal = s_idx <= t_idx
    gcol = gcol_ref[...]
    grow = grow_ref[...]

    for hh in range(MLSTM_HEADS):
        gi, gf = ATTN_HEADS + hh, ATTN_HEADS + MLSTM_HEADS + hh
        q = q_ref[:, hh * dk:(hh + 1) * dk]
        k = k_ref[:, hh * dk:(hh + 1) * dk]
        v = v_ref[:, hh * dv:(hh + 1) * dv]
        i_row, b_row = grow[gi:gi + 1, :], grow[gf:gf + 1, :]
        i_col, b_col = gcol[:, gi:gi + 1], gcol[:, gf:gf + 1]
        a_row = i_row - b_row
        a_col = i_col - b_col
        m_prev = m_scr[hh][:1, :1]
        ct = c_scr[hh]
        n = n_scr[hh][:1, :]

        big_m = jnp.maximum(jnp.max(jnp.where(causal, a_row, -jnp.inf), axis=-1, keepdims=True), m_prev)
        w_intra = jnp.exp(jnp.where(causal, a_row - big_m, -jnp.inf))
        w_inter = jnp.exp(m_prev - big_m)
        s_qk = lax.dot_general(q, k, (((1,), (1,)), ((), ())), preferred_element_type=F32) * w_intra
        num = jnp.dot(s_qk.astype(BF16), v, preferred_element_type=F32)
        num = num + w_inter * jnp.dot(q, ct.astype(BF16), preferred_element_type=F32)
        qn = jnp.sum(q.astype(F32) * n, axis=-1, keepdims=True)
        den = jnp.sum(s_qk, axis=-1, keepdims=True) + w_inter * qn
        hcell = num / jnp.maximum(jnp.abs(den), jnp.exp(-(b_col + big_m)))

        ms = jnp.mean(hcell * hcell, axis=-1, keepdims=True)
        hn = hcell * lax.rsqrt(ms + EPS)
        og = og_ref[:, hh * dv:(hh + 1) * dv].astype(F32)
        out = (hn * nw_ref[:, hh * dv:(hh + 1) * dv]) * _sigmoid(og)
        o_ref[:, hh * dv:(hh + 1) * dv] = out.astype(o_ref.dtype)

        m_last = jnp.maximum(jnp.max(a_row, axis=-1, keepdims=True), m_prev)
        ws_col = jnp.exp(a_col - m_last)
        wc = jnp.exp(m_prev - m_last)
        kw = k.astype(F32) * ws_col
        upd = lax.dot_general(kw.astype(BF16), v, (((0,), (0,)), ((), ())), preferred_element_type=F32)
        c_scr[hh] = wc * ct + upd
        n_new = wc * n + jnp.sum(kw, axis=0, keepdims=True)
        n_scr[hh] = jnp.broadcast_to(n_new, n_scr.shape[1:])
        m_new = b_row[:, L - 1:L] + m_last
        m_scr[hh] = jnp.broadcast_to(m_new, m_scr.shape[1:])


def _mlstm(proj3, gcol3, grow, nw, *, batch, seq):
    L = MLSTM_CHUNK
    grid = (batch, seq // L)
    qk_w, v_w = MLSTM_QK_WIDTH, MLSTM_WIDTH
    base = 3 * ATTN_WIDTH
    return pl.pallas_call(
        _mlstm_kernel,
        grid=grid,
        in_specs=[
            pl.BlockSpec((None, L, qk_w), lambda b, c: (b, c, base // qk_w)),
            pl.BlockSpec((None, L, qk_w), lambda b, c: (b, c, base // qk_w + 1)),
            pl.BlockSpec((None, L, v_w), lambda b, c: (b, c, (base + 2 * qk_w) // v_w)),
            pl.BlockSpec((None, L, v_w), lambda b, c: (b, c, (base + 2 * qk_w) // v_w + 1)),
            pl.BlockSpec((None, L, GATE_LANES), lambda b, c: (b, c, 0)),
            pl.BlockSpec((None, N_GATES, L), lambda b, c: (b, 0, c)),
            pl.BlockSpec((1, v_w), lambda b, c: (0, 0)),
        ],
        out_specs=pl.BlockSpec((None, L, v_w), lambda b, c: (b, c, 0)),
        out_shape=jax.ShapeDtypeStruct((batch, seq, v_w), BF16),
        scratch_shapes=[
            pltpu.VMEM((MLSTM_HEADS, MLSTM_QK_DIM, MLSTM_V_DIM), F32),
            pltpu.VMEM((MLSTM_HEADS, 8, MLSTM_QK_DIM), F32),
            pltpu.VMEM((MLSTM_HEADS, 8, 128), F32),
        ],
        compiler_params=pltpu.CompilerParams(
            dimension_semantics=("arbitrary", "arbitrary"), vmem_limit_bytes=VMEM_LIMIT_BYTES),
        name="mlstm",
    )(proj3, proj3, proj3, proj3, gcol3, grow, nw)


def _out_proj_kernel(x_ref, a_ref, m_ref, wa_ref, wm_ref, o_ref):
    y = jnp.dot(a_ref[...], wa_ref[...], preferred_element_type=F32)
    y = y + jnp.dot(m_ref[...], wm_ref[...], preferred_element_type=F32)
    o_ref[...] = x_ref[...] + y


def _out_proj(x2d, attn2d, mlstm2d, w_attn, w_mlstm):
    T, D = x2d.shape
    tm = ROW_TILE
    const = lambda i: (0, 0)
    return pl.pallas_call(
        _out_proj_kernel,
        grid=(T // tm,),
        in_specs=[
            pl.BlockSpec((tm, D), lambda i: (i, 0)),
            pl.BlockSpec((tm, attn2d.shape[1]), lambda i: (i, 0)),
            pl.BlockSpec((tm, mlstm2d.shape[1]), lambda i: (i, 0)),
            pl.BlockSpec(w_attn.shape, const),
            pl.BlockSpec(w_mlstm.shape, const),
        ],
        out_specs=pl.BlockSpec((tm, D), lambda i: (i, 0)),
        out_shape=jax.ShapeDtypeStruct((T, D), F32),
        compiler_params=pltpu.CompilerParams(
            dimension_semantics=("arbitrary",), vmem_limit_bytes=VMEM_LIMIT_BYTES),
        name="out_proj",
    )(x2d, attn2d, mlstm2d, w_attn, w_mlstm)


def _ffn_kernel(x_ref, wn_ref, wg_ref, wu_ref, wd_ref, o_ref, h_scr):
    f = pl.program_id(1)

    @pl.when(f == 0)
    def _():
        xf = x_ref[...]
        ms = jnp.mean(xf * xf, axis=-1, keepdims=True)
        h_scr[...] = ((xf * lax.rsqrt(ms + EPS)) * wn_ref[...]).astype(BF16)
        o_ref[...] = xf

    hb = h_scr[...]
    g = jnp.dot(hb, wg_ref[...], preferred_element_type=F32)
    u = jnp.dot(hb, wu_ref[...], preferred_element_type=F32)
    a = (g * _sigmoid(g)) * u
    o_ref[...] += jnp.dot(a.astype(BF16), wd_ref[...], preferred_element_type=F32)


def _ffn(x2d, wn, wg, wu, wd):
    T, D = x2d.shape
    F = wg.shape[1]
    tm, tf = ROW_TILE, FFN_COL_TILE
    return pl.pallas_call(
        _ffn_kernel,
        grid=(T // tm, F // tf),
        in_specs=[
            pl.BlockSpec((tm, D), lambda i, f: (i, 0)),
            pl.BlockSpec((1, D), lambda i, f: (0, 0)),
            pl.BlockSpec((D, tf), lambda i, f: (0, f)),
            pl.BlockSpec((D, tf), lambda i, f: (0, f)),
            pl.BlockSpec((tf, D), lambda i, f: (f, 0)),
        ],
        out_specs=pl.BlockSpec((tm, D), lambda i, f: (i, 0)),
        out_shape=jax.ShapeDtypeStruct((T, D), F32),
        scratch_shapes=[pltpu.VMEM((tm, D), BF16)],
        compiler_params=pltpu.CompilerParams(
            dimension_semantics=("arbitrary", "arbitrary"), vmem_limit_bytes=VMEM_LIMIT_BYTES),
        name="ffn",
    )(x2d, wn, wg, wu, wd)


def _ple_kernel(x_ref, p_ref, wn_ref, wgate_ref, wproj_ref, wpost_ref, o_ref):
    xf = x_ref[...]
    ms = jnp.mean(xf * xf, axis=-1, keepdims=True)
    hb = ((xf * lax.rsqrt(ms + EPS)) * wn_ref[...]).astype(BF16)
    gate = _sigmoid(jnp.dot(hb, wgate_ref[...], preferred_element_type=F32))
    e = jnp.dot(p_ref[...].astype(BF16), wproj_ref[...], preferred_element_type=F32)
    ems = jnp.mean(e * e, axis=-1, keepdims=True)
    e = (e * lax.rsqrt(ems + EPS)) * wpost_ref[...]
    o_ref[...] = xf + gate * e


def _ple(x2d, p2d, wn, wgate, wproj, wpost):
    T, D = x2d.shape
    P = p2d.shape[1]
    tm = ROW_TILE
    const = lambda i: (0, 0)
    return pl.pallas_call(
        _ple_kernel,
        grid=(T // tm,),
        in_specs=[
            pl.BlockSpec((tm, D), lambda i: (i, 0)),
            pl.BlockSpec((tm, P), lambda i: (i, 0)),
            pl.BlockSpec((1, D), const),
            pl.BlockSpec((D, D), const),
            pl.BlockSpec((P, D), const),
            pl.BlockSpec((1, D), const),
        ],
        out_specs=pl.BlockSpec((tm, D), lambda i: (i, 0)),
        out_shape=jax.ShapeDtypeStruct((T, D), F32),
        compiler_params=pltpu.CompilerParams(
            dimension_semantics=("arbitrary",), vmem_limit_bytes=VMEM_LIMIT_BYTES),
        name="ple",
    )(x2d, p2d, wn, wgate, wproj, wpost)


def _layer(x2d, p2d, batch, seq, w_norm_mix, w_in, fox_f_bias, q_norm_w, k_norm_w, mlstm_conv_w, mlstm_conv_b,
           mlstm_i_bias, mlstm_f_bias, mlstm_out_norm_w, w_out, w_norm_ffn, w_ffn_gate, w_ffn_up, w_ffn_down,
           w_norm_ple, w_ple_gate, w_ple_proj, w_ple_post_norm):
    D = x2d.shape[1]
    A, QK, MV = ATTN_WIDTH, MLSTM_QK_WIDTH, MLSTM_WIDTH
    o = 0
    cols = {}
    for name, width in (("aq", A), ("ak", A), ("av", A), ("af", ATTN_HEADS), ("mq", QK), ("mk", QK), ("mv", MV),
                        ("mi", MLSTM_HEADS), ("mf", MLSTM_HEADS), ("mo", MV)):
        cols[name] = w_in[:, o:o + width]
        o += width
    w_main = jnp.concatenate([cols[n] for n in ("aq", "ak", "av", "mq", "mk", "mv", "mo")], axis=1).astype(BF16)
    w_gate = jnp.concatenate([cols["af"], cols["mi"], cols["mf"]], axis=1)
    wgt = w_gate.T.astype(BF16)
    gbt = jnp.concatenate([fox_f_bias, mlstm_i_bias, mlstm_f_bias]).astype(F32)[:, None]
    cw = mlstm_conv_w.reshape(CONV_WIDTH, 2, QK).transpose(1, 0, 2)
    cb = mlstm_conv_b.reshape(2, 1, QK)

    proj, gcol, grow = _in_proj(x2d, w_norm_mix[None, :], w_main, wgt, gbt, q_norm_w[None, :],
                                k_norm_w[None, :], cw, cb, batch=batch, seq=seq)
    proj3 = proj.reshape(batch, seq, proj.shape[1])
    attn = _fox_attention(proj3, grow, batch=batch, seq=seq)
    mlstm = _mlstm(proj3, gcol.reshape(batch, seq, GATE_LANES), grow, mlstm_out_norm_w[None, :],
                   batch=batch, seq=seq)

    x1 = _out_proj(x2d, attn.reshape(batch * seq, A), mlstm.reshape(batch * seq, MV),
                   w_out[:A].astype(BF16), w_out[A:].astype(BF16))
    x2 = _ffn(x1, w_norm_ffn[None, :], w_ffn_gate.astype(BF16), w_ffn_up.astype(BF16), w_ffn_down.astype(BF16))
    x3 = _ple(x2, p2d, w_norm_ple[None, :], w_ple_gate.astype(BF16), w_ple_proj.astype(BF16),
              w_ple_post_norm[None, :])
    return x3


def kernel(x, p, w_norm_mix, w_in, fox_f_bias, q_norm_w, k_norm_w, mlstm_conv_w, mlstm_conv_b, mlstm_i_bias,
           mlstm_f_bias, mlstm_out_norm_w, w_out, w_norm_ffn, w_ffn_gate, w_ffn_up, w_ffn_down, w_norm_ple,
           w_ple_gate, w_ple_proj, w_ple_post_norm):
    B, S, D = x.shape
    depth = w_in.shape[0]
    x2d = x.reshape(B * S, D)
    for i in range(depth):
        x2d = _layer(x2d, p[i].reshape(B * S, p.shape[-1]), B, S, w_norm_mix[i], w_in[i], fox_f_bias[i],
                     q_norm_w[i], k_norm_w[i], mlstm_conv_w[i], mlstm_conv_b[i], mlstm_i_bias[i], mlstm_f_bias[i],
                     mlstm_out_norm_w[i], w_out[i], w_norm_ffn[i], w_ffn_gate[i], w_ffn_up[i], w_ffn_down[i],
                     w_norm_ple[i], w_ple_gate[i], w_ple_proj[i], w_ple_post_norm[i])
    return x2d.reshape(B, S, D)
```

```python
import functools
import math

import jax
import jax.numpy as jnp
from jax import lax
from jax.experimental import pallas as pl
from jax.experimental.pallas import tpu as pltpu

F32 = jnp.float32
BF16 = jnp.bfloat16
EPS = 1e-6

ATTN_HEADS = 8
ATTN_HEAD_DIM = 128
MLSTM_HEADS = 4
MLSTM_QK_DIM = 128
MLSTM_V_DIM = 256
CONV_WIDTH = 4
ATTN_WIDTH = ATTN_HEADS * ATTN_HEAD_DIM
MLSTM_QK_WIDTH = MLSTM_HEADS * MLSTM_QK_DIM
MLSTM_WIDTH = MLSTM_HEADS * MLSTM_V_DIM
N_GATES = ATTN_HEADS + 2 * MLSTM_HEADS
GATE_LANES = 128
CONV_HALO = 8

VMEM_LIMIT_BYTES = 56 * 1024 * 1024

ROW_TILE = 512
PROJ_COL_TILE = 512
MLSTM_CHUNK = 256
ATTN_Q_TILE = 512
ATTN_KV_TILE = 512
ATTN_HEADS_PER_STEP = 2
LOG2E = math.log2(math.e)
FOX_MAX_BOUND = 40.0
FFN_COL_TILE = 512

_T_AQ, _T_AK, _T_AV, _T_MQ, _T_MK, _T_MV, _T_MO, _T_END = 0, 2, 4, 6, 7, 8, 10, 12


def _log_sigmoid(z):
    return jnp.minimum(z, 0.0) - jnp.log1p(jnp.exp(-jnp.abs(z)))


def _sigmoid(z):
    return 1.0 / (1.0 + jnp.exp(-z))


def _split3(v):
    hi = v.astype(BF16)
    r1 = v - hi.astype(F32)
    mid = r1.astype(BF16)
    lo = (r1 - mid.astype(F32)).astype(BF16)
    return hi, mid, lo


def _lane_prefix_sum(v, period):
    axis = v.ndim - 1
    pos = lax.broadcasted_iota(jnp.int32, v.shape, axis) & (period - 1)
    shift = 1
    while shift < period:
        v = v + jnp.where(pos >= shift, pltpu.roll(v, shift, axis=axis), 0.0)
        shift *= 2
    return v


def _in_proj_kernel(x_ref, wn_ref, w_ref, wgt_ref, gbt_ref, qn_ref, kn_ref, cw_ref, cb_ref,
                    proj_ref, gcol_ref, grow_ref, h_scr, carry_scr, conv_scr, *, tiles_per_seq, chunk):
    i = pl.program_id(0)
    j = pl.program_id(1)
    tm = x_ref.shape[0]
    tn = PROJ_COL_TILE
    group = proj_ref.shape[1]
    tiles_per_group = group // tn
    seq_start = (i % tiles_per_seq) == 0

    def tile_acc(t):
        return jnp.dot(h_scr[...], w_ref[:, t * tn:(t + 1) * tn], preferred_element_type=F32)

    def store(t, val):
        c0 = (t % tiles_per_group) * tn
        proj_ref[:, c0:c0 + tn] = val.astype(BF16)

    def head_norm(acc, w):
        outs = []
        for hh in range(tn // ATTN_HEAD_DIM):
            a = acc[:, hh * ATTN_HEAD_DIM:(hh + 1) * ATTN_HEAD_DIM]
            ms = jnp.mean(a * a, axis=-1, keepdims=True)
            outs.append((a * lax.rsqrt(ms + EPS)) * w)
        return jnp.concatenate(outs, axis=-1)

    def conv_silu(acc, which, scale):
        buf = conv_scr.at[which]
        buf[0:CONV_HALO, :] = jnp.where(seq_start, 0.0, buf[tm:tm + CONV_HALO, :])
        buf[CONV_HALO:CONV_HALO + tm, :] = acc
        cw = cw_ref[which]
        y = cb_ref[which] + cw[CONV_WIDTH - 1:CONV_WIDTH, :] * acc
        for tap in range(CONV_WIDTH - 1):
            off = CONV_HALO - (CONV_WIDTH - 1) + tap
            y = y + cw[tap:tap + 1, :] * buf[off:off + tm, :]
        y = y * _sigmoid(y)
        return y * scale if scale != 1.0 else y

    @pl.when((i == 0) & (j == 0))
    def _():
        carry_scr[...] = jnp.zeros_like(carry_scr)
        conv_scr[...] = jnp.zeros_like(conv_scr)

    @pl.when(j == 0)
    def _():
        xf = x_ref[...]
        ms = jnp.mean(xf * xf, axis=-1, keepdims=True)
        hb = ((xf * lax.rsqrt(ms + EPS)) * wn_ref[...]).astype(BF16)
        h_scr[...] = hb

        gr = lax.dot_general(wgt_ref[...], hb, (((1,), (1,)), ((), ())), preferred_element_type=F32) + gbt_ref[...]
        row = lax.broadcasted_iota(jnp.int32, gr.shape, 0)
        is_in_gate = (row >= ATTN_HEADS) & (row < ATTN_HEADS + MLSTM_HEADS)
        gr = jnp.where(is_in_gate, gr, _log_sigmoid(gr))
        cs_chunk = _lane_prefix_sum(gr, chunk)
        cs_full = _lane_prefix_sum(gr, tm)
        carry = jnp.where(seq_start, 0.0, carry_scr[...])
        cs_full = cs_full + carry[:, :1]
        carry_scr[...] = jnp.broadcast_to(cs_full[:, tm - 1:tm], carry_scr.shape)
        gates = jnp.where(row < ATTN_HEADS, cs_full * LOG2E, jnp.where(is_in_gate, gr, cs_chunk))
        grow_ref[...] = gates
        padded = jnp.concatenate([gates, jnp.zeros((GATE_LANES - gates.shape[0], tm), F32)], axis=0)
        gcol_ref[...] = padded.T

        qw = qn_ref[...] * (LOG2E * ATTN_HEAD_DIM ** -0.5)
        for t in range(_T_AQ, _T_AK):
            store(t, head_norm(tile_acc(t), qw))
        for t in range(_T_AK, _T_AV):
            store(t, head_norm(tile_acc(t), kn_ref[...]))

    @pl.when(j == 1)
    def _():
        store(_T_MQ, conv_silu(tile_acc(_T_MQ), 0, MLSTM_QK_DIM ** -0.5))
        store(_T_MK, conv_silu(tile_acc(_T_MK), 1, 1.0))
        for t in range(_T_AV, _T_MQ):
            store(t, tile_acc(t))

    @pl.when(j == 2)
    def _():
        for t in range(_T_MV, _T_END):
            store(t, tile_acc(t))


def _in_proj(x2d, wn, w_main, wgt, gbt, qn, kn, cw, cb, *, batch, seq):
    T, D = x2d.shape
    tm, tn = ROW_TILE, PROJ_COL_TILE
    n_cols = w_main.shape[1]
    n_groups = 3
    group = n_cols // n_groups
    assert group == (_T_AV - _T_AQ) * tn == (_T_MV - _T_AV) * tn == (_T_END - _T_MV) * tn
    tiles_per_seq = seq // tm
    const = lambda i, j: (0, 0)
    const3 = lambda i, j: (0, 0, 0)
    kern = functools.partial(_in_proj_kernel, tiles_per_seq=tiles_per_seq, chunk=MLSTM_CHUNK)
    return pl.pallas_call(
        kern,
        grid=(T // tm, n_groups),
        in_specs=[
            pl.BlockSpec((tm, D), lambda i, j: (i, 0)),
            pl.BlockSpec((1, D), const),
            pl.BlockSpec((D, n_cols), const, pipeline_mode=pl.Buffered(1)),
            pl.BlockSpec((N_GATES, D), const),
            pl.BlockSpec((N_GATES, 1), const),
            pl.BlockSpec((1, ATTN_HEAD_DIM), const),
            pl.BlockSpec((1, ATTN_HEAD_DIM), const),
            pl.BlockSpec((2, CONV_WIDTH, tn), const3),
            pl.BlockSpec((2, 1, tn), const3),
        ],
        out_specs=[
            pl.BlockSpec((tm, group), lambda i, j: (i, j)),
            pl.BlockSpec((tm, GATE_LANES), lambda i, j: (i, 0)),
            pl.BlockSpec((None, N_GATES, tm), lambda i, j: (i // tiles_per_seq, 0, i % tiles_per_seq)),
        ],
        out_shape=[
            jax.ShapeDtypeStruct((T, n_cols), BF16),
            jax.ShapeDtypeStruct((T, GATE_LANES), F32),
            jax.ShapeDtypeStruct((batch, N_GATES, seq), F32),
        ],
        scratch_shapes=[
            pltpu.VMEM((tm, D), BF16),
            pltpu.VMEM((N_GATES, GATE_LANES), F32),
            pltpu.VMEM((2, CONV_HALO + tm, tn), F32),
        ],
        compiler_params=pltpu.CompilerParams(
            dimension_semantics=("arbitrary", "arbitrary"), vmem_limit_bytes=VMEM_LIMIT_BYTES),
        name="in_proj",
    )(x2d, wn, w_main, wgt, gbt, qn, kn, cw, cb)


def _fox_kernel(q_ref, k_ref, v_ref, c_ref, ccol_ref, o_ref, kmax_scr, *, tk):
    hg = pl.program_id(1)
    qi = pl.program_id(2)
    tq = q_ref.shape[0]
    d = ATTN_HEAD_DIM
    n_heads = q_ref.shape[1] // d
    n_diag = tq // tk
    qs = [q_ref[:, g * d:(g + 1) * d] for g in range(n_heads)]

    @pl.when(qi == 0)
    def _():
        for g in range(n_heads):
            kf = k_ref[:, g * d:(g + 1) * d].astype(F32)
            k2 = jnp.max(jnp.sum(kf * kf, axis=-1, keepdims=True), axis=0, keepdims=True)
            kmax_scr[g] = jnp.broadcast_to(k2, kmax_scr.shape[1:])

    lane = lax.broadcasted_iota(jnp.int32, ccol_ref.shape, 1)
    ccol = ccol_ref[...]
    mis, bounds = [], []
    for g in range(n_heads):
        qf = qs[g].astype(F32)
        bound = jnp.sqrt(jnp.sum(qf * qf, axis=-1, keepdims=True) * kmax_scr[g][:1, :1])
        ci = jnp.sum(jnp.where(lane == hg * n_heads + g, ccol, 0.0), axis=-1, keepdims=True)
        mis.append(bound - ci)
        bounds.append(jnp.max(bound))
    worst = functools.reduce(jnp.maximum, bounds)

    def load_block(j, g):
        start = pl.multiple_of(j * tk, tk)
        k = k_ref[pl.ds(start, tk), g * d:(g + 1) * d]
        v = v_ref[pl.ds(start, tk), g * d:(g + 1) * d]
        cj = c_ref[pl.ds(hg * n_heads + g, 1), pl.ds(start, tk)]
        return k, v, cj

    def causal_mask(r0):
        r = lax.broadcasted_iota(jnp.int32, (tq - r0, tk), 0)
        c = lax.broadcasted_iota(jnp.int32, (tq - r0, tk), 1)
        return c <= r

    def rejoin(old, new, r0):
        return jnp.concatenate([old[:r0], new], axis=0) if r0 else new

    def finish(carry):
        for g in range(n_heads):
            l, acc = carry[g][-2:]
            l = jnp.sum(l, axis=-1, keepdims=True)
            o_ref[:, g * d:(g + 1) * d] = (acc / l).astype(o_ref.dtype)

    def scores(j, g, r0):
        k = load_block(j, g)[0]
        return lax.dot_general(qs[g][r0:], k, (((1,), (1,)), ((), ())), preferred_element_type=F32)

    def consume(j, g, s, l, acc, diag):
        r0 = 0 if diag is None else diag * tk
        _, v, cj = load_block(j, g)
        e = (s - cj) - mis[g][r0:]
        if diag is not None:
            e = jnp.where(causal_mask(r0), e, -jnp.inf)
        p = jnp.exp2(e)
        l_new = l[r0:] + functools.reduce(jnp.add, [p[:, c0:c0 + d] for c0 in range(0, tk, d)])
        acc_new = acc[r0:] + jnp.dot(p.astype(BF16), v, preferred_element_type=F32)
        return rejoin(l, l_new, r0), rejoin(acc, acc_new, r0)

    def bounded_step(j, carry, diag):
        r0 = 0 if diag is None else diag * tk
        return tuple(consume(j, g, scores(j, g, r0), *carry[g], diag) for g in range(n_heads))

    def online_step(j, carry, diag):
        r0 = 0 if diag is None else diag * tk
        out = []
        for g in range(n_heads):
            m, l, acc = carry[g]
            k, v, cj = load_block(j, g)
            s = lax.dot_general(qs[g][r0:], k, (((1,), (1,)), ((), ())), preferred_element_type=F32) - cj
            if diag is not None:
                s = jnp.where(causal_mask(r0), s, -jnp.inf)
            m_new = jnp.maximum(m[r0:], jnp.max(s, axis=-1, keepdims=True))
            alpha = jnp.exp2(m[r0:] - m_new)
            p = jnp.exp2(s - m_new)
            l_new = alpha * l[r0:] + jnp.sum(p, axis=-1, keepdims=True)
            acc_new = alpha * acc[r0:] + jnp.dot(p.astype(BF16), v, preferred_element_type=F32)
            out.append((rejoin(m, m_new, r0), rejoin(l, l_new, r0), rejoin(acc, acc_new, r0)))
        return tuple(out)

    def run(step, init):
        carry = lax.fori_loop(0, qi * n_diag, lambda j, c: step(j, c, None), init)
        for jj in range(n_diag):
            carry = step(qi * n_diag + jj, carry, jj)
        finish(carry)

    zeros = (jnp.zeros((tq, 1), F32), jnp.zeros((tq, d), F32))

    @pl.when(worst <= FOX_MAX_BOUND)
    def _():
        run(bounded_step, tuple((jnp.zeros((tq, d), F32), jnp.zeros((tq, d), F32)) for _ in range(n_heads)))

    @pl.when(jnp.logical_not(worst <= FOX_MAX_BOUND))
    def _():
        run(online_step, tuple((jnp.full((tq, 1), -jnp.inf, F32),) + zeros for _ in range(n_heads)))


def _fox_attention(proj3, grow, gcol3, *, batch, seq):
    tq, tk = ATTN_Q_TILE, ATTN_KV_TILE
    assert tq % tk == 0
    dd =ATTN_HEAD_DIM * ATTN_HEADS_PER_STEP
    n_groups = ATTN_HEADS // ATTN_HEADS_PER_STEP
    grid = (batch, n_groups, seq // tq)
    return pl.pallas_call(
        functools.partial(_fox_kernel, tk=tk),
        grid=grid,
        in_specs=[
            pl.BlockSpec((None, tq, dd), lambda b, h, qi: (b, qi, h)),
            pl.BlockSpec((None, seq, dd), lambda b, h, qi: (b, 0, n_groups + h)),
            pl.BlockSpec((None, seq, dd), lambda b, h, qi: (b, 0, 2 * n_groups + h)),
            pl.BlockSpec((None, N_GATES, seq), lambda b, h, qi: (b, 0, 0)),
            pl.BlockSpec((None, tq, GATE_LANES), lambda b, h, qi: (b, qi, 0)),
        ],
        out_specs=pl.BlockSpec((None, tq, dd), lambda b, h, qi: (b, qi, h)),
        out_shape=jax.ShapeDtypeStruct((batch, seq, ATTN_WIDTH), BF16),
        scratch_shapes=[pltpu.VMEM((ATTN_HEADS_PER_STEP, 8, 128), F32)],
        compiler_params=pltpu.CompilerParams(
            dimension_semantics=("arbitrary", "arbitrary", "arbitrary"), vmem_limit_bytes=VMEM_LIMIT_BYTES),
        name="fox_attention",
    )(proj3, proj3, proj3, grow, gcol3)


def _mlstm_kernel(q_ref, k_ref, v_ref, og_ref, gcol_ref, grow_ref, nw_ref, o_ref, c_scr, n_scr, m_scr):
    ci = pl.program_id(1)
    L = q_ref.shape[0]
    dk, dv = MLSTM_QK_DIM, MLSTM_V_DIM

    @pl.when(ci == 0)
    def _():
        c_scr[...] = jnp.zeros_like(c_scr)
        n_scr[...] = jnp.zeros_like(n_scr)
        m_scr[...] = jnp.zeros_like(m_scr)

    t_idx = lax.broadcasted_iota(jnp.int32, (L, L), 0)
    s_idx = lax.broadcasted_iota(jnp.int32, (L, L), 1)
    causal = s_idx <= t_idx
    gcol = gcol_ref[...]
    grow = grow_ref[...]

    for hh in range(MLSTM_HEADS):
        gi, gf = ATTN_HEADS + hh, ATTN_HEADS + MLSTM_HEADS + hh
        q = q_ref[:, hh * dk:(hh + 1) * dk]
        k = k_ref[:, hh * dk:(hh + 1) * dk]
        v = v_ref[:, hh * dv:(hh + 1) * dv]
        i_row, b_row = grow[gi:gi + 1, :], grow[gf:gf + 1, :]
        i_col, b_col = gcol[:, gi:gi + 1], gcol[:, gf:gf + 1]
        a_row = i_row - b_row
        a_col = i_col - b_col
        m_prev = m_scr[hh][:1, :1]
        ct = c_scr[hh]
        n = n_scr[hh][:1, :]

        big_m = jnp.maximum(jnp.max(jnp.where(causal, a_row, -jnp.inf), axis=-1, keepdims=True), m_prev)
        w_intra = jnp.exp(jnp.where(causal, a_row - big_m, -jnp.inf))
        w_inter = jnp.exp(m_prev - big_m)
        s_qk = lax.dot_general(q, k, (((1,), (1,)), ((), ())), preferred_element_type=F32) * w_intra
        num = jnp.dot(s_qk.astype(BF16), v, preferred_element_type=F32)
        num = num + w_inter * jnp.dot(q, ct.astype(BF16), preferred_element_type=F32)
        qn = jnp.sum(q.astype(F32) * n, axis=-1, keepdims=True)
        den = jnp.sum(s_qk, axis=-1, keepdims=True) + w_inter * qn
        hcell = num / jnp.maximum(jnp.abs(den), jnp.exp(-(b_col + big_m)))

        ms = jnp.mean(hcell * hcell, axis=-1, keepdims=True)
        hn = hcell * lax.rsqrt(ms + EPS)
        og = og_ref[:, hh * dv:(hh + 1) * dv].astype(F32)
        out = (hn * nw_ref[:, hh * dv:(hh + 1) * dv]) * _sigmoid(og)
        o_ref[:, hh * dv:(hh + 1) * dv] = out.astype(o_ref.dtype)

        m_last = jnp.maximum(jnp.max(a_row, axis=-1, keepdims=True), m_prev)
        ws_col = jnp.exp(a_col - m_last)
        wc = jnp.exp(m_prev - m_last)
        kw = k.astype(F32) * ws_col
        upd = lax.dot_general(kw.astype(BF16), v, (((0,), (0,)), ((), ())), preferred_element_type=F32)
        c_scr[hh] = wc * ct + upd
        n_new = wc * n + jnp.sum(kw, axis=0, keepdims=True)
        n_scr[hh] = jnp.broadcast_to(n_new, n_scr.shape[1:])
        m_new = b_row[:, L - 1:L] + m_last
        m_scr[hh] = jnp.broadcast_to(m_new, m_scr.shape[1:])


def _mlstm(proj3, gcol3, grow, nw, *, batch, seq):
    L = MLSTM_CHUNK
    grid = (batch, seq // L)
    qk_w, v_w = MLSTM_QK_WIDTH, MLSTM_WIDTH
    base = 3 * ATTN_WIDTH
    return pl.pallas_call(
        _mlstm_kernel,
        grid=grid,
        in_specs=[
            pl.BlockSpec((None, L, qk_w), lambda b, c: (b, c, base // qk_w)),
            pl.BlockSpec((None, L, qk_w), lambda b, c: (b, c, base // qk_w + 1)),
            pl.BlockSpec((None, L, v_w), lambda b, c: (b, c, (base + 2 * qk_w) // v_w)),
            pl.BlockSpec((None, L, v_w), lambda b, c: (b, c, (base + 2 * qk_w) // v_w + 1)),
            pl.BlockSpec((None, L, GATE_LANES), lambda b, c: (b, c, 0)),
            pl.BlockSpec((None, N_GATES, L), lambda b, c: (b, 0, c)),
            pl.BlockSpec((1, v_w), lambda b, c: (0, 0)),
        ],
        out_specs=pl.BlockSpec((None, L, v_w), lambda b, c: (b, c, 0)),
        out_shape=jax.ShapeDtypeStruct((batch, seq, v_w), BF16),
        scratch_shapes=[
            pltpu.VMEM((MLSTM_HEADS, MLSTM_QK_DIM, MLSTM_V_DIM), F32),
            pltpu.VMEM((MLSTM_HEADS, 8, MLSTM_QK_DIM), F32),
            pltpu.VMEM((MLSTM_HEADS, 8, 128), F32),
        ],
        compiler_params=pltpu.CompilerParams(
            dimension_semantics=("arbitrary", "arbitrary"), vmem_limit_bytes=VMEM_LIMIT_BYTES),
        name="mlstm",
    )(proj3, proj3, proj3, proj3, gcol3, grow, nw)


def _out_proj_kernel(x_ref, a_ref, m_ref, wa_ref, wm_ref, o_ref):
    y = jnp.dot(a_ref[...], wa_ref[...], preferred_element_type=F32)
    y = y + jnp.dot(m_ref[...], wm_ref[...], preferred_element_type=F32)
    o_ref[...] = x_ref[...] + y


def _out_proj(x2d, attn2d, mlstm2d, w_attn, w_mlstm):
    T, D = x2d.shape
    tm = ROW_TILE
    const = lambda i: (0, 0)
    return pl.pallas_call(
        _out_proj_kernel,
        grid=(T // tm,),
        in_specs=[
            pl.BlockSpec((tm, D), lambda i: (i, 0)),
            pl.BlockSpec((tm, attn2d.shape[1]), lambda i: (i, 0)),
            pl.BlockSpec((tm, mlstm2d.shape[1]), lambda i: (i, 0)),
            pl.BlockSpec(w_attn.shape, const),
            pl.BlockSpec(w_mlstm.shape, const),
        ],
        out_specs=pl.BlockSpec((tm, D), lambda i: (i, 0)),
        out_shape=jax.ShapeDtypeStruct((T, D), F32),
        compiler_params=pltpu.CompilerParams(
            dimension_semantics=("arbitrary",), vmem_limit_bytes=VMEM_LIMIT_BYTES),
        name="out_proj",
    )(x2d, attn2d, mlstm2d, w_attn, w_mlstm)


def _ffn_kernel(x_ref, wn_ref, wg_ref, wu_ref, wd_ref, o_ref, h_scr):
    f = pl.program_id(1)

    @pl.when(f == 0)
    def _():
        xf = x_ref[...]
        ms = jnp.mean(xf * xf, axis=-1, keepdims=True)
        h_scr[...] = ((xf * lax.rsqrt(ms + EPS)) * wn_ref[...]).astype(BF16)
        o_ref[...] = xf

    hb = h_scr[...]
    g = jnp.dot(hb, wg_ref[...], preferred_element_type=F32)
    u = jnp.dot(hb, wu_ref[...], preferred_element_type=F32)
    a = (g * _sigmoid(g)) * u
    o_ref[...] += jnp.dot(a.astype(BF16), wd_ref[...], preferred_element_type=F32)


def _ffn(x2d, wn, wg, wu, wd):
    T, D = x2d.shape
    F = wg.shape[1]
    tm, tf = ROW_TILE, FFN_COL_TILE
    return pl.pallas_call(
        _ffn_kernel,
        grid=(T // tm, F // tf),
        in_specs=[
            pl.BlockSpec((tm, D), lambda i, f: (i, 0)),
            pl.BlockSpec((1, D), lambda i, f: (0, 0)),
            pl.BlockSpec((D, tf), lambda i, f: (0, f)),
            pl.BlockSpec((D, tf), lambda i, f: (0, f)),
            pl.BlockSpec((tf, D), lambda i, f: (f, 0)),
        ],
        out_specs=pl.BlockSpec((tm, D), lambda i, f: (i, 0)),
        out_shape=jax.ShapeDtypeStruct((T, D), F32),
        scratch_shapes=[pltpu.VMEM((tm, D), BF16)],
        compiler_params=pltpu.CompilerParams(
            dimension_semantics=("arbitrary", "arbitrary"), vmem_limit_bytes=VMEM_LIMIT_BYTES),
        name="ffn",
    )(x2d, wn, wg, wu, wd)


def _ple_kernel(x_ref, p_ref, wn_ref, wgate_ref, wproj_ref, wpost_ref, o_ref):
    xf = x_ref[...]
    ms = jnp.mean(xf * xf, axis=-1, keepdims=True)
    hb = ((xf * lax.rsqrt(ms + EPS)) * wn_ref[...]).astype(BF16)
    gate = _sigmoid(jnp.dot(hb, wgate_ref[...], preferred_element_type=F32))
    e = jnp.dot(p_ref[...].astype(BF16), wproj_ref[...], preferred_element_type=F32)
    ems = jnp.mean(e * e, axis=-1, keepdims=True)
    e = (e * lax.rsqrt(ems + EPS)) * wpost_ref[...]
    o_ref[...] = xf + gate * e


def _ple(x2d, p2d, wn, wgate, wproj, wpost):
    T, D = x2d.shape
    P = p2d.shape[1]
    tm = ROW_TILE
    const = lambda i: (0, 0)
    return pl.pallas_call(
        _ple_kernel,
        grid=(T // tm,),
        in_specs=[
            pl.BlockSpec((tm, D), lambda i: (i, 0)),
            pl.BlockSpec((tm, P), lambda i: (i, 0)),
            pl.BlockSpec((1, D), const),
            pl.BlockSpec((D, D), const),
            pl.BlockSpec((P, D), const),
            pl.BlockSpec((1, D), const),
        ],
        out_specs=pl.BlockSpec((tm, D), lambda i: (i, 0)),
        out_shape=jax.ShapeDtypeStruct((T, D), F32),
        compiler_params=pltpu.CompilerParams(
            dimension_semantics=("arbitrary",), vmem_limit_bytes=VMEM_LIMIT_BYTES),
        name="ple",
    )(x2d, p2d, wn, wgate, wproj, wpost)


def _layer(x2d, p2d, batch, seq, w_norm_mix, w_in, fox_f_bias, q_norm_w, k_norm_w, mlstm_conv_w, mlstm_conv_b,
           mlstm_i_bias, mlstm_f_bias, mlstm_out_norm_w, w_out, w_norm_ffn, w_ffn_gate, w_ffn_up, w_ffn_down,
           w_norm_ple, w_ple_gate, w_ple_proj, w_ple_post_norm):
    D = x2d.shape[1]
    A, QK, MV = ATTN_WIDTH, MLSTM_QK_WIDTH, MLSTM_WIDTH
    o = 0
    cols = {}
    for name, width in (("aq", A), ("ak", A), ("av", A), ("af", ATTN_HEADS), ("mq", QK), ("mk", QK), ("mv", MV),
                        ("mi", MLSTM_HEADS), ("mf", MLSTM_HEADS), ("mo", MV)):
        cols[name] = w_in[:, o:o + width]
        o += width
    w_main = jnp.concatenate([cols[n] for n in ("aq", "ak", "av", "mq", "mk", "mv", "mo")], axis=1).astype(BF16)
    w_gate = jnp.concatenate([cols["af"], cols["mi"], cols["mf"]], axis=1)
    wgt = w_gate.T.astype(BF16)
    gbt = jnp.concatenate([fox_f_bias, mlstm_i_bias, mlstm_f_bias]).astype(F32)[:, None]
    cw = mlstm_conv_w.reshape(CONV_WIDTH, 2, QK).transpose(1, 0, 2)
    cb = mlstm_conv_b.reshape(2, 1, QK)

    proj, gcol, grow = _in_proj(x2d, w_norm_mix[None, :], w_main, wgt, gbt, q_norm_w[None, :],
                                k_norm_w[None, :], cw, cb, batch=batch, seq=seq)
    proj3 = proj.reshape(batch, seq, proj.shape[1])
    gcol3 = gcol.reshape(batch, seq, GATE_LANES)
    attn = _fox_attention(proj3, grow, gcol3, batch=batch, seq=seq)
    mlstm = _mlstm(proj3, gcol3, grow, mlstm_out_norm_w[None, :], batch=batch, seq=seq)

    x1 = _out_proj(x2d, attn.reshape(batch * seq, A), mlstm.reshape(batch * seq, MV),
                   w_out[:A].astype(BF16), w_out[A:].astype(BF16))
    x2 = _ffn(x1, w_norm_ffn[None, :], w_ffn_gate.astype(BF16), w_ffn_up.astype(BF16), w_ffn_down.astype(BF16))
    x3 = _ple(x2, p2d, w_norm_ple[None, :], w_ple_gate.astype(BF16), w_ple_proj.astype(BF16),
              w_ple_post_norm[None, :])
    return x3


def kernel(x, p, w_norm_mix, w_in, fox_f_bias, q_norm_w, k_norm_w, mlstm_conv_w, mlstm_conv_b, mlstm_i_bias,
           mlstm_f_bias, mlstm_out_norm_w, w_out, w_norm_ffn, w_ffn_gate, w_ffn_up, w_ffn_down, w_norm_ple,
           w_ple_gate, w_ple_proj, w_ple_post_norm):
    B, S, D = x.shape
    depth = w_in.shape[0]
    x2d = x.reshape(B * S, D)
    for i in range(depth):
        x2d = _layer(x2d, p[i].reshape(B * S, p.shape[-1]), B, S, w_norm_mix[i], w_in[i], fox_f_bias[i],
                     q_norm_w[i], k_norm_w[i], mlstm_conv_w[i], mlstm_conv_b[i], mlstm_i_bias[i], mlstm_f_bias[i],
                     mlstm_out_norm_w[i], w_out[i], w_norm_ffn[i], w_ffn_gate[i], w_ffn_up[i], w_ffn_down[i],
                     w_norm_ple[i], w_ple_gate[i], w_ple_proj[i], w_ple_post_norm[i])
    return x2d.reshape(B, S, D)
```

```python
import functools
import math

import jax
import jax.numpy as jnp
from jax import lax
from jax.experimental import pallas as pl
from jax.experimental.pallas import tpu as pltpu

F32 = jnp.float32
BF16 = jnp.bfloat16
EPS = 1e-6

ATTN_HEADS = 8
ATTN_HEAD_DIM = 128
MLSTM_HEADS = 4
MLSTM_QK_DIM = 128
MLSTM_V_DIM = 256
CONV_WIDTH = 4
ATTN_WIDTH = ATTN_HEADS * ATTN_HEAD_DIM
MLSTM_QK_WIDTH = MLSTM_HEADS * MLSTM_QK_DIM
MLSTM_WIDTH = MLSTM_HEADS * MLSTM_V_DIM
N_GATES = ATTN_HEADS + 2 * MLSTM_HEADS
GATE_LANES = 128
CONV_HALO = 8

VMEM_LIMIT_BYTES = 56 * 1024 * 1024

ROW_TILE = 512
PROJ_COL_TILE = 512
MLSTM_CHUNK = 256
ATTN_Q_TILE = 1024
ATTN_KV_TILE = 512
ATTN_HEADS_PER_STEP = 2
LOG2E = math.log2(math.e)
FOX_MAX_BOUND = 40.0
FFN_COL_TILE = 512

_T_AQ, _T_AK, _T_AV, _T_MQ, _T_MK, _T_MV, _T_MO, _T_END = 0, 2, 4, 6, 7, 8, 10, 12


def _log_sigmoid(z):
    return jnp.minimum(z, 0.0) - jnp.log1p(jnp.exp(-jnp.abs(z)))


def _sigmoid(z):
    return 1.0 / (1.0 + jnp.exp(-z))


def _split3(v):
    hi = v.astype(BF16)
    r1 = v - hi.astype(F32)
    mid = r1.astype(BF16)
    lo = (r1 - mid.astype(F32)).astype(BF16)
    return hi, mid, lo


def _lane_prefix_sum(v, period):
    axis = v.ndim - 1
    pos = lax.broadcasted_iota(jnp.int32, v.shape, axis) & (period - 1)
    shift = 1
    while shift < period:
        v = v + jnp.where(pos >= shift, pltpu.roll(v, shift, axis=axis), 0.0)
        shift *= 2
    return v


def _in_proj_kernel(x_ref, wn_ref, wa_ref, wm_ref, wo_ref, wgt_ref, gbt_ref, qn_ref, kn_ref, cw_ref, cb_ref,
                    proj_ref, gcol_ref, grow_ref, h_scr, carry_scr, conv_scr, *, tiles_per_seq, chunk):
    i = pl.program_id(0)
    j = pl.program_id(1)
    tm = x_ref.shape[0]
    tn = PROJ_COL_TILE
    group = proj_ref.shape[1]
    tiles_per_group = group // tn
    seq_start = (i % tiles_per_seq) == 0

    def tile_acc(t):
        w_ref, t0 = (wa_ref, _T_AQ) if t < _T_MQ else (wm_ref, _T_MQ) if t < _T_MO else (wo_ref, _T_MO)
        return jnp.dot(h_scr[...], w_ref[:, (t - t0) * tn:(t - t0 + 1) * tn], preferred_element_type=F32)

    def store(t, val):
        c0 = (t % tiles_per_group) * tn
        proj_ref[:, c0:c0 + tn] = val.astype(BF16)

    def head_norm(acc, w):
        outs = []
        for hh in range(tn // ATTN_HEAD_DIM):
            a = acc[:, hh * ATTN_HEAD_DIM:(hh + 1) * ATTN_HEAD_DIM]
            ms = jnp.mean(a * a, axis=-1, keepdims=True)
            outs.append((a * lax.rsqrt(ms + EPS)) * w)
        return jnp.concatenate(outs, axis=-1)

    def conv_silu(acc, which, scale):
        buf = conv_scr.at[which]
        buf[0:CONV_HALO, :] = jnp.where(seq_start, 0.0, buf[tm:tm + CONV_HALO, :])
        buf[CONV_HALO:CONV_HALO + tm, :] = acc
        cw = cw_ref[which]
        y = cb_ref[which] + cw[CONV_WIDTH - 1:CONV_WIDTH, :] * acc
        for tap in range(CONV_WIDTH - 1):
            off = CONV_HALO - (CONV_WIDTH - 1) + tap
            y = y + cw[tap:tap + 1, :] * buf[off:off + tm, :]
        y = y * _sigmoid(y)
        return y * scale if scale != 1.0 else y

    @pl.when((i == 0) & (j == 0))
    def _():
        carry_scr[...] = jnp.zeros_like(carry_scr)
        conv_scr[...] = jnp.zeros_like(conv_scr)

    @pl.when(j == 0)
    def _():
        xf = x_ref[...]
        ms = jnp.mean(xf * xf, axis=-1, keepdims=True)
        hb = ((xf * lax.rsqrt(ms + EPS)) * wn_ref[...]).astype(BF16)
        h_scr[...] = hb

        gr = lax.dot_general(wgt_ref[...], hb, (((1,), (1,)), ((), ())), preferred_element_type=F32) + gbt_ref[...]
        row = lax.broadcasted_iota(jnp.int32, gr.shape, 0)
        is_in_gate = (row >= ATTN_HEADS) & (row < ATTN_HEADS + MLSTM_HEADS)
        gr = jnp.where(is_in_gate, gr, _log_sigmoid(gr))
        cs_chunk = _lane_prefix_sum(gr, chunk)
        cs_full = _lane_prefix_sum(gr, tm)
        carry = jnp.where(seq_start, 0.0, carry_scr[...])
        cs_full = cs_full + carry[:, :1]
        carry_scr[...] = jnp.broadcast_to(cs_full[:, tm - 1:tm], carry_scr.shape)
        gates = jnp.where(row < ATTN_HEADS, cs_full * LOG2E, jnp.where(is_in_gate, gr, cs_chunk))
        grow_ref[...] = gates
        padded = jnp.concatenate([gates, jnp.zeros((GATE_LANES - gates.shape[0], tm), F32)], axis=0)
        gcol_ref[...] = padded.T

        qw = qn_ref[...] * (LOG2E * ATTN_HEAD_DIM ** -0.5)
        for t in range(_T_AQ, _T_AK):
            store(t, head_norm(tile_acc(t), qw))
        for t in range(_T_AK, _T_AV):
            store(t, head_norm(tile_acc(t), kn_ref[...]))

    @pl.when(j == 1)
    def _():
        store(_T_MQ, conv_silu(tile_acc(_T_MQ), 0, MLSTM_QK_DIM ** -0.5))
        store(_T_MK, conv_silu(tile_acc(_T_MK), 1, 1.0))
        for t in range(_T_AV, _T_MQ):
            store(t, tile_acc(t))

    @pl.when(j == 2)
    def _():
        for t in range(_T_MV, _T_END):
            store(t, tile_acc(t))


def _in_proj(x2d, wn, w_attn, w_mlstm, w_ogate, wgt, gbt, qn, kn, cw, cb, *, batch, seq):
    T, D = x2d.shape
    tm, tn = ROW_TILE, PROJ_COL_TILE
    n_cols = w_attn.shape[1] + w_mlstm.shape[1] + w_ogate.shape[1]
    assert w_attn.shape[1] == (_T_MQ - _T_AQ) * tn and w_mlstm.shape[1] == (_T_MO - _T_MQ) * tn
    n_groups = 3
    group = n_cols // n_groups
    assert group == (_T_AV - _T_AQ) * tn == (_T_MV - _T_AV) * tn == (_T_END - _T_MV) * tn
    tiles_per_seq = seq // tm
    const = lambda i, j: (0, 0)
    const3 = lambda i, j: (0, 0, 0)
    kern = functools.partial(_in_proj_kernel, tiles_per_seq=tiles_per_seq, chunk=MLSTM_CHUNK)
    return pl.pallas_call(
        kern,
        grid=(T // tm, n_groups),
        in_specs=[
            pl.BlockSpec((tm, D), lambda i, j: (i, 0)),
            pl.BlockSpec((1, D), const),
            pl.BlockSpec(w_attn.shape, const, pipeline_mode=pl.Buffered(1)),
            pl.BlockSpec(w_mlstm.shape, const, pipeline_mode=pl.Buffered(1)),
            pl.BlockSpec(w_ogate.shape, const, pipeline_mode=pl.Buffered(1)),
            pl.BlockSpec((N_GATES, D), const),
            pl.BlockSpec((N_GATES, 1), const),
            pl.BlockSpec((1, ATTN_HEAD_DIM), const),
            pl.BlockSpec((1, ATTN_HEAD_DIM), const),
            pl.BlockSpec((2, CONV_WIDTH, tn), const3),
            pl.BlockSpec((2, 1, tn), const3),
        ],
        out_specs=[
            pl.BlockSpec((tm, group), lambda i, j: (i, j)),
            pl.BlockSpec((tm, GATE_LANES), lambda i, j: (i, 0)),
            pl.BlockSpec((None, N_GATES, tm), lambda i, j: (i // tiles_per_seq, 0, i % tiles_per_seq)),
        ],
        out_shape=[
            jax.ShapeDtypeStruct((T, n_cols), BF16),
            jax.ShapeDtypeStruct((T, GATE_LANES), F32),
            jax.ShapeDtypeStruct((batch, N_GATES, seq), F32),
        ],
        scratch_shapes=[
            pltpu.VMEM((tm, D), BF16),
            pltpu.VMEM((N_GATES, GATE_LANES), F32),
            pltpu.VMEM((2, CONV_HALO + tm, tn), F32),
        ],
        compiler_params=pltpu.CompilerParams(
            dimension_semantics=("arbitrary", "arbitrary"), vmem_limit_bytes=VMEM_LIMIT_BYTES),
        name="in_proj",
    )(x2d, wn, w_attn, w_mlstm, w_ogate, wgt, gbt, qn, kn, cw, cb)


def _fox_kernel(q_ref, k_ref, v_ref, c_ref, ccol_ref, o_ref, kmax_scr, *, tk):
    hg = pl.program_id(1)
    qi = pl.program_id(2)
    tq = q_ref.shape[0]
    d = ATTN_HEAD_DIM
    n_heads = q_ref.shape[1] // d
    n_diag = tq // tk
    qs = [q_ref[:, g * d:(g + 1) * d] for g in range(n_heads)]

    @pl.when(qi == 0)
    def _():
        for g in range(n_heads):
            kf = k_ref[:, g * d:(g + 1) * d].astype(F32)
            k2 = jnp.max(jnp.sum(kf * kf, axis=-1, keepdims=True), axis=0, keepdims=True)
            kmax_scr[g] = jnp.broadcast_to(k2, kmax_scr.shape[1:])

    lane = lax.broadcasted_iota(jnp.int32, ccol_ref.shape, 1)
    ccol = ccol_ref[...]
    mis, bounds = [], []
    for g in range(n_heads):
        qf = qs[g].astype(F32)
        bound = jnp.sqrt(jnp.sum(qf * qf, axis=-1, keepdims=True) * kmax_scr[g][:1, :1])
        ci = jnp.sum(jnp.where(lane == hg * n_heads + g, ccol, 0.0), axis=-1, keepdims=True)
        mis.append(bound - ci)
        bounds.append(jnp.max(bound))
    worst = functools.reduce(jnp.maximum, bounds)

    def load_block(j, g):
        start = pl.multiple_of(j * tk, tk)
        k = k_ref[pl.ds(start, tk), g * d:(g + 1) * d]
        v = v_ref[pl.ds(start, tk), g * d:(g + 1) * d]
        cj = c_ref[pl.ds(hg * n_heads + g, 1), pl.ds(start, tk)]
        return k, v, cj

    def causal_mask(r0):
        r = lax.broadcasted_iota(jnp.int32, (tq - r0, tk), 0)
        c = lax.broadcasted_iota(jnp.int32, (tq - r0, tk), 1)
        return c <= r

    def rejoin(old, new, r0):
        return jnp.concatenate([old[:r0], new], axis=0) if r0 else new

    def finish(carry):
        for g in range(n_heads):
            l, acc = carry[g][-2:]
            l = jnp.sum(l, axis=-1, keepdims=True)
            o_ref[:, g * d:(g + 1) * d] = (acc / l).astype(o_ref.dtype)

    def scores(j, g, r0):
        k = load_block(j, g)[0]
        return lax.dot_general(qs[g][r0:], k, (((1,), (1,)), ((), ())), preferred_element_type=F32)

    def consume(j, g, s, l, acc, diag):
        r0 = 0 if diag is None else diag * tk
        _, v, cj = load_block(j, g)
        e = (s - cj) - mis[g][r0:]
        if diag is not None:
            e = jnp.where(causal_mask(r0), e, -jnp.inf)
        p = jnp.exp2(e)
        l_new = l[r0:] + functools.reduce(jnp.add, [p[:, c0:c0 + d] for c0 in range(0, tk, d)])
        acc_new = acc[r0:] + jnp.dot(p.astype(BF16), v, preferred_element_type=F32)
        return rejoin(l, l_new, r0), rejoin(acc, acc_new, r0)

    def bounded_step(j, carry, diag):
        r0 = 0 if diag is None else diag * tk
        return tuple(consume(j, g, scores(j, g, r0), *carry[g], diag) for g in range(n_heads))


    def online_step(j, carry, diag):
        r0 = 0 if diag is None else diag * tk
        out = []
        for g in range(n_heads):
            m, l, acc = carry[g]
            k, v, cj = load_block(j, g)
            s = lax.dot_general(qs[g][r0:], k, (((1,), (1,)), ((), ())), preferred_element_type=F32) - cj
            if diag is not None:
                s = jnp.where(causal_mask(r0), s, -jnp.inf)
            m_new = jnp.maximum(m[r0:], jnp.max(s, axis=-1, keepdims=True))
            alpha = jnp.exp2(m[r0:] - m_new)
            p = jnp.exp2(s - m_new)
            l_new = alpha * l[r0:] + jnp.sum(p, axis=-1, keepdims=True)
            acc_new = alpha * acc[r0:] + jnp.dot(p.astype(BF16), v, preferred_element_type=F32)
            out.append((rejoin(m, m_new, r0), rejoin(l, l_new, r0), rejoin(acc, acc_new, r0)))
        return tuple(out)

    def run(step, init):
        carry = lax.fori_loop(0, qi * n_diag, lambda j, c: step(j, c, None), init)
        for jj in range(n_diag):
            carry = step(qi * n_diag + jj, carry, jj)
        finish(carry)

    zeros = (jnp.zeros((tq, 1), F32), jnp.zeros((tq, d), F32))

    @pl.when(worst <= FOX_MAX_BOUND)
    def _():
        run(bounded_step, tuple((jnp.zeros((tq, d), F32), jnp.zeros((tq, d), F32)) for _ in range(n_heads)))

    @pl.when(jnp.logical_not(worst <= FOX_MAX_BOUND))
    def _():
        run(online_step, tuple((jnp.full((tq, 1), -jnp.inf, F32),) + zeros for _ in range(n_heads)))


def _fox_attention(proj3, grow, gcol3, *, batch, seq):
    tq, tk = ATTN_Q_TILE, ATTN_KV_TILE
    assert tq % tk == 0
    dd =ATTN_HEAD_DIM * ATTN_HEADS_PER_STEP
    n_groups = ATTN_HEADS // ATTN_HEADS_PER_STEP
    grid = (batch, n_groups, seq // tq)
    return pl.pallas_call(
        functools.partial(_fox_kernel, tk=tk),
        grid=grid,
        in_specs=[
            pl.BlockSpec((None, tq, dd), lambda b, h, qi: (b, qi, h)),
            pl.BlockSpec((None, seq, dd), lambda b, h, qi: (b, 0, n_groups + h)),
            pl.BlockSpec((None, seq, dd), lambda b, h, qi: (b, 0, 2 * n_groups + h)),
            pl.BlockSpec((None, N_GATES, seq), lambda b, h, qi: (b, 0, 0)),
            pl.BlockSpec((None, tq, GATE_LANES), lambda b, h, qi: (b, qi, 0)),
        ],
        out_specs=pl.BlockSpec((None, tq, dd), lambda b, h, qi: (b, qi, h)),
        out_shape=jax.ShapeDtypeStruct((batch, seq, ATTN_WIDTH), BF16),
        scratch_shapes=[pltpu.VMEM((ATTN_HEADS_PER_STEP, 8, 128), F32)],
        compiler_params=pltpu.CompilerParams(
            dimension_semantics=("arbitrary", "arbitrary", "arbitrary"), vmem_limit_bytes=VMEM_LIMIT_BYTES),
        name="fox_attention",
    )(proj3, proj3, proj3, grow, gcol3)


def _mlstm_kernel(q_ref, k_ref, v_ref, og_ref, gcol_ref, grow_ref, nw_ref, o_ref, c_scr, n_scr, m_scr):
    ci = pl.program_id(1)
    L = q_ref.shape[0]
    dk, dv = MLSTM_QK_DIM, MLSTM_V_DIM

    @pl.when(ci == 0)
    def _():
        c_scr[...] = jnp.zeros_like(c_scr)
        n_scr[...] = jnp.zeros_like(n_scr)
        m_scr[...] = jnp.zeros_like(m_scr)

    t_idx = lax.broadcasted_iota(jnp.int32, (L, L), 0)
    s_idx = lax.broadcasted_iota(jnp.int32, (L, L), 1)
    causal = s_idx <= t_idx
    gcol = gcol_ref[...]
    grow = grow_ref[...]

    heads = range(MLSTM_HEADS)
    nt = (((1,), (1,)), ((), ()))
    q = [q_ref[:, hh * dk:(hh + 1) * dk] for hh in heads]
    k = [k_ref[:, hh * dk:(hh + 1) * dk] for hh in heads]
    v = [v_ref[:, hh * dv:(hh + 1) * dv] for hh in heads]
    gi = [ATTN_HEADS + hh for hh in heads]
    gf = [ATTN_HEADS + MLSTM_HEADS + hh for hh in heads]
    b_row = [grow[gf[hh]:gf[hh] + 1, :] for hh in heads]
    b_col = [gcol[:, gf[hh]:gf[hh] + 1] for hh in heads]
    a_row = [grow[gi[hh]:gi[hh] + 1, :] - b_row[hh] for hh in heads]
    a_col = [gcol[:, gi[hh]:gi[hh] + 1] - b_col[hh] for hh in heads]
    m_prev = [m_scr[hh][:1, :1] for hh in heads]
    ct = [c_scr[hh] for hh in heads]
    n = [n_scr[hh][:1, :] for hh in heads]

    qk = [lax.dot_general(q[hh], k[hh], nt, preferred_element_type=F32) for hh in heads]
    qc = [jnp.dot(q[hh], ct[hh].astype(BF16), preferred_element_type=F32) for hh in heads]
    big_m = [jnp.maximum(jnp.max(jnp.where(causal, a_row[hh], -jnp.inf), axis=-1, keepdims=True), m_prev[hh])
             for hh in heads]
    w_intra = [jnp.exp(jnp.where(causal, a_row[hh] - big_m[hh], -jnp.inf)) for hh in heads]
    w_inter = [jnp.exp(m_prev[hh] - big_m[hh]) for hh in heads]
    s_qk = [qk[hh] * w_intra[hh] for hh in heads]
    num = [jnp.dot(s_qk[hh].astype(BF16), v[hh], preferred_element_type=F32) + w_inter[hh] * qc[hh]
           for hh in heads]
    qn = [jnp.sum(q[hh].astype(F32) * n[hh], axis=-1, keepdims=True) for hh in heads]
    den = [jnp.sum(s_qk[hh], axis=-1, keepdims=True) + w_inter[hh] * qn[hh] for hh in heads]
    dmax = [jnp.maximum(jnp.abs(den[hh]), jnp.exp(-(b_col[hh] + big_m[hh]))) for hh in heads]
    ms = [jnp.mean(num[hh] * num[hh], axis=-1, keepdims=True) for hh in heads]
    for hh in heads:
        hn = num[hh] * lax.rsqrt(ms[hh] + EPS * (dmax[hh] * dmax[hh]))
        og = og_ref[:, hh * dv:(hh + 1) * dv].astype(F32)
        out = (hn * nw_ref[:, hh * dv:(hh + 1) * dv]) * _sigmoid(og)
        o_ref[:, hh * dv:(hh + 1) * dv] = out.astype(o_ref.dtype)

    m_last = [jnp.maximum(jnp.max(a_row[hh], axis=-1, keepdims=True), m_prev[hh]) for hh in heads]
    kw = [k[hh].astype(F32) * jnp.exp(a_col[hh] - m_last[hh]) for hh in heads]
    upd = [lax.dot_general(kw[hh].astype(BF16), v[hh], (((0,), (0,)), ((), ())), preferred_element_type=F32)
           for hh in heads]
    for hh in heads:
        wc = jnp.exp(m_prev[hh] - m_last[hh])
        c_scr[hh] = wc * ct[hh] + upd[hh]
        n_new = wc * n[hh] + jnp.sum(kw[hh], axis=0, keepdims=True)
        n_scr[hh] = jnp.broadcast_to(n_new, n_scr.shape[1:])
        m_scr[hh] = jnp.broadcast_to(b_row[hh][:, L - 1:L] + m_last[hh], m_scr.shape[1:])


def _mlstm(proj3, gcol3, grow, nw, *, batch, seq):
    L = MLSTM_CHUNK
    grid = (batch, seq // L)
    qk_w, v_w = MLSTM_QK_WIDTH, MLSTM_WIDTH
    base = 3 * ATTN_WIDTH
    return pl.pallas_call(
        _mlstm_kernel,
        grid=grid,
        in_specs=[
            pl.BlockSpec((None, L, qk_w), lambda b, c: (b, c, base // qk_w)),
            pl.BlockSpec((None, L, qk_w), lambda b, c: (b, c, base // qk_w + 1)),
            pl.BlockSpec((None, L, v_w), lambda b, c: (b, c, (base + 2 * qk_w) // v_w)),
            pl.BlockSpec((None, L, v_w), lambda b, c: (b, c, (base + 2 * qk_w) // v_w + 1)),
            pl.BlockSpec((None, L, GATE_LANES), lambda b, c: (b, c, 0)),
            pl.BlockSpec((None, N_GATES, L), lambda b, c: (b, 0, c)),
            pl.BlockSpec((1, v_w), lambda b, c: (0, 0)),
        ],
        out_specs=pl.BlockSpec((None, L, v_w), lambda b, c: (b, c, 0)),
        out_shape=jax.ShapeDtypeStruct((batch, seq, v_w), BF16),
        scratch_shapes=[
            pltpu.VMEM((MLSTM_HEADS, MLSTM_QK_DIM, MLSTM_V_DIM), F32),
            pltpu.VMEM((MLSTM_HEADS, 8, MLSTM_QK_DIM), F32),
            pltpu.VMEM((MLSTM_HEADS, 8, 128), F32),
        ],
        compiler_params=pltpu.CompilerParams(
            dimension_semantics=("arbitrary", "arbitrary"), vmem_limit_bytes=VMEM_LIMIT_BYTES),
        name="mlstm",
    )(proj3, proj3, proj3, proj3, gcol3, grow, nw)


def _out_proj_kernel(x_ref, a_ref, m_ref, wa_ref, wm_ref, o_ref):
    y = jnp.dot(a_ref[...], wa_ref[...], preferred_element_type=F32)
    y = y + jnp.dot(m_ref[...], wm_ref[...], preferred_element_type=F32)
    o_ref[...] = x_ref[...] + y


def _out_proj(x2d, attn2d, mlstm2d, w_attn, w_mlstm):
    T, D = x2d.shape
    tm = ROW_TILE
    const = lambda i: (0, 0)
    return pl.pallas_call(
        _out_proj_kernel,
        grid=(T // tm,),
        in_specs=[
            pl.BlockSpec((tm, D), lambda i: (i, 0)),
            pl.BlockSpec((tm, attn2d.shape[1]), lambda i: (i, 0)),
            pl.BlockSpec((tm, mlstm2d.shape[1]), lambda i: (i, 0)),
            pl.BlockSpec(w_attn.shape, const),
            pl.BlockSpec(w_mlstm.shape, const),
        ],
        out_specs=pl.BlockSpec((tm, D), lambda i: (i, 0)),
        out_shape=jax.ShapeDtypeStruct((T, D), F32),
        compiler_params=pltpu.CompilerParams(
            dimension_semantics=("arbitrary",), vmem_limit_bytes=VMEM_LIMIT_BYTES),
        name="out_proj",
    )(x2d, attn2d, mlstm2d, w_attn, w_mlstm)


def _ffn_kernel(x_ref, wn_ref, wg_ref, wu_ref, wd_ref, o_ref, h_scr):
    f = pl.program_id(1)

    @pl.when(f == 0)
    def _():
        xf = x_ref[...]
        ms = jnp.mean(xf * xf, axis=-1, keepdims=True)
        h_scr[...] = ((xf * lax.rsqrt(ms + EPS)) * wn_ref[...]).astype(BF16)
        o_ref[...] = xf

    hb = h_scr[...]
    g = jnp.dot(hb, wg_ref[...], preferred_element_type=F32)
    u = jnp.dot(hb, wu_ref[...], preferred_element_type=F32)
    a = (g * _sigmoid(g)) * u
    o_ref[...] += jnp.dot(a.astype(BF16), wd_ref[...], preferred_element_type=F32)


def _ffn(x2d, wn, wg, wu, wd):
    T, D = x2d.shape
    F = wg.shape[1]
    tm, tf = ROW_TILE, FFN_COL_TILE
    return pl.pallas_call(
        _ffn_kernel,
        grid=(T // tm, F // tf),
        in_specs=[
            pl.BlockSpec((tm, D), lambda i, f: (i, 0)),
            pl.BlockSpec((1, D), lambda i, f: (0, 0)),
            pl.BlockSpec((D, tf), lambda i, f: (0, f)),
            pl.BlockSpec((D, tf), lambda i, f: (0, f)),
            pl.BlockSpec((tf, D), lambda i, f: (f, 0)),
        ],
        out_specs=pl.BlockSpec((tm, D), lambda i, f: (i, 0)),
        out_shape=jax.ShapeDtypeStruct((T, D), F32),
        scratch_shapes=[pltpu.VMEM((tm, D), BF16)],
        compiler_params=pltpu.CompilerParams(
            dimension_semantics=("arbitrary", "arbitrary"), vmem_limit_bytes=VMEM_LIMIT_BYTES),
        name="ffn",
    )(x2d, wn, wg, wu, wd)


def _ple_kernel(x_ref, p_ref, wn_ref, wgate_ref, wproj_ref, wpost_ref, o_ref):
    xf = x_ref[...]
    ms = jnp.mean(xf * xf, axis=-1, keepdims=True)
    hb = ((xf * lax.rsqrt(ms + EPS)) * wn_ref[...]).astype(BF16)
    e = jnp.dot(p_ref[...].astype(BF16), wproj_ref[...], preferred_element_type=F32)
    ems = jnp.mean(e * e, axis=-1, keepdims=True)
    e = (e * lax.rsqrt(ems + EPS)) * wpost_ref[...]
    tn = PROJ_COL_TILE
    for c0 in range(0, o_ref.shape[1], tn):
        gate = _sigmoid(jnp.dot(hb, wgate_ref[:, c0:c0 + tn], preferred_element_type=F32))
        o_ref[:, c0:c0 + tn] = xf[:, c0:c0 + tn] + gate * e[:, c0:c0 + tn]


def _ple(x2d, p2d, wn, wgate, wproj, wpost):
    T, D = x2d.shape
    P = p2d.shape[1]
    tm = ROW_TILE
    const = lambda i: (0, 0)
    return pl.pallas_call(
        _ple_kernel,
        grid=(T // tm,),
        in_specs=[
            pl.BlockSpec((tm, D), lambda i: (i, 0)),
            pl.BlockSpec((tm, P), lambda i: (i, 0)),
            pl.BlockSpec((1, D), const),
            pl.BlockSpec((D, D), const),
            pl.BlockSpec((P, D), const),
            pl.BlockSpec((1, D), const),
        ],
        out_specs=pl.BlockSpec((tm, D), lambda i: (i, 0)),
        out_shape=jax.ShapeDtypeStruct((T, D), F32),
        compiler_params=pltpu.CompilerParams(
            dimension_semantics=("arbitrary",), vmem_limit_bytes=VMEM_LIMIT_BYTES),
        name="ple",
    )(x2d, p2d, wn, wgate, wproj, wpost)


def _layer(x2d, p2d, batch, seq, w_norm_mix, w_in, fox_f_bias, q_norm_w, k_norm_w, mlstm_conv_w, mlstm_conv_b,
           mlstm_i_bias, mlstm_f_bias, mlstm_out_norm_w, w_out, w_norm_ffn, w_ffn_gate, w_ffn_up, w_ffn_down,
           w_norm_ple, w_ple_gate, w_ple_proj, w_ple_post_norm):
    D = x2d.shape[1]
    A, QK, MV = ATTN_WIDTH, MLSTM_QK_WIDTH, MLSTM_WIDTH
    o_af = 3 * A
    o_m = o_af + ATTN_HEADS
    o_mi = o_m + 2 * QK + MV
    o_mo = o_mi + 2 * MLSTM_HEADS
    w_attn = w_in[:, :o_af].astype(BF16)
    w_mlstm = w_in[:, o_m:o_mi].astype(BF16)
    w_ogate = w_in[:, o_mo:].astype(BF16)
    w_gate = jnp.concatenate([w_in[:, o_af:o_m], w_in[:, o_mi:o_mo]], axis=1)
    wgt = w_gate.T.astype(BF16)
    gbt = jnp.concatenate([fox_f_bias, mlstm_i_bias, mlstm_f_bias]).astype(F32)[:, None]
    cw = mlstm_conv_w.reshape(CONV_WIDTH, 2, QK).transpose(1, 0, 2)
    cb = mlstm_conv_b.reshape(2, 1, QK)

    proj, gcol, grow = _in_proj(x2d, w_norm_mix[None, :], w_attn, w_mlstm, w_ogate, wgt, gbt, q_norm_w[None, :],
                                k_norm_w[None, :], cw, cb, batch=batch, seq=seq)
    proj3 = proj.reshape(batch, seq, proj.shape[1])
    gcol3 = gcol.reshape(batch, seq, GATE_LANES)
    attn = _fox_attention(proj3, grow, gcol3, batch=batch, seq=seq)
    mlstm = _mlstm(proj3, gcol3, grow, mlstm_out_norm_w[None, :], batch=batch, seq=seq)

    x1 = _out_proj(x2d, attn.reshape(batch * seq, A), mlstm.reshape(batch * seq, MV),
                   w_out[:A].astype(BF16), w_out[A:].astype(BF16))
    x2 = _ffn(x1, w_norm_ffn[None, :], w_ffn_gate.astype(BF16), w_ffn_up.astype(BF16), w_ffn_down.astype(BF16))
    x3 = _ple(x2, p2d, w_norm_ple[None, :], w_ple_gate.astype(BF16), w_ple_proj.astype(BF16),
              w_ple_post_norm[None, :])
    return x3


def kernel(x, p, w_norm_mix, w_in, fox_f_bias, q_norm_w, k_norm_w, mlstm_conv_w, mlstm_conv_b, mlstm_i_bias,
           mlstm_f_bias, mlstm_out_norm_w, w_out, w_norm_ffn, w_ffn_gate, w_ffn_up, w_ffn_down, w_norm_ple,
           w_ple_gate, w_ple_proj, w_ple_post_norm):
    B, S, D = x.shape
    depth = w_in.shape[0]
    x2d = x.reshape(B * S, D)
    for i in range(depth):
        x2d = _layer(x2d, p[i].reshape(B * S, p.shape[-1]), B, S, w_norm_mix[i], w_in[i], fox_f_bias[i],
                     q_norm_w[i], k_norm_w[i], mlstm_conv_w[i], mlstm_conv_b[i], mlstm_i_bias[i], mlstm_f_bias[i],
                     mlstm_out_norm_w[i], w_out[i], w_norm_ffn[i], w_ffn_gate[i], w_ffn_up[i], w_ffn_down[i],
                     w_norm_ple[i], w_ple_gate[i], w_ple_proj[i], w_ple_post_norm[i])
    return x2d.reshape(B, S, D)
```

```python
import functools
import math

import jax
import jax.numpy as jnp
from jax import lax
from jax.experimental import pallas as pl
from jax.experimental.pallas import tpu as pltpu

F32 = jnp.float32
BF16 = jnp.bfloat16
EPS = 1e-6

ATTN_HEADS = 8
ATTN_HEAD_DIM = 128
MLSTM_HEADS = 4
MLSTM_QK_DIM = 128
MLSTM_V_DIM = 256
CONV_WIDTH = 4
ATTN_WIDTH = ATTN_HEADS * ATTN_HEAD_DIM
MLSTM_QK_WIDTH = MLSTM_HEADS * MLSTM_QK_DIM
MLSTM_WIDTH = MLSTM_HEADS * MLSTM_V_DIM
N_GATES = ATTN_HEADS + 2 * MLSTM_HEADS
GATE_LANES = 128
CONV_HALO = 8

VMEM_LIMIT_BYTES = 56 * 1024 * 1024

ROW_TILE = 512
PROJ_COL_TILE = 512
MLSTM_CHUNK = 256
ATTN_Q_TILE = 1024
ATTN_KV_TILE = 512
ATTN_HEADS_PER_STEP = 2
LOG2E = math.log2(math.e)
FOX_MAX_BOUND = 40.0
FFN_COL_TILE = 512

_T_AQ, _T_AK, _T_AV, _T_MQ, _T_MK, _T_MV, _T_MO, _T_END = 0, 2, 4, 6, 7, 8, 10, 12


def _log_sigmoid(z):
    return jnp.minimum(z, 0.0) - jnp.log1p(jnp.exp(-jnp.abs(z)))


def _sigmoid(z):
    return 1.0 / (1.0 + jnp.exp(-z))


def _split3(v):
    hi = v.astype(BF16)
    r1 = v - hi.astype(F32)
    mid = r1.astype(BF16)
    lo = (r1 - mid.astype(F32)).astype(BF16)
    return hi, mid, lo


def _lane_prefix_sum(v, period):
    axis = v.ndim - 1
    pos = lax.broadcasted_iota(jnp.int32, v.shape, axis) & (period - 1)
    shift = 1
    while shift < period:
        v = v + jnp.where(pos >= shift, pltpu.roll(v, shift, axis=axis), 0.0)
        shift *= 2
    return v


def _in_proj_kernel(x_ref, wn_ref, wa_ref, wm_ref, wo_ref, wgt_ref, gbt_ref, qn_ref, kn_ref, cw_ref, cb_ref,
                    proj_ref, gcol_ref, grow_ref, h_scr, carry_scr, conv_scr, *, tiles_per_seq, chunk):
    i = pl.program_id(0)
    j = pl.program_id(1)
    tm = x_ref.shape[0]
    tn = PROJ_COL_TILE
    group = proj_ref.shape[1]
    tiles_per_group = group // tn
    seq_start = (i % tiles_per_seq) == 0

    def tile_acc(t):
        w_ref, t0 = (wa_ref, _T_AQ) if t < _T_MQ else (wm_ref, _T_MQ) if t < _T_MO else (wo_ref, _T_MO)
        return lax.dot_general(h_scr[...], w_ref[(t - t0) * tn:(t - t0 + 1) * tn, :], (((1,), (1,)), ((), ())),
                               preferred_element_type=F32)

    def store(t, val):
        c0 = (t % tiles_per_group) * tn
        proj_ref[:, c0:c0 + tn] = val.astype(BF16)

    def head_norm(acc, w):
        outs = []
        for hh in range(tn // ATTN_HEAD_DIM):
            a = acc[:, hh * ATTN_HEAD_DIM:(hh + 1) * ATTN_HEAD_DIM]
            ms = jnp.mean(a * a, axis=-1, keepdims=True)
            outs.append((a * lax.rsqrt(ms + EPS)) * w)
        return jnp.concatenate(outs, axis=-1)

    def conv_silu(acc, which, scale):
        buf = conv_scr.at[which]
        buf[0:CONV_HALO, :] = jnp.where(seq_start, 0.0, buf[tm:tm + CONV_HALO, :])
        buf[CONV_HALO:CONV_HALO + tm, :] = acc
        cw = cw_ref[which]
        y = cb_ref[which] + cw[CONV_WIDTH - 1:CONV_WIDTH, :] * acc
        for tap in range(CONV_WIDTH - 1):
            off = CONV_HALO - (CONV_WIDTH - 1) + tap
            y = y + cw[tap:tap + 1, :] * buf[off:off + tm, :]
        y = y * _sigmoid(y)
        return y * scale if scale != 1.0 else y

    @pl.when((i == 0) & (j == 0))
    def _():
        carry_scr[...] = jnp.zeros_like(carry_scr)
        conv_scr[...] = jnp.zeros_like(conv_scr)

    @pl.when(j == 0)
    def _():
        xf = x_ref[...]
        ms = jnp.mean(xf * xf, axis=-1, keepdims=True)
        hb = ((xf * lax.rsqrt(ms + EPS)) * wn_ref[...]).astype(BF16)
        h_scr[...] = hb

        gr = lax.dot_general(wgt_ref[...], hb, (((1,), (1,)), ((), ())), preferred_element_type=F32) + gbt_ref[...]
        row = lax.broadcasted_iota(jnp.int32, gr.shape, 0)
        is_in_gate = (row >= ATTN_HEADS) & (row < ATTN_HEADS + MLSTM_HEADS)
        gr = jnp.where(is_in_gate, gr, _log_sigmoid(gr))
        cs_chunk = _lane_prefix_sum(gr, chunk)
        cs_full = _lane_prefix_sum(gr, tm)
        carry = jnp.where(seq_start, 0.0, carry_scr[...])
        cs_full = cs_full + carry[:, :1]
        carry_scr[...] = jnp.broadcast_to(cs_full[:, tm - 1:tm], carry_scr.shape)
        gates = jnp.where(row < ATTN_HEADS, cs_full * LOG2E, jnp.where(is_in_gate, gr, cs_chunk))
        grow_ref[...] = gates
        padded = jnp.concatenate([gates, jnp.zeros((GATE_LANES - gates.shape[0], tm), F32)], axis=0)
        gcol_ref[...] = padded.T

        qw = qn_ref[...] * (LOG2E * ATTN_HEAD_DIM ** -0.5)
        for t in range(_T_AQ, _T_AK):
            store(t, head_norm(tile_acc(t), qw))
        for t in range(_T_AK, _T_AV):
            store(t, head_norm(tile_acc(t), kn_ref[...]))

    @pl.when(j == 1)
    def _():
        store(_T_MQ, conv_silu(tile_acc(_T_MQ), 0, MLSTM_QK_DIM ** -0.5))
        store(_T_MK, conv_silu(tile_acc(_T_MK), 1, 1.0))
        for t in range(_T_AV, _T_MQ):
            store(t, tile_acc(t))

    @pl.when(j == 2)
    def _():
        for t in range(_T_MV, _T_END):
            store(t, tile_acc(t))


def _in_proj(x2d, wn, w_attn, w_mlstm, w_ogate, wgt, gbt, qn, kn, cw, cb, *, batch, seq):
    T, D = x2d.shape
    tm, tn = ROW_TILE, PROJ_COL_TILE
    n_cols = w_attn.shape[0] + w_mlstm.shape[0] + w_ogate.shape[0]
    assert w_attn.shape[0] == (_T_MQ - _T_AQ) * tn and w_mlstm.shape[0] == (_T_MO - _T_MQ) * tn
    n_groups = 3
    group = n_cols // n_groups
    assert group == (_T_AV - _T_AQ) * tn == (_T_MV - _T_AV) * tn == (_T_END - _T_MV) * tn
    tiles_per_seq = seq // tm
    const = lambda i, j: (0, 0)
    const3 = lambda i, j: (0, 0, 0)
    kern = functools.partial(_in_proj_kernel, tiles_per_seq=tiles_per_seq, chunk=MLSTM_CHUNK)
    return pl.pallas_call(
        kern,
        grid=(T // tm, n_groups),
        in_specs=[
            pl.BlockSpec((tm, D), lambda i, j: (i, 0)),
            pl.BlockSpec((1, D), const),
            pl.BlockSpec(w_attn.shape, const, pipeline_mode=pl.Buffered(1)),
            pl.BlockSpec(w_mlstm.shape, const, pipeline_mode=pl.Buffered(1)),
            pl.BlockSpec(w_ogate.shape, const, pipeline_mode=pl.Buffered(1)),
            pl.BlockSpec((N_GATES, D), const),
            pl.BlockSpec((N_GATES, 1), const),
            pl.BlockSpec((1, ATTN_HEAD_DIM), const),
            pl.BlockSpec((1, ATTN_HEAD_DIM), const),
            pl.BlockSpec((2, CONV_WIDTH, tn), const3),
            pl.BlockSpec((2, 1, tn), const3),
        ],
        out_specs=[
            pl.BlockSpec((tm, group), lambda i, j: (i, j)),
            pl.BlockSpec((tm, GATE_LANES), lambda i, j: (i, 0)),
            pl.BlockSpec((None, N_GATES, tm), lambda i, j: (i // tiles_per_seq, 0, i % tiles_per_seq)),
        ],
        out_shape=[
            jax.ShapeDtypeStruct((T, n_cols), BF16),
            jax.ShapeDtypeStruct((T, GATE_LANES), F32),
            jax.ShapeDtypeStruct((batch, N_GATES, seq), F32),
        ],
        scratch_shapes=[
            pltpu.VMEM((tm, D), BF16),
            pltpu.VMEM((N_GATES, GATE_LANES), F32),
            pltpu.VMEM((2, CONV_HALO + tm, tn), F32),
        ],
        compiler_params=pltpu.CompilerParams(
            dimension_semantics=("arbitrary", "arbitrary"), vmem_limit_bytes=VMEM_LIMIT_BYTES),
        name="in_proj",
    )(x2d, wn, w_attn, w_mlstm, w_ogate, wgt, gbt, qn, kn, cw, cb)


def _fox_kernel(q_ref, k_ref, v_ref, c_ref, ccol_ref, o_ref, kmax_scr, *, tk):
    hg = pl.program_id(1)
    qi = pl.program_id(2)
    tq = q_ref.shape[0]
    d = ATTN_HEAD_DIM
    n_heads = q_ref.shape[1] // d
    n_diag = tq // tk
    qs = [q_ref[:, g * d:(g + 1) * d] for g in range(n_heads)]

    @pl.when(qi == 0)
    def _():
        for g in range(n_heads):
            kf = k_ref[:, g * d:(g + 1) * d].astype(F32)
            k2 = jnp.max(jnp.sum(kf * kf, axis=-1, keepdims=True), axis=0, keepdims=True)
            kmax_scr[g] = jnp.broadcast_to(k2, kmax_scr.shape[1:])

    lane = lax.broadcasted_iota(jnp.int32, ccol_ref.shape, 1)
    ccol = ccol_ref[...]
    mis, bounds = [], []
    for g in range(n_heads):
        qf = qs[g].astype(F32)
        bound = jnp.sqrt(jnp.sum(qf * qf, axis=-1, keepdims=True) * kmax_scr[g][:1, :1])
        ci = jnp.sum(jnp.where(lane == hg * n_heads + g, ccol, 0.0), axis=-1, keepdims=True)
        mis.append(bound - ci)
        bounds.append(jnp.max(bound))
    worst = functools.reduce(jnp.maximum, bounds)

    def load_block(j, g):
        start = j * tk if isinstance(j, int) else pl.multiple_of(j * tk, tk)
        k = k_ref[pl.ds(start, tk), g * d:(g + 1) * d]
        v = v_ref[pl.ds(start, tk), g * d:(g + 1) * d]
        cj = c_ref[pl.ds(hg * n_heads + g, 1), pl.ds(start, tk)]
        return k, v, cj

    def causal_mask(r0):
        r = lax.broadcasted_iota(jnp.int32, (tq - r0, tk), 0)
        c = lax.broadcasted_iota(jnp.int32, (tq - r0, tk), 1)
        return c <= r

    def rejoin(old, new, r0):
        return jnp.concatenate([old[:r0], new], axis=0) if r0 else new

    def finish(carry):
        for g in range(n_heads):
            l, acc = carry[g][-2:]
            l = jnp.sum(l, axis=-1, keepdims=True)
            o_ref[:, g * d:(g + 1) * d] = (acc / l).astype(o_ref.dtype)

    def scores(j, g, r0):
        k = load_block(j, g)[0]
        return lax.dot_general(qs[g][r0:], k, (((1,), (1,)), ((), ())), preferred_element_type=F32)

    def consume(j, g, s, l, acc, diag):
        r0 = 0 if diag is None else diag * tk
        _, v, cj = load_block(j, g)
        e = (s - cj) - mis[g][r0:]
        if diag is not None:
            e = jnp.where(causal_mask(r0), e, -jnp.inf)
        p = jnp.exp2(e)
        l_new = l[r0:] + functools.reduce(jnp.add, [p[:, c0:c0 + d] for c0 in range(0, tk, d)])
        acc_new = acc[r0:] + jnp.dot(p.astype(BF16), v, preferred_element_type=F32)
        return rejoin(l, l_new, r0), rejoin(acc, acc_new, r0)

    def bounded_step(j, carry, diag):
        r0 = 0 if diag is None else diag * tk
        return tuple(consume(j, g, scores(j, g, r0), *carry[g], diag) for g in range(n_heads))


    def online_step(j, carry, diag):
        r0 = 0 if diag is None else diag * tk
        out = []
        for g in range(n_heads):
            m, l, acc = carry[g]
            k, v, cj = load_block(j, g)
            s = lax.dot_general(qs[g][r0:], k, (((1,), (1,)), ((), ())), preferred_element_type=F32) - cj
            if diag is not None:
                s = jnp.where(causal_mask(r0), s, -jnp.inf)
            m_new = jnp.maximum(m[r0:], jnp.max(s, axis=-1, keepdims=True))
            alpha = jnp.exp2(m[r0:] - m_new)
            p = jnp.exp2(s - m_new)
            l_new = alpha * l[r0:] + jnp.sum(p, axis=-1, keepdims=True)
            acc_new = alpha * acc[r0:] + jnp.dot(p.astype(BF16), v, preferred_element_type=F32)
            out.append((rejoin(m, m_new, r0), rejoin(l, l_new, r0), rejoin(acc, acc_new, r0)))
        return tuple(out)

    def run(step, init):
        carry = lax.fori_loop(0, qi * n_diag, lambda j, c: step(j, c, None), init)
        for jj in range(n_diag):
            carry = step(qi * n_diag + jj, carry, jj)
        finish(carry)

    zeros = (jnp.zeros((tq, 1), F32), jnp.zeros((tq, d), F32))

    def run_unrolled(step, init):
        for c in range(k_ref.shape[0] // tq):
            @pl.when(qi == c)
            def _():
                carry = init
                for j in range(c * n_diag):
                    carry = step(j, carry, None)
                for jj in range(n_diag):
                    carry = step(c * n_diag + jj, carry, jj)
                finish(carry)

    @pl.when(worst <= FOX_MAX_BOUND)
    def _():
        run_unrolled(bounded_step,
                     tuple((jnp.zeros((tq, d), F32), jnp.zeros((tq, d), F32)) for _ in range(n_heads)))

    @pl.when(jnp.logical_not(worst <= FOX_MAX_BOUND))
    def _():
        run(online_step, tuple((jnp.full((tq, 1), -jnp.inf, F32),) + zeros for _ in range(n_heads)))


def _fox_attention(proj3, grow, gcol3, *, batch, seq):
    tq, tk = ATTN_Q_TILE, ATTN_KV_TILE
    assert tq % tk == 0
    dd =ATTN_HEAD_DIM * ATTN_HEADS_PER_STEP
    n_groups = ATTN_HEADS // ATTN_HEADS_PER_STEP
    grid = (batch, n_groups, seq // tq)
    return pl.pallas_call(
        functools.partial(_fox_kernel, tk=tk),
        grid=grid,
        in_specs=[
            pl.BlockSpec((None, tq, dd), lambda b, h, qi: (b, qi, h)),
            pl.BlockSpec((None, seq, dd), lambda b, h, qi: (b, 0, n_groups + h)),
            pl.BlockSpec((None, seq, dd), lambda b, h, qi: (b, 0, 2 * n_groups + h)),
            pl.BlockSpec((None, N_GATES, seq), lambda b, h, qi: (b, 0, 0)),
            pl.BlockSpec((None, tq, GATE_LANES), lambda b, h, qi: (b, qi, 0)),
        ],
        out_specs=pl.BlockSpec((None, tq, dd), lambda b, h, qi: (b, qi, h)),
        out_shape=jax.ShapeDtypeStruct((batch, seq, ATTN_WIDTH), BF16),
        scratch_shapes=[pltpu.VMEM((ATTN_HEADS_PER_STEP, 8, 128), F32)],
        compiler_params=pltpu.CompilerParams(
            dimension_semantics=("arbitrary", "arbitrary", "arbitrary"), vmem_limit_bytes=VMEM_LIMIT_BYTES),
        name="fox_attention",
    )(proj3, proj3, proj3, grow, gcol3)


def _mlstm_kernel(q_ref, k_ref, v_ref, og_ref, gcol_ref, grow_ref, nw_ref, o_ref, c_scr, n_scr, m_scr):
    ci = pl.program_id(1)
    L = q_ref.shape[0]
    dk, dv = MLSTM_QK_DIM, MLSTM_V_DIM

    @pl.when(ci == 0)
    def _():
        c_scr[...] = jnp.zeros_like(c_scr)
        n_scr[...] = jnp.zeros_like(n_scr)
        m_scr[...] = jnp.zeros_like(m_scr)

    t_idx = lax.broadcasted_iota(jnp.int32, (L, L), 0)
    s_idx = lax.broadcasted_iota(jnp.int32, (L, L), 1)
    causal = s_idx <= t_idx
    gcol = gcol_ref[...]
    grow = grow_ref[...]

    heads = range(MLSTM_HEADS)
    nt = (((1,), (1,)), ((), ()))
    q = [q_ref[:, hh * dk:(hh + 1) * dk] for hh in heads]
    k = [k_ref[:, hh * dk:(hh + 1) * dk] for hh in heads]
    v = [v_ref[:, hh * dv:(hh + 1) * dv] for hh in heads]
    gi = [ATTN_HEADS + hh for hh in heads]
    gf = [ATTN_HEADS + MLSTM_HEADS + hh for hh in heads]
    b_row = [grow[gf[hh]:gf[hh] + 1, :] for hh in heads]
    b_col = [gcol[:, gf[hh]:gf[hh] + 1] for hh in heads]
    a_row = [grow[gi[hh]:gi[hh] + 1, :] - b_row[hh] for hh in heads]
    a_col = [gcol[:, gi[hh]:gi[hh] + 1] - b_col[hh] for hh in heads]
    m_prev = [m_scr[hh][:1, :1] for hh in heads]
    ct = [c_scr[hh] for hh in heads]
    n = [n_scr[hh][:1, :] for hh in heads]

    qk = [lax.dot_general(q[hh], k[hh], nt, preferred_element_type=F32) for hh in heads]
    qc = [jnp.dot(q[hh], ct[hh].astype(BF16), preferred_element_type=F32) for hh in heads]
    big_m = [jnp.maximum(jnp.max(jnp.where(causal, a_row[hh], -jnp.inf), axis=-1, keepdims=True), m_prev[hh])
             for hh in heads]
    w_intra = [jnp.exp(jnp.where(causal, a_row[hh] - big_m[hh], -jnp.inf)) for hh in heads]
    w_inter = [jnp.exp(m_prev[hh] - big_m[hh]) for hh in heads]
    s_qk = [qk[hh] * w_intra[hh] for hh in heads]
    num = [jnp.dot(s_qk[hh].astype(BF16), v[hh], preferred_element_type=F32) + w_inter[hh] * qc[hh]
           for hh in heads]
    qn = [jnp.sum(q[hh].astype(F32) * n[hh], axis=-1, keepdims=True) for hh in heads]
    den = [jnp.sum(s_qk[hh], axis=-1, keepdims=True) + w_inter[hh] * qn[hh] for hh in heads]
    dmax = [jnp.maximum(jnp.abs(den[hh]), jnp.exp(-(b_col[hh] + big_m[hh]))) for hh in heads]
    ms = [jnp.mean(num[hh] * num[hh], axis=-1, keepdims=True) for hh in heads]
    for hh in heads:
        hn = num[hh] * lax.rsqrt(ms[hh] + EPS * (dmax[hh] * dmax[hh]))
        og = og_ref[:, hh * dv:(hh + 1) * dv].astype(F32)
        out = (hn * nw_ref[:, hh * dv:(hh + 1) * dv]) * _sigmoid(og)
        o_ref[:, hh * dv:(hh + 1) * dv] = out.astype(o_ref.dtype)

    m_last = [jnp.maximum(jnp.max(a_row[hh], axis=-1, keepdims=True), m_prev[hh]) for hh in heads]
    kw = [k[hh].astype(F32) * jnp.exp(a_col[hh] - m_last[hh]) for hh in heads]
    upd = [lax.dot_general(kw[hh].astype(BF16), v[hh], (((0,), (0,)), ((), ())), preferred_element_type=F32)
           for hh in heads]
    for hh in heads:
        wc = jnp.exp(m_prev[hh] - m_last[hh])
        c_scr[hh] = wc * ct[hh] + upd[hh]
        n_new = wc * n[hh] + jnp.sum(kw[hh], axis=0, keepdims=True)
        n_scr[hh] = jnp.broadcast_to(n_new, n_scr.shape[1:])
        m_scr[hh] = jnp.broadcast_to(b_row[hh][:, L - 1:L] + m_last[hh], m_scr.shape[1:])


def _mlstm(proj3, gcol3, grow, nw, *, batch, seq):
    L = MLSTM_CHUNK
    grid = (batch, seq // L)
    qk_w, v_w = MLSTM_QK_WIDTH, MLSTM_WIDTH
    base = 3 * ATTN_WIDTH
    return pl.pallas_call(
        _mlstm_kernel,
        grid=grid,
        in_specs=[
            pl.BlockSpec((None, L, qk_w), lambda b, c: (b, c, base // qk_w)),
            pl.BlockSpec((None, L, qk_w), lambda b, c: (b, c, base // qk_w + 1)),
            pl.BlockSpec((None, L, v_w), lambda b, c: (b, c, (base + 2 * qk_w) // v_w)),
            pl.BlockSpec((None, L, v_w), lambda b, c: (b, c, (base + 2 * qk_w) // v_w + 1)),
            pl.BlockSpec((None, L, GATE_LANES), lambda b, c: (b, c, 0)),
            pl.BlockSpec((None, N_GATES, L), lambda b, c: (b, 0, c)),
            pl.BlockSpec((1, v_w), lambda b, c: (0, 0)),
        ],
        out_specs=pl.BlockSpec((None, L, v_w), lambda b, c: (b, c, 0)),
        out_shape=jax.ShapeDtypeStruct((batch, seq, v_w), BF16),
        scratch_shapes=[
            pltpu.VMEM((MLSTM_HEADS, MLSTM_QK_DIM, MLSTM_V_DIM), F32),
            pltpu.VMEM((MLSTM_HEADS, 8, MLSTM_QK_DIM), F32),
            pltpu.VMEM((MLSTM_HEADS, 8, 128), F32),
        ],
        compiler_params=pltpu.CompilerParams(
            dimension_semantics=("arbitrary", "arbitrary"), vmem_limit_bytes=VMEM_LIMIT_BYTES),
        name="mlstm",
    )(proj3, proj3, proj3, proj3, gcol3, grow, nw)


def _out_proj_kernel(x_ref, a_ref, m_ref, wa_ref, wm_ref, o_ref):
    y = jnp.dot(a_ref[...], wa_ref[...], preferred_element_type=F32)
    y = y + jnp.dot(m_ref[...], wm_ref[...], preferred_element_type=F32)
    o_ref[...] = x_ref[...] + y


def _out_proj(x2d, attn2d, mlstm2d, w_attn, w_mlstm):
    T, D = x2d.shape
    tm = ROW_TILE
    const = lambda i: (0, 0)
    return pl.pallas_call(
        _out_proj_kernel,
        grid=(T // tm,),
        in_specs=[
            pl.BlockSpec((tm, D), lambda i: (i, 0)),
            pl.BlockSpec((tm, attn2d.shape[1]), lambda i: (i, 0)),
            pl.BlockSpec((tm, mlstm2d.shape[1]), lambda i: (i, 0)),
            pl.BlockSpec(w_attn.shape, const),
            pl.BlockSpec(w_mlstm.shape, const),
        ],
        out_specs=pl.BlockSpec((tm, D), lambda i: (i, 0)),
        out_shape=jax.ShapeDtypeStruct((T, D), F32),
        compiler_params=pltpu.CompilerParams(
            dimension_semantics=("arbitrary",), vmem_limit_bytes=VMEM_LIMIT_BYTES),
        name="out_proj",
    )(x2d, attn2d, mlstm2d, w_attn, w_mlstm)


def _ffn_kernel(x_ref, wn_ref, wg_ref, wu_ref, wd_ref, o_ref, h_scr):
    f = pl.program_id(1)

    @pl.when(f == 0)
    def _():
        xf = x_ref[...]
        ms = jnp.mean(xf * xf, axis=-1, keepdims=True)
        h_scr[...] = ((xf * lax.rsqrt(ms + EPS)) * wn_ref[...]).astype(BF16)
        o_ref[...] = xf

    hb = h_scr[...]
    g = jnp.dot(hb, wg_ref[...], preferred_element_type=F32)
    u = jnp.dot(hb, wu_ref[...], preferred_element_type=F32)
    a = (g * _sigmoid(g)) * u
    o_ref[...] += jnp.dot(a.astype(BF16), wd_ref[...], preferred_element_type=F32)


def _ffn(x2d, wn, wg, wu, wd):
    T, D = x2d.shape
    F = wg.shape[1]
    tm, tf = ROW_TILE, FFN_COL_TILE
    return pl.pallas_call(
        _ffn_kernel,
        grid=(T // tm, F // tf),
        in_specs=[
            pl.BlockSpec((tm, D), lambda i, f: (i, 0)),
            pl.BlockSpec((1, D), lambda i, f: (0, 0)),
            pl.BlockSpec((D, tf), lambda i, f: (0, f)),
            pl.BlockSpec((D, tf), lambda i, f: (0, f)),
            pl.BlockSpec((tf, D), lambda i, f: (f, 0)),
        ],
        out_specs=pl.BlockSpec((tm, D), lambda i, f: (i, 0)),
        out_shape=jax.ShapeDtypeStruct((T, D), F32),
        scratch_shapes=[pltpu.VMEM((tm, D), BF16)],
        compiler_params=pltpu.CompilerParams(
            dimension_semantics=("arbitrary", "arbitrary"), vmem_limit_bytes=VMEM_LIMIT_BYTES),
        name="ffn",
    )(x2d, wn, wg, wu, wd)


def _ple_kernel(x_ref, p_ref, wn_ref, wgate_ref, wproj_ref, wpost_ref, o_ref):
    xf = x_ref[...]
    ms = jnp.mean(xf * xf, axis=-1, keepdims=True)
    hb = ((xf * lax.rsqrt(ms + EPS)) * wn_ref[...]).astype(BF16)
    e = jnp.dot(p_ref[...].astype(BF16), wproj_ref[...], preferred_element_type=F32)
    ems = jnp.mean(e * e, axis=-1, keepdims=True)
    e = (e * lax.rsqrt(ems + EPS)) * wpost_ref[...]
    tn = PROJ_COL_TILE
    for c0 in range(0, o_ref.shape[1], tn):
        gate = _sigmoid(jnp.dot(hb, wgate_ref[:, c0:c0 + tn], preferred_element_type=F32))
        o_ref[:, c0:c0 + tn] = xf[:, c0:c0 + tn] + gate * e[:, c0:c0 + tn]


def _ple(x2d, p2d, wn, wgate, wproj, wpost):
    T, D = x2d.shape
    P = p2d.shape[1]
    tm = ROW_TILE
    const = lambda i: (0, 0)
    return pl.pallas_call(
        _ple_kernel,
        grid=(T // tm,),
        in_specs=[
            pl.BlockSpec((tm, D), lambda i: (i, 0)),
            pl.BlockSpec((tm, P), lambda i: (i, 0)),
            pl.BlockSpec((1, D), const),
            pl.BlockSpec((D, D), const),
            pl.BlockSpec((P, D), const),
            pl.BlockSpec((1, D), const),
        ],
        out_specs=pl.BlockSpec((tm, D), lambda i: (i, 0)),
        out_shape=jax.ShapeDtypeStruct((T, D), F32),
        compiler_params=pltpu.CompilerParams(
            dimension_semantics=("arbitrary",), vmem_limit_bytes=VMEM_LIMIT_BYTES),
        name="ple",
    )(x2d, p2d, wn, wgate, wproj, wpost)


def _layer(x2d, p2d, batch, seq, w_norm_mix, w_in, fox_f_bias, q_norm_w, k_norm_w, mlstm_conv_w, mlstm_conv_b,
           mlstm_i_bias, mlstm_f_bias, mlstm_out_norm_w, w_out, w_norm_ffn, w_ffn_gate, w_ffn_up, w_ffn_down,
           w_norm_ple, w_ple_gate, w_ple_proj, w_ple_post_norm):
    D = x2d.shape[1]
    A, QK, MV = ATTN_WIDTH, MLSTM_QK_WIDTH, MLSTM_WIDTH
    o_af = 3 * A
    o_m = o_af + ATTN_HEADS
    o_mi = o_m + 2 * QK + MV
    o_mo = o_mi + 2 * MLSTM_HEADS
    w_t = w_in.T.astype(BF16)
    w_attn, w_mlstm, w_ogate = w_t[:o_af], w_t[o_m:o_mi], w_t[o_mo:]
    wgt = jnp.concatenate([w_t[o_af:o_m], w_t[o_mi:o_mo]], axis=0)
    gbt = jnp.concatenate([fox_f_bias, mlstm_i_bias, mlstm_f_bias]).astype(F32)[:, None]
    cw = mlstm_conv_w.reshape(CONV_WIDTH, 2, QK).transpose(1, 0, 2)
    cb = mlstm_conv_b.reshape(2, 1, QK)

    proj, gcol, grow = _in_proj(x2d, w_norm_mix[None, :], w_attn, w_mlstm, w_ogate, wgt, gbt, q_norm_w[None, :],
                                k_norm_w[None, :], cw, cb, batch=batch, seq=seq)
    proj3 = proj.reshape(batch, seq, proj.shape[1])
    gcol3 = gcol.reshape(batch, seq, GATE_LANES)
    attn = _fox_attention(proj3, grow, gcol3, batch=batch, seq=seq)
    mlstm = _mlstm(proj3, gcol3, grow, mlstm_out_norm_w[None, :], batch=batch, seq=seq)

    x1 = _out_proj(x2d, attn.reshape(batch * seq, A), mlstm.reshape(batch * seq, MV),
                   w_out[:A].astype(BF16), w_out[A:].astype(BF16))
    x2 = _ffn(x1, w_norm_ffn[None, :], w_ffn_gate.astype(BF16), w_ffn_up.astype(BF16), w_ffn_down.astype(BF16))
    x3 = _ple(x2, p2d, w_norm_ple[None, :], w_ple_gate.astype(BF16), w_ple_proj.astype(BF16),
              w_ple_post_norm[None, :])
    return x3


def kernel(x, p, w_norm_mix, w_in, fox_f_bias, q_norm_w, k_norm_w, mlstm_conv_w, mlstm_conv_b, mlstm_i_bias,
           mlstm_f_bias, mlstm_out_norm_w, w_out, w_norm_ffn, w_ffn_gate, w_ffn_up, w_ffn_down, w_norm_ple,
           w_ple_gate, w_ple_proj, w_ple_post_norm):
    B, S, D = x.shape
    depth = w_in.shape[0]
    x2d = x.reshape(B * S, D)
    for i in range(depth):
        x2d = _layer(x2d, p[i].reshape(B * S, p.shape[-1]), B, S, w_norm_mix[i], w_in[i], fox_f_bias[i],
                     q_norm_w[i], k_norm_w[i], mlstm_conv_w[i], mlstm_conv_b[i], mlstm_i_bias[i], mlstm_f_bias[i],
                     mlstm_out_norm_w[i], w_out[i], w_norm_ffn[i], w_ffn_gate[i], w_ffn_up[i], w_ffn_down[i],
                     w_norm_ple[i], w_ple_gate[i], w_ple_proj[i], w_ple_post_norm[i])
    return x2d.reshape(B, S, D)
```

```python
import functools
import math

import jax
import jax.numpy as jnp
from jax import lax
from jax.experimental import pallas as pl
from jax.experimental.pallas import tpu as pltpu

F32 = jnp.float32
BF16 = jnp.bfloat16
EPS = 1e-6

ATTN_HEADS = 8
ATTN_HEAD_DIM = 128
MLSTM_HEADS = 4
MLSTM_QK_DIM = 128
MLSTM_V_DIM = 256
CONV_WIDTH = 4
ATTN_WIDTH = ATTN_HEADS * ATTN_HEAD_DIM
MLSTM_QK_WIDTH = MLSTM_HEADS * MLSTM_QK_DIM
MLSTM_WIDTH = MLSTM_HEADS * MLSTM_V_DIM
N_GATES = ATTN_HEADS + 2 * MLSTM_HEADS
GATE_LANES = 128
CONV_HALO = 8

VMEM_LIMIT_BYTES = 56 * 1024 * 1024

ROW_TILE = 512
PROJ_COL_TILE = 512
MLSTM_CHUNK = 256
ATTN_Q_TILE = 1024
ATTN_KV_TILE = 512
ATTN_HEADS_PER_STEP = 2
LOG2E = math.log2(math.e)
FOX_MAX_BOUND = 40.0
FFN_COL_TILE = 512

_T_AQ, _T_AK, _T_AV, _T_MQ, _T_MK, _T_MV, _T_MO, _T_END = 0, 2, 4, 6, 7, 8, 10, 12


def _log_sigmoid(z):
    return jnp.minimum(z, 0.0) - jnp.log1p(jnp.exp(-jnp.abs(z)))


def _sigmoid(z):
    return 1.0 / (1.0 + jnp.exp(-z))


def _split3(v):
    hi = v.astype(BF16)
    r1 = v - hi.astype(F32)
    mid = r1.astype(BF16)
    lo = (r1 - mid.astype(F32)).astype(BF16)
    return hi, mid, lo


def _lane_prefix_sum(v, period):
    axis = v.ndim - 1
    pos = lax.broadcasted_iota(jnp.int32, v.shape, axis) & (period - 1)
    shift = 1
    while shift < period:
        v = v + jnp.where(pos >= shift, pltpu.roll(v, shift, axis=axis), 0.0)
        shift *= 2
    return v


def _in_proj_kernel(x_ref, wn_ref, wa_ref, wm_ref, wo_ref, wgt_ref, gbt_ref, qn_ref, kn_ref, cw_ref, cb_ref,
                    proj_ref, gcol_ref, grow_ref, h_scr, carry_scr, conv_scr, *, n_tiles, tiles_per_seq, chunk):
    i = pl.program_id(0)
    tm = x_ref.shape[0]
    tn = PROJ_COL_TILE
    seq_start = ((i - 1) % tiles_per_seq) == 0
    slot_in, slot_out = [0], [0]

    def tile_acc(t):
        w_ref, t0 = (wa_ref, _T_AQ) if t < _T_MQ else (wm_ref, _T_MQ) if t < _T_MO else (wo_ref, _T_MO)
        return lax.dot_general(h_scr[slot_in[0]], w_ref[(t - t0) * tn:(t - t0 + 1) * tn, :],
                               (((1,), (1,)), ((), ())), preferred_element_type=F32)

    def store(t, val):
        proj_ref[:, t * tn:(t + 1) * tn] = val.astype(BF16)

    def head_norm(acc, w):
        outs = []
        for hh in range(tn // ATTN_HEAD_DIM):
            a = acc[:, hh * ATTN_HEAD_DIM:(hh + 1) * ATTN_HEAD_DIM]
            ms = jnp.mean(a * a, axis=-1, keepdims=True)
            outs.append((a * lax.rsqrt(ms + EPS)) * w)
        return jnp.concatenate(outs, axis=-1)

    def conv_silu(acc, which, scale):
        buf = conv_scr.at[which]
        buf[0:CONV_HALO, :] = jnp.where(seq_start, 0.0, buf[tm:tm + CONV_HALO, :])
        buf[CONV_HALO:CONV_HALO + tm, :] = acc
        cw = cw_ref[which]
        y = cb_ref[which] + cw[CONV_WIDTH - 1:CONV_WIDTH, :] * acc
        for tap in range(CONV_WIDTH - 1):
            off = CONV_HALO - (CONV_WIDTH - 1) + tap
            y = y + cw[tap:tap + 1, :] * buf[off:off + tm, :]
        y = y * _sigmoid(y)
        return y * scale if scale != 1.0 else y

    def norm_and_gates():
        xf = x_ref[...]
        ms = jnp.mean(xf * xf, axis=-1, keepdims=True)
        hb = ((xf * lax.rsqrt(ms + EPS)) * wn_ref[...]).astype(BF16)

        gr = lax.dot_general(wgt_ref[...], hb, (((1,), (1,)), ((), ())), preferred_element_type=F32) + gbt_ref[...]
        row = lax.broadcasted_iota(jnp.int32, gr.shape, 0)
        is_in_gate = (row >= ATTN_HEADS) & (row < ATTN_HEADS + MLSTM_HEADS)
        gr = jnp.where(is_in_gate, gr, _log_sigmoid(gr))
        cs_chunk = _lane_prefix_sum(gr, chunk)
        cs_full = _lane_prefix_sum(gr, tm)
        carry = jnp.where((i % tiles_per_seq) == 0, 0.0, carry_scr[...])
        cs_full = cs_full + carry[:, :1]
        gates = jnp.where(row < ATTN_HEADS, cs_full * LOG2E, jnp.where(is_in_gate, gr, cs_chunk))
        padded = jnp.concatenate([gates, jnp.zeros((GATE_LANES - gates.shape[0], tm), F32)], axis=0)
        h_scr[slot_out[0]] = hb
        carry_scr[...] = jnp.broadcast_to(cs_full[:, tm - 1:tm], carry_scr.shape)
        grow_ref[...] = gates
        gcol_ref[...] = padded.T

    def project(between):
        qw = qn_ref[...] * (LOG2E * ATTN_HEAD_DIM ** -0.5)
        heavy = [lambda: store(_T_MQ, conv_silu(tile_acc(_T_MQ), 0, MLSTM_QK_DIM ** -0.5)),
                 lambda: store(_T_MK, conv_silu(tile_acc(_T_MK), 1, 1.0))]
        heavy += [functools.partial(lambda t: store(t, head_norm(tile_acc(t), qw)), t) for t in range(_T_AQ, _T_AK)]
        heavy += [functools.partial(lambda t: store(t, head_norm(tile_acc(t), kn_ref[...])), t)
                  for t in range(_T_AK, _T_AV)]
        plain = [functools.partial(lambda t: store(t, tile_acc(t)), t)
                 for t in list(range(_T_AV, _T_MQ)) + list(range(_T_MV, _T_END))]
        for n, (hv, pn) in enumerate(zip(heavy, plain)):
            hv()
            pn()
            if n == 1:
                between()

    @pl.when(i == 0)
    def _():
        carry_scr[...] = jnp.zeros_like(carry_scr)
        conv_scr[...] = jnp.zeros_like(conv_scr)
        slot_out[0] = 0
        norm_and_gates()

    for parity in range(2):
        @pl.when((i > 0) & (i % 2 == parity))
        def _():
            slot_in[0], slot_out[0] = 1 - parity, parity
            project(norm_and_gates)


def _in_proj(x2d, wn, w_attn, w_mlstm, w_ogate, wgt, gbt, qn, kn, cw, cb, *, batch, seq):
    T, D = x2d.shape
    tm, tn = ROW_TILE, PROJ_COL_TILE
    n_cols = w_attn.shape[0] + w_mlstm.shape[0] + w_ogate.shape[0]
    assert w_attn.shape[0] == (_T_MQ - _T_AQ) * tn and w_mlstm.shape[0] == (_T_MO - _T_MQ) * tn
    assert n_cols == _T_END * tn
    n_tiles = T // tm
    tiles_per_seq = seq // tm
    const = lambda i: (0, 0)
    const3 = lambda i: (0, 0, 0)
    tile = lambda i: jnp.minimum(i, n_tiles - 1)
    kern = functools.partial(_in_proj_kernel, n_tiles=n_tiles, tiles_per_seq=tiles_per_seq, chunk=MLSTM_CHUNK)
    return pl.pallas_call(
        kern,
        grid=(n_tiles + 1,),
        in_specs=[
            pl.BlockSpec((tm, D), lambda i: (tile(i), 0)),
            pl.BlockSpec((1, D), const),
            pl.BlockSpec(w_attn.shape, const, pipeline_mode=pl.Buffered(1)),
            pl.BlockSpec(w_mlstm.shape, const, pipeline_mode=pl.Buffered(1)),
            pl.BlockSpec(w_ogate.shape, const, pipeline_mode=pl.Buffered(1)),
            pl.BlockSpec((N_GATES, D), const),
            pl.BlockSpec((N_GATES, 1), const),
            pl.BlockSpec((1, ATTN_HEAD_DIM), const),
            pl.BlockSpec((1, ATTN_HEAD_DIM), const),
            pl.BlockSpec((2, CONV_WIDTH, tn), const3),
            pl.BlockSpec((2, 1, tn), const3),
        ],
        out_specs=[
            pl.BlockSpec((tm, n_cols), lambda i: (jnp.maximum(i - 1, 0), 0)),
            pl.BlockSpec((tm, GATE_LANES), lambda i: (i, 0)),
            pl.BlockSpec((None, N_GATES, tm), lambda i: (i // tiles_per_seq, 0, i % tiles_per_seq)),
        ],
        out_shape=[
            jax.ShapeDtypeStruct((T, n_cols), BF16),
            jax.ShapeDtypeStruct((T + tm, GATE_LANES), F32),
            jax.ShapeDtypeStruct((batch + 1, N_GATES, seq), F32),
        ],
        scratch_shapes=[
            pltpu.VMEM((2, tm, D), BF16),
            pltpu.VMEM((N_GATES, GATE_LANES), F32),
            pltpu.VMEM((2, CONV_HALO + tm, tn), F32),
        ],
        compiler_params=pltpu.CompilerParams(
            dimension_semantics=("arbitrary",), vmem_limit_bytes=VMEM_LIMIT_BYTES),
        name="in_proj",
    )(x2d, wn, w_attn, w_mlstm, w_ogate, wgt, gbt, qn, kn, cw, cb)


def _fox_kernel(q_ref, k_ref, v_ref, c_ref, ccol_ref, o_ref, kmax_scr, *, tk):
    hg = pl.program_id(1)
    qi = pl.program_id(2)
    tq = q_ref.shape[0]
    d = ATTN_HEAD_DIM
    n_heads = q_ref.shape[1] // d
    n_diag = tq // tk
    qs = [q_ref[:, g * d:(g + 1) * d] for g in range(n_heads)]

    @pl.when(qi == 0)
    def _():
        for g in range(n_heads):
            kf = k_ref[:, g * d:(g + 1) * d].astype(F32)
            k2 = jnp.max(jnp.sum(kf * kf, axis=-1, keepdims=True), axis=0, keepdims=True)
            kmax_scr[g] = jnp.broadcast_to(k2, kmax_scr.shape[1:])

    lane = lax.broadcasted_iota(jnp.int32, ccol_ref.shape, 1)
    ccol = ccol_ref[...]
    mis, bounds = [], []
    for g in range(n_heads):
        qf = qs[g].astype(F32)
        bound = jnp.sqrt(jnp.sum(qf * qf, axis=-1, keepdims=True) * kmax_scr[g][:1, :1])
        ci = jnp.sum(jnp.where(lane == hg * n_heads + g, ccol, 0.0), axis=-1, keepdims=True)
        mis.append(bound - ci)
        bounds.append(jnp.max(bound))
    worst = functools.reduce(jnp.maximum, bounds)

    def load_block(j, g):
        start = j * tk if isinstance(j, int) else pl.multiple_of(j * tk, tk)
        k = k_ref[pl.ds(start, tk), g * d:(g + 1) * d]
        v = v_ref[pl.ds(start, tk), g * d:(g + 1) * d]
        cj = c_ref[pl.ds(hg * n_heads + g, 1), pl.ds(start, tk)]
        return k, v, cj

    def causal_mask(r0):
        r = lax.broadcasted_iota(jnp.int32, (tq - r0, tk), 0)
        c = lax.broadcasted_iota(jnp.int32, (tq - r0, tk), 1)
        return c <= r

    def rejoin(old, new, r0):
        return jnp.concatenate([old[:r0], new], axis=0) if r0 else new

    def finish(carry):
        for g in range(n_heads):
            l, acc = carry[g][-2:]
            l = jnp.sum(l, axis=-1, keepdims=True)
            o_ref[:, g * d:(g + 1) * d] = (acc / l).astype(o_ref.dtype)

    def scores(j, g, r0):
        k = load_block(j, g)[0]
        return lax.dot_general(qs[g][r0:], k, (((1,), (1,)), ((), ())), preferred_element_type=F32)

    def consume(j, g, s, l, acc, diag):
        r0 = 0 if diag is None else diag * tk
        _, v, cj = load_block(j, g)
        e = (s - cj) - mis[g][r0:]
        if diag is not None:
            e = jnp.where(causal_mask(r0), e, -jnp.inf)
        p = jnp.exp2(e)
        l_new = l[r0:] + functools.reduce(jnp.add, [p[:, c0:c0 + d] for c0 in range(0, tk, d)])
        acc_new = acc[r0:] + jnp.dot(p.astype(BF16), v, preferred_element_type=F32)
        return rejoin(l, l_new, r0), rejoin(acc, acc_new, r0)

    def bounded_step(j, carry, diag):
        r0 = 0 if diag is None else diag * tk
        return tuple(consume(j, g, scores(j, g, r0), *carry[g], diag) for g in range(n_heads))


    def online_step(j, carry, diag):
        r0 = 0 if diag is None else diag * tk
        out = []
        for g in range(n_heads):
            m, l, acc = carry[g]
            k, v, cj = load_block(j, g)
            s = lax.dot_general(qs[g][r0:], k, (((1,), (1,)), ((), ())), preferred_element_type=F32) - cj
            if diag is not None:
                s = jnp.where(causal_mask(r0), s, -jnp.inf)
            m_new = jnp.maximum(m[r0:], jnp.max(s, axis=-1, keepdims=True))
            alpha = jnp.exp2(m[r0:] - m_new)
            p = jnp.exp2(s - m_new)
            l_new = alpha * l[r0:] + jnp.sum(p, axis=-1, keepdims=True)
            acc_new = alpha * acc[r0:] + jnp.dot(p.astype(BF16), v, preferred_element_type=F32)
            out.append((rejoin(m, m_new, r0), rejoin(l, l_new, r0), rejoin(acc, acc_new, r0)))
        return tuple(out)

    def run(step, init):
        carry = lax.fori_loop(0, qi * n_diag, lambda j, c: step(j, c, None), init)
        for jj in range(n_diag):
            carry = step(qi * n_diag + jj, carry, jj)
        finish(carry)

    zeros = (jnp.zeros((tq, 1), F32), jnp.zeros((tq, d), F32))

    def run_unrolled(step, init):
        for c in range(k_ref.shape[0] // tq):
            @pl.when(qi == c)
            def _():
                carry = init
                for j in range(c * n_diag):
                    carry = step(j, carry, None)
                for jj in range(n_diag):
                    carry = step(c * n_diag + jj, carry, jj)
                finish(carry)

    @pl.when(worst <= FOX_MAX_BOUND)
    def _():
        run_unrolled(bounded_step,
                     tuple((jnp.zeros((tq, d), F32), jnp.zeros((tq, d), F32)) for _ in range(n_heads)))

    @pl.when(jnp.logical_not(worst <= FOX_MAX_BOUND))
    def _():
        run(online_step, tuple((jnp.full((tq, 1), -jnp.inf, F32),) + zeros for _ in range(n_heads)))


def _fox_attention(proj3, grow, gcol3, *, batch, seq):
    tq, tk = ATTN_Q_TILE, ATTN_KV_TILE
    assert tq % tk == 0
    dd =ATTN_HEAD_DIM * ATTN_HEADS_PER_STEP
    n_groups = ATTN_HEADS // ATTN_HEADS_PER_STEP
    grid = (batch, n_groups, seq // tq)
    return pl.pallas_call(
        functools.partial(_fox_kernel, tk=tk),
        grid=grid,
        in_specs=[
            pl.BlockSpec((None, tq, dd), lambda b, h, qi: (b, qi, h)),
            pl.BlockSpec((None, seq, dd), lambda b, h, qi: (b, 0, n_groups + h)),
            pl.BlockSpec((None, seq, dd), lambda b, h, qi: (b, 0, 2 * n_groups + h)),
            pl.BlockSpec((None, N_GATES, seq), lambda b, h, qi: (b, 0, 0)),
            pl.BlockSpec((tq, GATE_LANES), lambda b, h, qi: (b * (seq // tq) + qi, 0)),
        ],
        out_specs=pl.BlockSpec((None, tq, dd), lambda b, h, qi: (b, qi, h)),
        out_shape=jax.ShapeDtypeStruct((batch, seq, ATTN_WIDTH), BF16),
        scratch_shapes=[pltpu.VMEM((ATTN_HEADS_PER_STEP, 8, 128), F32)],
        compiler_params=pltpu.CompilerParams(
            dimension_semantics=("arbitrary", "arbitrary", "arbitrary"), vmem_limit_bytes=VMEM_LIMIT_BYTES),
        name="fox_attention",
    )(proj3, proj3, proj3, grow, gcol3)


def _mlstm_kernel(q_ref, k_ref, v_ref, og_ref, gcol_ref, grow_ref, nw_ref, o_ref, c_scr, n_scr, m_scr):
    ci = pl.program_id(1)
    L = q_ref.shape[0]
    dk, dv = MLSTM_QK_DIM, MLSTM_V_DIM

    @pl.when(ci == 0)
    def _():
        c_scr[...] = jnp.zeros_like(c_scr)
        n_scr[...] = jnp.zeros_like(n_scr)
        m_scr[...] = jnp.zeros_like(m_scr)

    t_idx = lax.broadcasted_iota(jnp.int32, (L, L), 0)
    s_idx = lax.broadcasted_iota(jnp.int32, (L, L), 1)
    causal = s_idx <= t_idx
    gcol = gcol_ref[...]
    grow = grow_ref[...]

    heads = range(MLSTM_HEADS)
    nt = (((1,), (1,)), ((), ()))
    q = [q_ref[:, hh * dk:(hh + 1) * dk] for hh in heads]
    k = [k_ref[:, hh * dk:(hh + 1) * dk] for hh in heads]
    v = [v_ref[:, hh * dv:(hh + 1) * dv] for hh in heads]
    gi = [ATTN_HEADS + hh for hh in heads]
    gf = [ATTN_HEADS + MLSTM_HEADS + hh for hh in heads]
    b_row = [grow[gf[hh]:gf[hh] + 1, :] for hh in heads]
    b_col = [gcol[:, gf[hh]:gf[hh] + 1] for hh in heads]
    a_row = [grow[gi[hh]:gi[hh] + 1, :] - b_row[hh] for hh in heads]
    a_col = [gcol[:, gi[hh]:gi[hh] + 1] - b_col[hh] for hh in heads]
    m_prev = [m_scr[hh][:1, :1] for hh in heads]
    ct = [c_scr[hh] for hh in heads]
    n = [n_scr[hh][:1, :] for hh in heads]

    qk = [lax.dot_general(q[hh], k[hh], nt, preferred_element_type=F32) for hh in heads]
    qc = [jnp.dot(q[hh], ct[hh].astype(BF16), preferred_element_type=F32) for hh in heads]
    big_m = [jnp.maximum(jnp.max(jnp.where(causal, a_row[hh], -jnp.inf), axis=-1, keepdims=True), m_prev[hh])
             for hh in heads]
    w_intra = [jnp.exp(jnp.where(causal, a_row[hh] - big_m[hh], -jnp.inf)) for hh in heads]
    w_inter = [jnp.exp(m_prev[hh] - big_m[hh]) for hh in heads]
    s_qk = [qk[hh] * w_intra[hh] for hh in heads]
    num = [jnp.dot(s_qk[hh].astype(BF16), v[hh], preferred_element_type=F32) + w_inter[hh] * qc[hh]
           for hh in heads]
    qn = [jnp.sum(q[hh].astype(F32) * n[hh], axis=-1, keepdims=True) for hh in heads]
    den = [jnp.sum(s_qk[hh], axis=-1, keepdims=True) + w_inter[hh] * qn[hh] for hh in heads]
    dmax = [jnp.maximum(jnp.abs(den[hh]), jnp.exp(-(b_col[hh] + big_m[hh]))) for hh in heads]
    ms = [jnp.mean(num[hh] * num[hh], axis=-1, keepdims=True) for hh in heads]
    for hh in heads:
        hn = num[hh] * lax.rsqrt(ms[hh] + EPS * (dmax[hh] * dmax[hh]))
        og = og_ref[:, hh * dv:(hh + 1) * dv].astype(F32)
        out = (hn * nw_ref[:, hh * dv:(hh + 1) * dv]) * _sigmoid(og)
        o_ref[:, hh * dv:(hh + 1) * dv] = out.astype(o_ref.dtype)

    m_last = [jnp.maximum(jnp.max(a_row[hh], axis=-1, keepdims=True), m_prev[hh]) for hh in heads]
    kw = [k[hh].astype(F32) * jnp.exp(a_col[hh] - m_last[hh]) for hh in heads]
    upd = [lax.dot_general(kw[hh].astype(BF16), v[hh], (((0,), (0,)), ((), ())), preferred_element_type=F32)
           for hh in heads]
    for hh in heads:
        wc = jnp.exp(m_prev[hh] - m_last[hh])
        c_scr[hh] = wc * ct[hh] + upd[hh]
        n_new = wc * n[hh] + jnp.sum(kw[hh], axis=0, keepdims=True)
        n_scr[hh] = jnp.broadcast_to(n_new, n_scr.shape[1:])
        m_scr[hh] = jnp.broadcast_to(b_row[hh][:, L - 1:L] + m_last[hh], m_scr.shape[1:])


def _mlstm(proj3, gcol3, grow, nw, *, batch, seq):
    L = MLSTM_CHUNK
    grid = (batch, seq // L)
    qk_w, v_w = MLSTM_QK_WIDTH, MLSTM_WIDTH
    base = 3 * ATTN_WIDTH
    return pl.pallas_call(
        _mlstm_kernel,
        grid=grid,
        in_specs=[
            pl.BlockSpec((None, L, qk_w), lambda b, c: (b, c, base // qk_w)),
            pl.BlockSpec((None, L, qk_w), lambda b, c: (b, c, base // qk_w + 1)),
            pl.BlockSpec((None, L, v_w), lambda b, c: (b, c, (base + 2 * qk_w) // v_w)),
            pl.BlockSpec((None, L, v_w), lambda b, c: (b, c, (base + 2 * qk_w) // v_w + 1)),
            pl.BlockSpec((L, GATE_LANES), lambda b, c: (b * (seq // L) + c, 0)),
            pl.BlockSpec((None, N_GATES, L), lambda b, c: (b, 0, c)),
            pl.BlockSpec((1, v_w), lambda b, c: (0, 0)),
        ],
        out_specs=pl.BlockSpec((None, L, v_w), lambda b, c: (b, c, 0)),
        out_shape=jax.ShapeDtypeStruct((batch, seq, v_w), BF16),
        scratch_shapes=[
            pltpu.VMEM((MLSTM_HEADS, MLSTM_QK_DIM, MLSTM_V_DIM), F32),
            pltpu.VMEM((MLSTM_HEADS, 8, MLSTM_QK_DIM), F32),
            pltpu.VMEM((MLSTM_HEADS, 8, 128), F32),
        ],
        compiler_params=pltpu.CompilerParams(
            dimension_semantics=("arbitrary", "arbitrary"), vmem_limit_bytes=VMEM_LIMIT_BYTES),
        name="mlstm",
    )(proj3, proj3, proj3, proj3, gcol3, grow, nw)


def _out_proj_kernel(x_ref, a_ref, m_ref, wa_ref, wm_ref, o_ref):
    y = jnp.dot(a_ref[...], wa_ref[...], preferred_element_type=F32)
    y = y + jnp.dot(m_ref[...], wm_ref[...], preferred_element_type=F32)
    o_ref[...] = x_ref[...] + y


def _out_proj(x2d, attn2d, mlstm2d, w_attn, w_mlstm):
    T, D = x2d.shape
    tm = ROW_TILE
    const = lambda i: (0, 0)
    return pl.pallas_call(
        _out_proj_kernel,
        grid=(T // tm,),
        in_specs=[
            pl.BlockSpec((tm, D), lambda i: (i, 0)),
            pl.BlockSpec((tm, attn2d.shape[1]), lambda i: (i, 0)),
            pl.BlockSpec((tm, mlstm2d.shape[1]), lambda i: (i, 0)),
            pl.BlockSpec(w_attn.shape, const),
            pl.BlockSpec(w_mlstm.shape, const),
        ],
        out_specs=pl.BlockSpec((tm, D), lambda i: (i, 0)),
        out_shape=jax.ShapeDtypeStruct((T, D), F32),
        compiler_params=pltpu.CompilerParams(
            dimension_semantics=("arbitrary",), vmem_limit_bytes=VMEM_LIMIT_BYTES),
        name="out_proj",
    )(x2d, attn2d, mlstm2d, w_attn, w_mlstm)


def _ffn_kernel(x_ref, wn_ref, wg_ref, wu_ref, wd_ref, o_ref, h_scr):
    f = pl.program_id(1)

    @pl.when(f == 0)
    def _():
        xf = x_ref[...]
        ms = jnp.mean(xf * xf, axis=-1, keepdims=True)
        h_scr[...] = ((xf * lax.rsqrt(ms + EPS)) * wn_ref[...]).astype(BF16)
        o_ref[...] = xf

    hb = h_scr[...]
    g = jnp.dot(hb, wg_ref[...], preferred_element_type=F32)
    u = jnp.dot(hb, wu_ref[...], preferred_element_type=F32)
    a = (g * _sigmoid(g)) * u
    o_ref[...] += jnp.dot(a.astype(BF16), wd_ref[...], preferred_element_type=F32)


def _ffn(x2d, wn, wg, wu, wd):
    T, D = x2d.shape
    F = wg.shape[1]
    tm, tf = ROW_TILE, FFN_COL_TILE
    return pl.pallas_call(
        _ffn_kernel,
        grid=(T // tm, F // tf),
        in_specs=[
            pl.BlockSpec((tm, D), lambda i, f: (i, 0)),
            pl.BlockSpec((1, D), lambda i, f: (0, 0)),
            pl.BlockSpec((D, tf), lambda i, f: (0, f)),
            pl.BlockSpec((D, tf), lambda i, f: (0, f)),
            pl.BlockSpec((tf, D), lambda i, f: (f, 0)),
        ],
        out_specs=pl.BlockSpec((tm, D), lambda i, f: (i, 0)),
        out_shape=jax.ShapeDtypeStruct((T, D), F32),
        scratch_shapes=[pltpu.VMEM((tm, D), BF16)],
        compiler_params=pltpu.CompilerParams(
            dimension_semantics=("arbitrary", "arbitrary"), vmem_limit_bytes=VMEM_LIMIT_BYTES),
        name="ffn",
    )(x2d, wn, wg, wu, wd)


def _ple_kernel(x_ref, p_ref, wn_ref, wgate_ref, wproj_ref, wpost_ref, o_ref):
    xf = x_ref[...]
    ms = jnp.mean(xf * xf, axis=-1, keepdims=True)
    hb = ((xf * lax.rsqrt(ms + EPS)) * wn_ref[...]).astype(BF16)
    e = jnp.dot(p_ref[...].astype(BF16), wproj_ref[...], preferred_element_type=F32)
    ems = jnp.mean(e * e, axis=-1, keepdims=True)
    e = (e * lax.rsqrt(ems + EPS)) * wpost_ref[...]
    tn = PROJ_COL_TILE
    for c0 in range(0, o_ref.shape[1], tn):
        gate = _sigmoid(jnp.dot(hb, wgate_ref[:, c0:c0 + tn], preferred_element_type=F32))
        o_ref[:, c0:c0 + tn] = xf[:, c0:c0 + tn] + gate * e[:, c0:c0 + tn]


def _ple(x2d, p2d, wn, wgate, wproj, wpost):
    T, D = x2d.shape
    P = p2d.shape[1]
    tm = ROW_TILE
    const = lambda i: (0, 0)
    return pl.pallas_call(
        _ple_kernel,
        grid=(T // tm,),
        in_specs=[
            pl.BlockSpec((tm, D), lambda i: (i, 0)),
            pl.BlockSpec((tm, P), lambda i: (i, 0)),
            pl.BlockSpec((1, D), const),
            pl.BlockSpec((D, D), const),
            pl.BlockSpec((P, D), const),
            pl.BlockSpec((1, D), const),
        ],
        out_specs=pl.BlockSpec((tm, D), lambda i: (i, 0)),
        out_shape=jax.ShapeDtypeStruct((T, D), F32),
        compiler_params=pltpu.CompilerParams(
            dimension_semantics=("arbitrary",), vmem_limit_bytes=VMEM_LIMIT_BYTES),
        name="ple",
    )(x2d, p2d, wn, wgate, wproj, wpost)


def _layer(x2d, p2d, batch, seq, w_norm_mix, w_in, fox_f_bias, q_norm_w, k_norm_w, mlstm_conv_w, mlstm_conv_b,
           mlstm_i_bias, mlstm_f_bias, mlstm_out_norm_w, w_out, w_norm_ffn, w_ffn_gate, w_ffn_up, w_ffn_down,
           w_norm_ple, w_ple_gate, w_ple_proj, w_ple_post_norm):
    D = x2d.shape[1]
    A, QK, MV = ATTN_WIDTH, MLSTM_QK_WIDTH, MLSTM_WIDTH
    o_af = 3 * A
    o_m = o_af + ATTN_HEADS
    o_mi = o_m + 2 * QK + MV
    o_mo = o_mi + 2 * MLSTM_HEADS
    w_t = w_in.T.astype(BF16)
    w_attn, w_mlstm, w_ogate = w_t[:o_af], w_t[o_m:o_mi], w_t[o_mo:]
    wgt = jnp.concatenate([w_t[o_af:o_m], w_t[o_mi:o_mo]], axis=0)
    gbt = jnp.concatenate([fox_f_bias, mlstm_i_bias, mlstm_f_bias]).astype(F32)[:, None]
    cw = mlstm_conv_w.reshape(CONV_WIDTH, 2, QK).transpose(1, 0, 2)
    cb = mlstm_conv_b.reshape(2, 1, QK)

    proj, gcol, grow = _in_proj(x2d, w_norm_mix[None, :], w_attn, w_mlstm, w_ogate, wgt, gbt, q_norm_w[None, :],
                                k_norm_w[None, :], cw, cb, batch=batch, seq=seq)
    proj3 = proj.reshape(batch, seq, proj.shape[1])
    attn = _fox_attention(proj3, grow, gcol, batch=batch, seq=seq)
    mlstm = _mlstm(proj3, gcol, grow, mlstm_out_norm_w[None, :], batch=batch, seq=seq)

    x1 = _out_proj(x2d, attn.reshape(batch * seq, A), mlstm.reshape(batch * seq, MV),
                   w_out[:A].astype(BF16), w_out[A:].astype(BF16))
    x2 = _ffn(x1, w_norm_ffn[None, :], w_ffn_gate.astype(BF16), w_ffn_up.astype(BF16), w_ffn_down.astype(BF16))
    x3 = _ple(x2, p2d, w_norm_ple[None, :], w_ple_gate.astype(BF16), w_ple_proj.astype(BF16),
              w_ple_post_norm[None, :])
    return x3


def kernel(x, p, w_norm_mix, w_in, fox_f_bias, q_norm_w, k_norm_w, mlstm_conv_w, mlstm_conv_b, mlstm_i_bias,
           mlstm_f_bias, mlstm_out_norm_w, w_out, w_norm_ffn, w_ffn_gate, w_ffn_up, w_ffn_down, w_norm_ple,
           w_ple_gate, w_ple_proj, w_ple_post_norm):
    B, S, D = x.shape
    depth = w_in.shape[0]
    x2d = x.reshape(B * S, D)
    for i in range(depth):
        x2d = _layer(x2d, p[i].reshape(B * S, p.shape[-1]), B, S, w_norm_mix[i], w_in[i], fox_f_bias[i],
                     q_norm_w[i], k_norm_w[i], mlstm_conv_w[i], mlstm_conv_b[i], mlstm_i_bias[i], mlstm_f_bias[i],
                     mlstm_out_norm_w[i], w_out[i], w_norm_ffn[i], w_ffn_gate[i], w_ffn_up[i], w_ffn_down[i],
                     w_norm_ple[i], w_ple_gate[i], w_ple_proj[i], w_ple_post_norm[i])
    return x2d.reshape(B, S, D)
```

```python
import functools
import math

import jax
import jax.numpy as jnp
from jax import lax
from jax.experimental import pallas as pl
from jax.experimental.pallas import tpu as pltpu

F32 = jnp.float32
BF16 = jnp.bfloat16
EPS = 1e-6

ATTN_HEADS = 8
ATTN_HEAD_DIM = 128
MLSTM_HEADS = 4
MLSTM_QK_DIM = 128
MLSTM_V_DIM = 256
CONV_WIDTH = 4
ATTN_WIDTH = ATTN_HEADS * ATTN_HEAD_DIM
MLSTM_QK_WIDTH = MLSTM_HEADS * MLSTM_QK_DIM
MLSTM_WIDTH = MLSTM_HEADS * MLSTM_V_DIM
N_GATES = ATTN_HEADS + 2 * MLSTM_HEADS
GATE_LANES = 128
CONV_HALO = 8

VMEM_LIMIT_BYTES = 56 * 1024 * 1024

ROW_TILE = 512
PROJ_COL_TILE = 512
MLSTM_CHUNK = 256
ATTN_Q_TILE = 1024
ATTN_KV_TILE = 512
ATTN_HEADS_PER_STEP = 2
LOG2E = math.log2(math.e)
FOX_MAX_BOUND = 40.0
FFN_COL_TILE = 512
FFN_ROW_TILE = 1024

_T_AQ, _T_AK, _T_AV, _T_MQ, _T_MK, _T_MV, _T_MO, _T_END = 0, 2, 4, 6, 7, 8, 10, 12


def _log_sigmoid(z):
    return jnp.minimum(z, 0.0) - jnp.log1p(jnp.exp(-jnp.abs(z)))


def _sigmoid(z):
    return 1.0 / (1.0 + jnp.exp(-z))


def _split3(v):
    hi = v.astype(BF16)
    r1 = v - hi.astype(F32)
    mid = r1.astype(BF16)
    lo = (r1 - mid.astype(F32)).astype(BF16)
    return hi, mid, lo


def _lane_prefix_sum(v, period):
    axis = v.ndim - 1
    pos = lax.broadcasted_iota(jnp.int32, v.shape, axis) & (period - 1)
    shift = 1
    while shift < period:
        v = v + jnp.where(pos >= shift, pltpu.roll(v, shift, axis=axis), 0.0)
        shift *= 2
    return v


def _in_proj_kernel(x_ref, wn_ref, wa_ref, wm_ref, wo_ref, wgt_ref, gbt_ref, qn_ref, kn_ref, cw_ref, cb_ref,
                    proj_ref, gcol_ref, grow_ref, h_scr, carry_scr, conv_scr, *, n_tiles, tiles_per_seq, chunk):
    i = pl.program_id(0)
    tm = x_ref.shape[0]
    tn = PROJ_COL_TILE
    seq_start = ((i - 1) % tiles_per_seq) == 0
    slot_in, slot_out = [0], [0]

    def tile_acc(t):
        w_ref, t0 = (wa_ref, _T_AQ) if t < _T_MQ else (wm_ref, _T_MQ) if t < _T_MO else (wo_ref, _T_MO)
        return lax.dot_general(h_scr[slot_in[0]], w_ref[(t - t0) * tn:(t - t0 + 1) * tn, :],
                               (((1,), (1,)), ((), ())), preferred_element_type=F32)

    def store(t, val):
        proj_ref[:, t * tn:(t + 1) * tn] = val.astype(BF16)

    def head_norm(acc, w):
        outs = []
        for hh in range(tn // ATTN_HEAD_DIM):
            a = acc[:, hh * ATTN_HEAD_DIM:(hh + 1) * ATTN_HEAD_DIM]
            ms = jnp.mean(a * a, axis=-1, keepdims=True)
            outs.append((a * lax.rsqrt(ms + EPS)) * w)
        return jnp.concatenate(outs, axis=-1)

    def conv_silu(acc, which, scale):
        buf = conv_scr.at[which]
        buf[0:CONV_HALO, :] = jnp.where(seq_start, 0.0, buf[tm:tm + CONV_HALO, :])
        buf[CONV_HALO:CONV_HALO + tm, :] = acc
        cw = cw_ref[which]
        y = cb_ref[which] + cw[CONV_WIDTH - 1:CONV_WIDTH, :] * acc
        for tap in range(CONV_WIDTH - 1):
            off = CONV_HALO - (CONV_WIDTH - 1) + tap
            y = y + cw[tap:tap + 1, :] * buf[off:off + tm, :]
        y = y * _sigmoid(y)
        return y * scale if scale != 1.0 else y

    def norm_and_gates():
        xf = x_ref[...]
        ms = jnp.mean(xf * xf, axis=-1, keepdims=True)
        hb = ((xf * lax.rsqrt(ms + EPS)) * wn_ref[...]).astype(BF16)

        gr = lax.dot_general(wgt_ref[...], hb, (((1,), (1,)), ((), ())), preferred_element_type=F32) + gbt_ref[...]
        row = lax.broadcasted_iota(jnp.int32, gr.shape, 0)
        is_in_gate = (row >= ATTN_HEADS) & (row < ATTN_HEADS + MLSTM_HEADS)
        gr = jnp.where(is_in_gate, gr, _log_sigmoid(gr))
        cs_chunk = _lane_prefix_sum(gr, chunk)
        cs_full = _lane_prefix_sum(gr, tm)
        carry = jnp.where((i % tiles_per_seq) == 0, 0.0, carry_scr[...])
        cs_full = cs_full + carry[:, :1]
        gates = jnp.where(row < ATTN_HEADS, cs_full * LOG2E, jnp.where(is_in_gate, gr, cs_chunk))
        padded = jnp.concatenate([gates, jnp.zeros((GATE_LANES - gates.shape[0], tm), F32)], axis=0)
        h_scr[slot_out[0]] = hb
        carry_scr[...] = jnp.broadcast_to(cs_full[:, tm - 1:tm], carry_scr.shape)
        grow_ref[...] = gates
        gcol_ref[...] = padded.T

    def project(between):
        qw = qn_ref[...] * (LOG2E * ATTN_HEAD_DIM ** -0.5)
        heavy = [lambda: store(_T_MQ, conv_silu(tile_acc(_T_MQ), 0, MLSTM_QK_DIM ** -0.5)),
                 lambda: store(_T_MK, conv_silu(tile_acc(_T_MK), 1, 1.0))]
        heavy += [functools.partial(lambda t: store(t, head_norm(tile_acc(t), qw)), t) for t in range(_T_AQ, _T_AK)]
        heavy += [functools.partial(lambda t: store(t, head_norm(tile_acc(t), kn_ref[...])), t)
                  for t in range(_T_AK, _T_AV)]
        plain = [functools.partial(lambda t: store(t, tile_acc(t)), t)
                 for t in list(range(_T_AV, _T_MQ)) + list(range(_T_MV, _T_END))]
        for n, (hv, pn) in enumerate(zip(heavy, plain)):
            hv()
            pn()
            if n == 1:
                between()

    @pl.when(i == 0)
    def _():
        carry_scr[...] = jnp.zeros_like(carry_scr)
        conv_scr[...] = jnp.zeros_like(conv_scr)
        slot_out[0] = 0
        norm_and_gates()

    for parity in range(2):
        @pl.when((i > 0) & (i % 2 == parity))
        def _():
            slot_in[0], slot_out[0] = 1 - parity, parity
            project(norm_and_gates)


def _in_proj(x2d, wn, w_attn, w_mlstm, w_ogate, wgt, gbt, qn, kn, cw, cb, *, batch, seq):
    T, D = x2d.shape
    tm, tn = ROW_TILE, PROJ_COL_TILE
    n_cols = w_attn.shape[0] + w_mlstm.shape[0] + w_ogate.shape[0]
    assert w_attn.shape[0] == (_T_MQ - _T_AQ) * tn and w_mlstm.shape[0] == (_T_MO - _T_MQ) * tn
    assert n_cols == _T_END * tn
    n_tiles = T // tm
    tiles_per_seq = seq // tm
    const = lambda i: (0, 0)
    const3 = lambda i: (0, 0, 0)
    tile = lambda i: jnp.minimum(i, n_tiles - 1)
    kern = functools.partial(_in_proj_kernel, n_tiles=n_tiles, tiles_per_seq=tiles_per_seq, chunk=MLSTM_CHUNK)
    return pl.pallas_call(
        kern,
        grid=(n_tiles + 1,),
        in_specs=[
            pl.BlockSpec((tm, D), lambda i: (tile(i), 0)),
            pl.BlockSpec((1, D), const),
            pl.BlockSpec(w_attn.shape, const, pipeline_mode=pl.Buffered(1)),
            pl.BlockSpec(w_mlstm.shape, const, pipeline_mode=pl.Buffered(1)),
            pl.BlockSpec(w_ogate.shape, const, pipeline_mode=pl.Buffered(1)),
            pl.BlockSpec((N_GATES, D), const),
            pl.BlockSpec((N_GATES, 1), const),
            pl.BlockSpec((1, ATTN_HEAD_DIM), const),
            pl.BlockSpec((1, ATTN_HEAD_DIM), const),
            pl.BlockSpec((2, CONV_WIDTH, tn), const3),
            pl.BlockSpec((2, 1, tn), const3),
        ],
        out_specs=[
            pl.BlockSpec((tm, n_cols), lambda i: (jnp.maximum(i - 1, 0), 0)),
            pl.BlockSpec((tm, GATE_LANES), lambda i: (i, 0)),
            pl.BlockSpec((None, N_GATES, tm), lambda i: (i // tiles_per_seq, 0, i % tiles_per_seq)),
        ],
        out_shape=[
            jax.ShapeDtypeStruct((T, n_cols), BF16),
            jax.ShapeDtypeStruct((T + tm, GATE_LANES), F32),
            jax.ShapeDtypeStruct((batch + 1, N_GATES, seq), F32),
        ],
        scratch_shapes=[
            pltpu.VMEM((2, tm, D), BF16),
            pltpu.VMEM((N_GATES, GATE_LANES), F32),
            pltpu.VMEM((2, CONV_HALO + tm, tn), F32),
        ],
        compiler_params=pltpu.CompilerParams(
            dimension_semantics=("arbitrary",), vmem_limit_bytes=VMEM_LIMIT_BYTES),
        name="in_proj",
    )(x2d, wn, w_attn, w_mlstm, w_ogate, wgt, gbt, qn, kn, cw, cb)


def _fox_kernel(q_ref, k_ref, v_ref, c_ref, ccol_ref, o_ref, kmax_scr, *, tk):
    hg = pl.program_id(1)
    qi = pl.program_id(2)
    tq = q_ref.shape[0]
    d = ATTN_HEAD_DIM
    n_heads = q_ref.shape[1] // d
    n_diag = tq // tk
    qs = [q_ref[:, g * d:(g + 1) * d] for g in range(n_heads)]

    @pl.when(qi == 0)
    def _():
        for g in range(n_heads):
            kf = k_ref[:, g * d:(g + 1) * d].astype(F32)
            k2 = jnp.max(jnp.sum(kf * kf, axis=-1, keepdims=True), axis=0, keepdims=True)
            kmax_scr[g] = jnp.broadcast_to(k2, kmax_scr.shape[1:])

    lane = lax.broadcasted_iota(jnp.int32, ccol_ref.shape, 1)
    ccol = ccol_ref[...]
    mis, bounds = [], []
    for g in range(n_heads):
        qf = qs[g].astype(F32)
        bound = jnp.sqrt(jnp.sum(qf * qf, axis=-1, keepdims=True) * kmax_scr[g][:1, :1])
        ci = jnp.sum(jnp.where(lane == hg * n_heads + g, ccol, 0.0), axis=-1, keepdims=True)
        mis.append(bound - ci)
        bounds.append(jnp.max(bound))
    worst = functools.reduce(jnp.maximum, bounds)

    def load_block(j, g):
        start = j * tk if isinstance(j, int) else pl.multiple_of(j * tk, tk)
        k = k_ref[pl.ds(start, tk), g * d:(g + 1) * d]
        v = v_ref[pl.ds(start, tk), g * d:(g + 1) * d]
        cj = c_ref[pl.ds(hg * n_heads + g, 1), pl.ds(start, tk)]
        return k, v, cj

    def causal_mask(r0):
        r = lax.broadcasted_iota(jnp.int32, (tq - r0, tk), 0)
        c = lax.broadcasted_iota(jnp.int32, (tq - r0, tk), 1)
        return c <= r

    def rejoin(old, new, r0):
        return jnp.concatenate([old[:r0], new], axis=0) if r0 else new

    def finish(carry):
        for g in range(n_heads):
            l, acc = carry[g][-2:]
            l = jnp.sum(l, axis=-1, keepdims=True)
            o_ref[:, g * d:(g + 1) * d] = (acc / l).astype(o_ref.dtype)

    def scores(j, g, r0):
        k = load_block(j, g)[0]
        return lax.dot_general(qs[g][r0:], k, (((1,), (1,)), ((), ())), preferred_element_type=F32)

    def consume(j, g, s, l, acc, diag):
        r0 = 0 if diag is None else diag * tk
        _, v, cj = load_block(j, g)
        e = (s - cj) - mis[g][r0:]
        if diag is not None:
            e = jnp.where(causal_mask(r0), e, -jnp.inf)
        p = jnp.exp2(e)
        l_new = l[r0:] + functools.reduce(jnp.add, [p[:, c0:c0 + d] for c0 in range(0, tk, d)])
        acc_new = acc[r0:] + jnp.dot(p.astype(BF16), v, preferred_element_type=F32)
        return rejoin(l, l_new, r0), rejoin(acc, acc_new, r0)

    def bounded_step(j, carry, diag):
        r0 = 0 if diag is None else diag * tk
        return tuple(consume(j, g, scores(j, g, r0), *carry[g], diag) for g in range(n_heads))


    def online_step(j, carry, diag):
        r0 = 0 if diag is None else diag * tk
        out = []
        for g in range(n_heads):
            m, l, acc = carry[g]
            k, v, cj = load_block(j, g)
            s = lax.dot_general(qs[g][r0:], k, (((1,), (1,)), ((), ())), preferred_element_type=F32) - cj
            if diag is not None:
                s = jnp.where(causal_mask(r0), s, -jnp.inf)
            m_new = jnp.maximum(m[r0:], jnp.max(s, axis=-1, keepdims=True))
            alpha = jnp.exp2(m[r0:] - m_new)
            p = jnp.exp2(s - m_new)
            l_new = alpha * l[r0:] + jnp.sum(p, axis=-1, keepdims=True)
            acc_new = alpha * acc[r0:] + jnp.dot(p.astype(BF16), v, preferred_element_type=F32)
            out.append((rejoin(m, m_new, r0), rejoin(l, l_new, r0), rejoin(acc, acc_new, r0)))
        return tuple(out)

    def run(step, init):
        carry = lax.fori_loop(0, qi * n_diag, lambda j, c: step(j, c, None), init)
        for jj in range(n_diag):
            carry = step(qi * n_diag + jj, carry, jj)
        finish(carry)

    zeros = (jnp.zeros((tq, 1), F32), jnp.zeros((tq, d), F32))

    def run_unrolled(step, init):
        for c in range(k_ref.shape[0] // tq):
            @pl.when(qi == c)
            def _():
                carry = init
                for j in range(c * n_diag):
                    carry = step(j, carry, None)
                for jj in range(n_diag):
                    carry = step(c * n_diag + jj, carry, jj)
                finish(carry)

    @pl.when(worst <= FOX_MAX_BOUND)
    def _():
        run_unrolled(bounded_step,
                     tuple((jnp.zeros((tq, d), F32), jnp.zeros((tq, d), F32)) for _ in range(n_heads)))

    @pl.when(jnp.logical_not(worst <= FOX_MAX_BOUND))
    def _():
        run(online_step, tuple((jnp.full((tq, 1), -jnp.inf, F32),) + zeros for _ in range(n_heads)))


def _fox_attention(proj3, grow, gcol3, *, batch, seq):
    tq, tk = ATTN_Q_TILE, ATTN_KV_TILE
    assert tq % tk == 0
    dd =ATTN_HEAD_DIM * ATTN_HEADS_PER_STEP
    n_groups = ATTN_HEADS // ATTN_HEADS_PER_STEP
    grid = (batch, n_groups, seq // tq)
    return pl.pallas_call(
        functools.partial(_fox_kernel, tk=tk),
        grid=grid,
        in_specs=[
            pl.BlockSpec((None, tq, dd), lambda b, h, qi: (b, qi, h)),
            pl.BlockSpec((None, seq, dd), lambda b, h, qi: (b, 0, n_groups + h)),
            pl.BlockSpec((None, seq, dd), lambda b, h, qi: (b, 0, 2 * n_groups + h)),
            pl.BlockSpec((None, N_GATES, seq), lambda b, h, qi: (b, 0, 0)),
            pl.BlockSpec((tq, GATE_LANES), lambda b, h, qi: (b * (seq // tq) + qi, 0)),
        ],
        out_specs=pl.BlockSpec((None, tq, dd), lambda b, h, qi: (b, qi, h)),
        out_shape=jax.ShapeDtypeStruct((batch, seq, ATTN_WIDTH), BF16),
        scratch_shapes=[pltpu.VMEM((ATTN_HEADS_PER_STEP, 8, 128), F32)],
        compiler_params=pltpu.CompilerParams(
            dimension_semantics=("arbitrary", "arbitrary", "arbitrary"), vmem_limit_bytes=VMEM_LIMIT_BYTES),
        name="fox_attention",
    )(proj3, proj3, proj3, grow, gcol3)


def _mlstm_kernel(q_ref, k_ref, v_ref, og_ref, gcol_ref, grow_ref, nw_ref, o_ref, c_scr, n_scr, m_scr):
    ci = pl.program_id(1)
    L = q_ref.shape[0]
    dk, dv = MLSTM_QK_DIM, MLSTM_V_DIM

    @pl.when(ci == 0)
    def _():
        c_scr[...] = jnp.zeros_like(c_scr)
        n_scr[...] = jnp.zeros_like(n_scr)
        m_scr[...] = jnp.zeros_like(m_scr)

    t_idx = lax.broadcasted_iota(jnp.int32, (L, L), 0)
    s_idx = lax.broadcasted_iota(jnp.int32, (L, L), 1)
    causal = s_idx <= t_idx
    gcol = gcol_ref[...]
    grow = grow_ref[...]

    heads = range(MLSTM_HEADS)
    nt = (((1,), (1,)), ((), ()))
    q = [q_ref[:, hh * dk:(hh + 1) * dk] for hh in heads]
    k = [k_ref[:, hh * dk:(hh + 1) * dk] for hh in heads]
    v = [v_ref[:, hh * dv:(hh + 1) * dv] for hh in heads]
    gi = [ATTN_HEADS + hh for hh in heads]
    gf = [ATTN_HEADS + MLSTM_HEADS + hh for hh in heads]
    b_row = [grow[gf[hh]:gf[hh] + 1, :] for hh in heads]
    b_col = [gcol[:, gf[hh]:gf[hh] + 1] for hh in heads]
    a_row = [grow[gi[hh]:gi[hh] + 1, :] - b_row[hh] for hh in heads]
    a_col = [gcol[:, gi[hh]:gi[hh] + 1] - b_col[hh] for hh in heads]
    m_prev = [m_scr[hh][:1, :1] for hh in heads]
    ct = [c_scr[hh] for hh in heads]
    n = [n_scr[hh][:1, :] for hh in heads]

    qk = [lax.dot_general(q[hh], k[hh], nt, preferred_element_type=F32) for hh in heads]
    qc = [jnp.dot(q[hh], ct[hh].astype(BF16), preferred_element_type=F32) for hh in heads]
    big_m = [jnp.maximum(jnp.max(jnp.where(causal, a_row[hh], -jnp.inf), axis=-1, keepdims=True), m_prev[hh])
             for hh in heads]
    w_intra = [jnp.exp(jnp.where(causal, a_row[hh] - big_m[hh], -jnp.inf)) for hh in heads]
    w_inter = [jnp.exp(m_prev[hh] - big_m[hh]) for hh in heads]
    s_qk = [qk[hh] * w_intra[hh] for hh in heads]
    num = [jnp.dot(s_qk[hh].astype(BF16), v[hh], preferred_element_type=F32) + w_inter[hh] * qc[hh]
           for hh in heads]
    qn = [jnp.sum(q[hh].astype(F32) * n[hh], axis=-1, keepdims=True) for hh in heads]
    den = [jnp.sum(s_qk[hh], axis=-1, keepdims=True) + w_inter[hh] * qn[hh] for hh in heads]
    dmax = [jnp.maximum(jnp.abs(den[hh]), jnp.exp(-(b_col[hh] + big_m[hh]))) for hh in heads]
    ms = [jnp.mean(num[hh] * num[hh], axis=-1, keepdims=True) for hh in heads]
    for hh in heads:
        hn = num[hh] * lax.rsqrt(ms[hh] + EPS * (dmax[hh] * dmax[hh]))
        og = og_ref[:, hh * dv:(hh + 1) * dv].astype(F32)
        out = (hn * nw_ref[:, hh * dv:(hh + 1) * dv]) * _sigmoid(og)
        o_ref[:, hh * dv:(hh + 1) * dv] = out.astype(o_ref.dtype)

    m_last = [jnp.maximum(jnp.max(a_row[hh], axis=-1, keepdims=True), m_prev[hh]) for hh in heads]
    kw = [k[hh].astype(F32) * jnp.exp(a_col[hh] - m_last[hh]) for hh in heads]
    upd = [lax.dot_general(kw[hh].astype(BF16), v[hh], (((0,), (0,)), ((), ())), preferred_element_type=F32)
           for hh in heads]
    for hh in heads:
        wc = jnp.exp(m_prev[hh] - m_last[hh])
        c_scr[hh] = wc * ct[hh] + upd[hh]
        n_new = wc * n[hh] + jnp.sum(kw[hh], axis=0, keepdims=True)
        n_scr[hh] = jnp.broadcast_to(n_new, n_scr.shape[1:])
        m_scr[hh] = jnp.broadcast_to(b_row[hh][:, L - 1:L] + m_last[hh], m_scr.shape[1:])


def _mlstm(proj3, gcol3, grow, nw, *, batch, seq):
    L = MLSTM_CHUNK
    grid = (batch, seq // L)
    qk_w, v_w = MLSTM_QK_WIDTH, MLSTM_WIDTH
    base = 3 * ATTN_WIDTH
    return pl.pallas_call(
        _mlstm_kernel,
        grid=grid,
        in_specs=[
            pl.BlockSpec((None, L, qk_w), lambda b, c: (b, c, base // qk_w)),
            pl.BlockSpec((None, L, qk_w), lambda b, c: (b, c, base // qk_w + 1)),
            pl.BlockSpec((None, L, v_w), lambda b, c: (b, c, (base + 2 * qk_w) // v_w)),
            pl.BlockSpec((None, L, v_w), lambda b, c: (b, c, (base + 2 * qk_w) // v_w + 1)),
            pl.BlockSpec((L, GATE_LANES), lambda b, c: (b * (seq // L) + c, 0)),
            pl.BlockSpec((None, N_GATES, L), lambda b, c: (b, 0, c)),
            pl.BlockSpec((1, v_w), lambda b, c: (0, 0)),
        ],
        out_specs=pl.BlockSpec((None, L, v_w), lambda b, c: (b, c, 0)),
        out_shape=jax.ShapeDtypeStruct((batch, seq, v_w), BF16),
        scratch_shapes=[
            pltpu.VMEM((MLSTM_HEADS, MLSTM_QK_DIM, MLSTM_V_DIM), F32),
            pltpu.VMEM((MLSTM_HEADS, 8, MLSTM_QK_DIM), F32),
            pltpu.VMEM((MLSTM_HEADS, 8, 128), F32),
        ],
        compiler_params=pltpu.CompilerParams(
            dimension_semantics=("arbitrary", "arbitrary"), vmem_limit_bytes=VMEM_LIMIT_BYTES),
        name="mlstm",
    )(proj3, proj3, proj3, proj3, gcol3, grow, nw)


def _out_proj_kernel(x_ref, a_ref, m_ref, wa_ref, wm_ref, o_ref):
    y = jnp.dot(a_ref[...], wa_ref[...], preferred_element_type=F32)
    y = y + jnp.dot(m_ref[...], wm_ref[...], preferred_element_type=F32)
    o_ref[...] = x_ref[...] + y


def _out_proj(x2d, attn2d, mlstm2d, w_attn, w_mlstm):
    T, D = x2d.shape
    tm = ROW_TILE
    const = lambda i: (0, 0)
    return pl.pallas_call(
        _out_proj_kernel,
        grid=(T // tm,),
        in_specs=[
            pl.BlockSpec((tm, D), lambda i: (i, 0)),
            pl.BlockSpec((tm, attn2d.shape[1]), lambda i: (i, 0)),
            pl.BlockSpec((tm, mlstm2d.shape[1]), lambda i: (i, 0)),
            pl.BlockSpec(w_attn.shape, const),
            pl.BlockSpec(w_mlstm.shape, const),
        ],
        out_specs=pl.BlockSpec((tm, D), lambda i: (i, 0)),
        out_shape=jax.ShapeDtypeStruct((T, D), F32),
        compiler_params=pltpu.CompilerParams(
            dimension_semantics=("arbitrary",), vmem_limit_bytes=VMEM_LIMIT_BYTES),
        name="out_proj",
    )(x2d, attn2d, mlstm2d, w_attn, w_mlstm)


def _ffn_kernel(x_ref, wn_ref, wg_ref, wu_ref, wd_ref, o_ref, h_scr):
    f = pl.program_id(1)

    @pl.when(f == 0)
    def _():
        xf = x_ref[...]
        ms = jnp.mean(xf * xf, axis=-1, keepdims=True)
        h_scr[...] = ((xf * lax.rsqrt(ms + EPS)) * wn_ref[...]).astype(BF16)
        o_ref[...] = xf

    hb = h_scr[...]
    g = jnp.dot(hb, wg_ref[...], preferred_element_type=F32)
    u = jnp.dot(hb, wu_ref[...], preferred_element_type=F32)
    a = (g * _sigmoid(g)) * u
    o_ref[...] += jnp.dot(a.astype(BF16), wd_ref[...], preferred_element_type=F32)


def _ffn(x2d, wn, wg, wu, wd):
    T, D = x2d.shape
    F = wg.shape[1]
    tm, tf = FFN_ROW_TILE, FFN_COL_TILE
    return pl.pallas_call(
        _ffn_kernel,
        grid=(T // tm, F // tf),
        in_specs=[
            pl.BlockSpec((tm, D), lambda i, f: (i, 0)),
            pl.BlockSpec((1, D), lambda i, f: (0, 0)),
            pl.BlockSpec((D, tf), lambda i, f: (0, f)),
            pl.BlockSpec((D, tf), lambda i, f: (0, f)),
            pl.BlockSpec((tf, D), lambda i, f: (f, 0)),
        ],
        out_specs=pl.BlockSpec((tm, D), lambda i, f: (i, 0)),
        out_shape=jax.ShapeDtypeStruct((T, D), F32),
        scratch_shapes=[pltpu.VMEM((tm, D), BF16)],
        compiler_params=pltpu.CompilerParams(
            dimension_semantics=("arbitrary", "arbitrary"), vmem_limit_bytes=VMEM_LIMIT_BYTES),
        name="ffn",
    )(x2d, wn, wg, wu, wd)


def _ple_kernel(x_ref, p_ref, wn_ref, wgate_ref, wproj_ref, wpost_ref, o_ref):
    xf = x_ref[...]
    ms = jnp.mean(xf * xf, axis=-1, keepdims=True)
    hb = ((xf * lax.rsqrt(ms + EPS)) * wn_ref[...]).astype(BF16)
    e = jnp.dot(p_ref[...].astype(BF16), wproj_ref[...], preferred_element_type=F32)
    ems = jnp.mean(e * e, axis=-1, keepdims=True)
    e = (e * lax.rsqrt(ems + EPS)) * wpost_ref[...]
    tn = PROJ_COL_TILE
    for c0 in range(0, o_ref.shape[1], tn):
        gate = _sigmoid(jnp.dot(hb, wgate_ref[:, c0:c0 + tn], preferred_element_type=F32))
        o_ref[:, c0:c0 + tn] = xf[:, c0:c0 + tn] + gate * e[:, c0:c0 + tn]


def _ple(x2d, p2d, wn, wgate, wproj, wpost):
    T, D = x2d.shape
    P = p2d.shape[1]
    tm = ROW_TILE
    const = lambda i: (0, 0)
    return pl.pallas_call(
        _ple_kernel,
        grid=(T // tm,),
        in_specs=[
            pl.BlockSpec((tm, D), lambda i: (i, 0)),
            pl.BlockSpec((tm, P), lambda i: (i, 0)),
            pl.BlockSpec((1, D), const),
            pl.BlockSpec((D, D), const),
            pl.BlockSpec((P, D), const),
            pl.BlockSpec((1, D), const),
        ],
        out_specs=pl.BlockSpec((tm, D), lambda i: (i, 0)),
        out_shape=jax.ShapeDtypeStruct((T, D), F32),
        compiler_params=pltpu.CompilerParams(
            dimension_semantics=("arbitrary",), vmem_limit_bytes=VMEM_LIMIT_BYTES),
        name="ple",
    )(x2d, p2d, wn, wgate, wproj, wpost)


def _layer(x2d, p2d, batch, seq, w_norm_mix, w_in, fox_f_bias, q_norm_w, k_norm_w, mlstm_conv_w, mlstm_conv_b,
           mlstm_i_bias, mlstm_f_bias, mlstm_out_norm_w, w_out, w_norm_ffn, w_ffn_gate, w_ffn_up, w_ffn_down,
           w_norm_ple, w_ple_gate, w_ple_proj, w_ple_post_norm):
    D = x2d.shape[1]
    A, QK, MV = ATTN_WIDTH, MLSTM_QK_WIDTH, MLSTM_WIDTH
    o_af = 3 * A
    o_m = o_af + ATTN_HEADS
    o_mi = o_m + 2 * QK + MV
    o_mo = o_mi + 2 * MLSTM_HEADS
    w_t = w_in.T.astype(BF16)
    w_attn, w_mlstm, w_ogate = w_t[:o_af], w_t[o_m:o_mi], w_t[o_mo:]
    wgt = jnp.concatenate([w_t[o_af:o_m], w_t[o_mi:o_mo]], axis=0)
    gbt = jnp.concatenate([fox_f_bias, mlstm_i_bias, mlstm_f_bias]).astype(F32)[:, None]
    cw = mlstm_conv_w.reshape(CONV_WIDTH, 2, QK).transpose(1, 0, 2)
    cb = mlstm_conv_b.reshape(2, 1, QK)

    proj, gcol, grow = _in_proj(x2d, w_norm_mix[None, :], w_attn, w_mlstm, w_ogate, wgt, gbt, q_norm_w[None, :],
                                k_norm_w[None, :], cw, cb, batch=batch, seq=seq)
    proj3 = proj.reshape(batch, seq, proj.shape[1])
    attn = _fox_attention(proj3, grow, gcol, batch=batch, seq=seq)
    mlstm = _mlstm(proj3, gcol, grow, mlstm_out_norm_w[None, :], batch=batch, seq=seq)

    x1 = _out_proj(x2d, attn.reshape(batch * seq, A), mlstm.reshape(batch * seq, MV),
                   w_out[:A].astype(BF16), w_out[A:].astype(BF16))
    x2 = _ffn(x1, w_norm_ffn[None, :], w_ffn_gate.astype(BF16), w_ffn_up.astype(BF16), w_ffn_down.astype(BF16))
    x3 = _ple(x2, p2d, w_norm_ple[None, :], w_ple_gate.astype(BF16), w_ple_proj.astype(BF16),
              w_ple_post_norm[None, :])
    return x3


def kernel(x, p, w_norm_mix, w_in, fox_f_bias, q_norm_w, k_norm_w, mlstm_conv_w, mlstm_conv_b, mlstm_i_bias,
           mlstm_f_bias, mlstm_out_norm_w, w_out, w_norm_ffn, w_ffn_gate, w_ffn_up, w_ffn_down, w_norm_ple,
           w_ple_gate, w_ple_proj, w_ple_post_norm):
    B, S, D = x.shape
    depth = w_in.shape[0]
    x2d = x.reshape(B * S, D)
    for i in range(depth):
        x2d = _layer(x2d, p[i].reshape(B * S, p.shape[-1]), B, S, w_norm_mix[i], w_in[i], fox_f_bias[i],
                     q_norm_w[i], k_norm_w[i], mlstm_conv_w[i], mlstm_conv_b[i], mlstm_i_bias[i], mlstm_f_bias[i],
                     mlstm_out_norm_w[i], w_out[i], w_norm_ffn[i], w_ffn_gate[i], w_ffn_up[i], w_ffn_down[i],
                     w_norm_ple[i], w_ple_gate[i], w_ple_proj[i], w_ple_post_norm[i])
    return x2d.reshape(B, S, D)
```

```python
import functools
import math

import jax
import jax.numpy as jnp
from jax import lax
from jax.experimental import pallas as pl
from jax.experimental.pallas import tpu as pltpu

F32 = jnp.float32
BF16 = jnp.bfloat16
EPS = 1e-6

ATTN_HEADS = 8
ATTN_HEAD_DIM = 128
MLSTM_HEADS = 4
MLSTM_QK_DIM = 128
MLSTM_V_DIM = 256
CONV_WIDTH = 4
ATTN_WIDTH = ATTN_HEADS * ATTN_HEAD_DIM
MLSTM_QK_WIDTH = MLSTM_HEADS * MLSTM_QK_DIM
MLSTM_WIDTH = MLSTM_HEADS * MLSTM_V_DIM
N_GATES = ATTN_HEADS + 2 * MLSTM_HEADS
GATE_LANES = 128
CONV_HALO = 8

VMEM_LIMIT_BYTES = 56 * 1024 * 1024

ROW_TILE = 512
PROJ_COL_TILE = 512
MLSTM_CHUNK = 256
ONES_LANES = 128
MLSTM_CHUNKS_PER_STEP = 2
ATTN_Q_TILE = 1024
ATTN_KV_TILE = 512
ATTN_HEADS_PER_STEP = 2
LOG2E = math.log2(math.e)
FOX_MAX_BOUND = 40.0
FFN_COL_TILE = 512
FFN_ROW_TILE = 1024

_T_AQ, _T_AK, _T_AV, _T_MQ, _T_MK, _T_MV, _T_MO, _T_END = 0, 2, 4, 6, 7, 8, 10, 12


def _log_sigmoid(z):
    return jnp.minimum(z, 0.0) - jnp.log1p(jnp.exp(-jnp.abs(z)))


def _sigmoid(z):
    return 1.0 / (1.0 + jnp.exp(-z))


def _split3(v):
    hi = v.astype(BF16)
    r1 = v - hi.astype(F32)
    mid = r1.astype(BF16)
    lo = (r1 - mid.astype(F32)).astype(BF16)
    return hi, mid, lo


def _lane_prefix_sum(v, period):
    axis = v.ndim - 1
    pos = lax.broadcasted_iota(jnp.int32, v.shape, axis) & (period - 1)
    shift = 1
    while shift < period:
        v = v + jnp.where(pos >= shift, pltpu.roll(v, shift, axis=axis), 0.0)
        shift *= 2
    return v


def _in_proj_kernel(x_ref, wn_ref, wa_ref, wm_ref, wo_ref, wgt_ref, gbt_ref, qn_ref, kn_ref, cw_ref, cb_ref,
                    proj_ref, gcol_ref, grow_ref, h_scr, carry_scr, conv_scr, *, n_tiles, tiles_per_seq, chunk):
    i = pl.program_id(0)
    tm = x_ref.shape[0]
    tn = PROJ_COL_TILE
    seq_start = ((i - 1) % tiles_per_seq) == 0
    slot_in, slot_out = [0], [0]

    def tile_acc(t):
        w_ref, t0 = (wa_ref, _T_AQ) if t < _T_MQ else (wm_ref, _T_MQ) if t < _T_MO else (wo_ref, _T_MO)
        return lax.dot_general(h_scr[slot_in[0]], w_ref[(t - t0) * tn:(t - t0 + 1) * tn, :],
                               (((1,), (1,)), ((), ())), preferred_element_type=F32)

    def store(t, val):
        proj_ref[:, t * tn:(t + 1) * tn] = val.astype(BF16)

    def head_norm(acc, w):
        outs = []
        for hh in range(tn // ATTN_HEAD_DIM):
            a = acc[:, hh * ATTN_HEAD_DIM:(hh + 1) * ATTN_HEAD_DIM]
            ms = jnp.mean(a * a, axis=-1, keepdims=True)
            outs.append((a * lax.rsqrt(ms + EPS)) * w)
        return jnp.concatenate(outs, axis=-1)

    def conv_silu(acc, which, scale):
        buf = conv_scr.at[which]
        buf[0:CONV_HALO, :] = jnp.where(seq_start, 0.0, buf[tm:tm + CONV_HALO, :])
        buf[CONV_HALO:CONV_HALO + tm, :] = acc
        cw = cw_ref[which]
        y = cb_ref[which] + cw[CONV_WIDTH - 1:CONV_WIDTH, :] * acc
        for tap in range(CONV_WIDTH - 1):
            off = CONV_HALO - (CONV_WIDTH - 1) + tap
            y = y + cw[tap:tap + 1, :] * buf[off:off + tm, :]
        y = y * _sigmoid(y)
        return y * scale if scale != 1.0 else y

    def norm_and_gates():
        xf = x_ref[...]
        ms = jnp.mean(xf * xf, axis=-1, keepdims=True)
        hb = ((xf * lax.rsqrt(ms + EPS)) * wn_ref[...]).astype(BF16)

        gr = lax.dot_general(wgt_ref[...], hb, (((1,), (1,)), ((), ())), preferred_element_type=F32) + gbt_ref[...]
        row = lax.broadcasted_iota(jnp.int32, gr.shape, 0)
        is_in_gate = (row >= ATTN_HEADS) & (row < ATTN_HEADS + MLSTM_HEADS)
        gr = jnp.where(is_in_gate, gr, _log_sigmoid(gr))
        cs_chunk = _lane_prefix_sum(gr, chunk)
        cs_full = _lane_prefix_sum(gr, tm)
        carry = jnp.where((i % tiles_per_seq) == 0, 0.0, carry_scr[...])
        cs_full = cs_full + carry[:, :1]
        b_on_in_rows = pltpu.roll(cs_chunk, N_GATES - MLSTM_HEADS, axis=0)
        gates = jnp.where(row < ATTN_HEADS, cs_full * LOG2E, jnp.where(is_in_gate, gr - b_on_in_rows, cs_chunk))
        padded = jnp.concatenate([gates, jnp.zeros((GATE_LANES - gates.shape[0], tm), F32)], axis=0)
        h_scr[slot_out[0]] = hb
        carry_scr[...] = jnp.broadcast_to(cs_full[:, tm - 1:tm], carry_scr.shape)
        grow_ref[...] = gates
        gcol_ref[...] = padded.T

    def project(between):
        qw = qn_ref[...] * (LOG2E * ATTN_HEAD_DIM ** -0.5)
        heavy = [lambda: store(_T_MQ, conv_silu(tile_acc(_T_MQ), 0, MLSTM_QK_DIM ** -0.5)),
                 lambda: store(_T_MK, conv_silu(tile_acc(_T_MK), 1, 1.0))]
        heavy += [functools.partial(lambda t: store(t, head_norm(tile_acc(t), qw)), t) for t in range(_T_AQ, _T_AK)]
        heavy += [functools.partial(lambda t: store(t, head_norm(tile_acc(t), kn_ref[...])), t)
                  for t in range(_T_AK, _T_AV)]
        plain = [functools.partial(lambda t: store(t, tile_acc(t)), t)
                 for t in list(range(_T_AV, _T_MQ)) + list(range(_T_MV, _T_END))]
        for n, (hv, pn) in enumerate(zip(heavy, plain)):
            hv()
            pn()
            if n == 1:
                between()

    @pl.when(i == 0)
    def _():
        carry_scr[...] = jnp.zeros_like(carry_scr)
        conv_scr[...] = jnp.zeros_like(conv_scr)
        slot_out[0] = 0
        norm_and_gates()

    for parity in range(2):
        @pl.when((i > 0) & (i % 2 == parity))
        def _():
            slot_in[0], slot_out[0] = 1 - parity, parity
            project(norm_and_gates)


def _in_proj(x2d, wn, w_attn, w_mlstm, w_ogate, wgt, gbt, qn, kn, cw, cb, *, batch, seq):
    T, D = x2d.shape
    tm, tn = ROW_TILE, PROJ_COL_TILE
    n_cols = w_attn.shape[0] + w_mlstm.shape[0] + w_ogate.shape[0]
    assert w_attn.shape[0] == (_T_MQ - _T_AQ) * tn and w_mlstm.shape[0] == (_T_MO - _T_MQ) * tn
    assert n_cols == _T_END * tn
    n_tiles = T // tm
    tiles_per_seq = seq // tm
    const = lambda i: (0, 0)
    const3 = lambda i: (0, 0, 0)
    tile = lambda i: jnp.minimum(i, n_tiles - 1)
    kern = functools.partial(_in_proj_kernel, n_tiles=n_tiles, tiles_per_seq=tiles_per_seq, chunk=MLSTM_CHUNK)
    return pl.pallas_call(
        kern,
        grid=(n_tiles + 1,),
        in_specs=[
            pl.BlockSpec((tm, D), lambda i: (tile(i), 0)),
            pl.BlockSpec((1, D), const),
            pl.BlockSpec(w_attn.shape, const, pipeline_mode=pl.Buffered(1)),
            pl.BlockSpec(w_mlstm.shape, const, pipeline_mode=pl.Buffered(1)),
            pl.BlockSpec(w_ogate.shape, const, pipeline_mode=pl.Buffered(1)),
            pl.BlockSpec((N_GATES, D), const),
            pl.BlockSpec((N_GATES, 1), const),
            pl.BlockSpec((1, ATTN_HEAD_DIM), const),
            pl.BlockSpec((1, ATTN_HEAD_DIM), const),
            pl.BlockSpec((2, CONV_WIDTH, tn), const3),
            pl.BlockSpec((2, 1, tn), const3),
        ],
        out_specs=[
            pl.BlockSpec((tm, n_cols), lambda i: (jnp.maximum(i - 1, 0), 0)),
            pl.BlockSpec((tm, GATE_LANES), lambda i: (i, 0)),
            pl.BlockSpec((None, N_GATES, tm), lambda i: (i, 0, 0)),
        ],
        out_shape=[
            jax.ShapeDtypeStruct((T, n_cols), BF16),
            jax.ShapeDtypeStruct((T + tm, GATE_LANES), F32),
            jax.ShapeDtypeStruct((n_tiles + 1, N_GATES, tm), F32),
        ],
        scratch_shapes=[
            pltpu.VMEM((2, tm, D), BF16),
            pltpu.VMEM((N_GATES, GATE_LANES), F32),
            pltpu.VMEM((2, CONV_HALO + tm, tn), F32),
        ],
        compiler_params=pltpu.CompilerParams(
            dimension_semantics=("arbitrary",), vmem_limit_bytes=VMEM_LIMIT_BYTES),
        name="in_proj",
    )(x2d, wn, w_attn, w_mlstm, w_ogate, wgt, gbt, qn, kn, cw, cb)


def _fox_kernel(q_ref, k_ref, v_ref, c_ref, ccol_ref, o_ref, kmax_scr, *, tk):
    hg = pl.program_id(1)
    qi = pl.program_id(2)
    tq = q_ref.shape[0]
    d = ATTN_HEAD_DIM
    n_heads = q_ref.shape[1] // d
    n_diag = tq // tk
    qs = [q_ref[:, g * d:(g + 1) * d] for g in range(n_heads)]

    @pl.when(qi == 0)
    def _():
        for g in range(n_heads):
            kf = k_ref[:, g * d:(g + 1) * d].astype(F32)
            k2 = jnp.max(jnp.sum(kf * kf, axis=-1, keepdims=True), axis=0, keepdims=True)
            kmax_scr[g] = jnp.broadcast_to(k2, kmax_scr.shape[1:])

    lane = lax.broadcasted_iota(jnp.int32, ccol_ref.shape, 1)
    ccol = ccol_ref[...]
    mis, bounds = [], []
    for g in range(n_heads):
        qf = qs[g].astype(F32)
        bound = jnp.sqrt(jnp.sum(qf * qf, axis=-1, keepdims=True) * kmax_scr[g][:1, :1])
        ci = jnp.sum(jnp.where(lane == hg * n_heads + g, ccol, 0.0), axis=-1, keepdims=True)
        mis.append(bound - ci)
        bounds.append(jnp.max(bound))
    worst = functools.reduce(jnp.maximum, bounds)

    def load_block(j, g):
        start = j * tk if isinstance(j, int) else pl.multiple_of(j * tk, tk)
        k = k_ref[pl.ds(start, tk), g * d:(g + 1) * d]
        v = v_ref[pl.ds(start, tk), g * d:(g + 1) * d]
        per_tile = c_ref.shape[2] // tk
        cj = c_ref[j // per_tile, pl.ds(hg * n_heads + g, 1), pl.ds((j % per_tile) * tk, tk)]
        return k, v, cj

    def causal_mask(r0):
        r = lax.broadcasted_iota(jnp.int32, (tq - r0, tk), 0)
        c = lax.broadcasted_iota(jnp.int32, (tq - r0, tk), 1)
        return c <= r

    def rejoin(old, new, r0):
        return jnp.concatenate([old[:r0], new], axis=0) if r0 else new

    def finish(carry):
        for g in range(n_heads):
            l, acc = carry[g][-2:]
            l = jnp.sum(l, axis=-1, keepdims=True)
            o_ref[:, g * d:(g + 1) * d] = (acc / l).astype(o_ref.dtype)

    def scores(j, g, r0):
        k = load_block(j, g)[0]
        return lax.dot_general(qs[g][r0:], k, (((1,), (1,)), ((), ())), preferred_element_type=F32)

    def consume(j, g, s, l, acc, diag):
        r0 = 0 if diag is None else diag * tk
        _, v, cj = load_block(j, g)
        e = (s - cj) - mis[g][r0:]
        if diag is not None:
            e = jnp.where(causal_mask(r0), e, -jnp.inf)
        p = jnp.exp2(e)
        l_new = l[r0:] + functools.reduce(jnp.add, [p[:, c0:c0 + d] for c0 in range(0, tk, d)])
        acc_new = acc[r0:] + jnp.dot(p.astype(BF16), v, preferred_element_type=F32)
        return rejoin(l, l_new, r0), rejoin(acc, acc_new, r0)

    def bounded_step(j, carry, diag):
        r0 = 0 if diag is None else diag * tk
        return tuple(consume(j, g, scores(j, g, r0), *carry[g], diag) for g in range(n_heads))


    def online_step(j, carry, diag):
        r0 = 0 if diag is None else diag * tk
        out = []
        for g in range(n_heads):
            m, l, acc = carry[g]
            k, v, cj = load_block(j, g)
            s = lax.dot_general(qs[g][r0:], k, (((1,), (1,)), ((), ())), preferred_element_type=F32) - cj
            if diag is not None:
                s = jnp.where(causal_mask(r0), s, -jnp.inf)
            m_new = jnp.maximum(m[r0:], jnp.max(s, axis=-1, keepdims=True))
            alpha = jnp.exp2(m[r0:] - m_new)
            p = jnp.exp2(s - m_new)
            l_new = alpha * l[r0:] + jnp.sum(p, axis=-1, keepdims=True)
            acc_new = alpha * acc[r0:] + jnp.dot(p.astype(BF16), v, preferred_element_type=F32)
            out.append((rejoin(m, m_new, r0), rejoin(l, l_new, r0), rejoin(acc, acc_new, r0)))
        return tuple(out)

    def run(step, init):
        carry = lax.fori_loop(0, qi * n_diag, lambda j, c: step(j, c, None), init)
        for jj in range(n_diag):
            carry = step(qi * n_diag + jj, carry, jj)
        finish(carry)

    zeros = (jnp.zeros((tq, 1), F32), jnp.zeros((tq, d), F32))

    def run_unrolled(step, init):
        for c in range(k_ref.shape[0] // tq):
            @pl.when(qi == c)
            def _():
                carry = init
                for j in range(c * n_diag):
                    carry = step(j, carry, None)
                for jj in range(n_diag):
                    carry = step(c * n_diag + jj, carry, jj)
                finish(carry)

    @pl.when(worst <= FOX_MAX_BOUND)
    def _():
        run_unrolled(bounded_step,
                     tuple((jnp.zeros((tq, d), F32), jnp.zeros((tq, d), F32)) for _ in range(n_heads)))

    @pl.when(jnp.logical_not(worst <= FOX_MAX_BOUND))
    def _():
        run(online_step, tuple((jnp.full((tq, 1), -jnp.inf, F32),) + zeros for _ in range(n_heads)))


def _fox_attention(proj3, grow, gcol3, *, batch, seq):
    tq, tk = ATTN_Q_TILE, ATTN_KV_TILE
    assert tq % tk == 0
    dd =ATTN_HEAD_DIM * ATTN_HEADS_PER_STEP
    n_groups = ATTN_HEADS // ATTN_HEADS_PER_STEP
    grid = (batch, n_groups, seq // tq)
    return pl.pallas_call(
        functools.partial(_fox_kernel, tk=tk),
        grid=grid,
        in_specs=[
            pl.BlockSpec((None, tq, dd), lambda b, h, qi: (b, qi, h)),
            pl.BlockSpec((None, seq, dd), lambda b, h, qi: (b, 0, n_groups + h)),
            pl.BlockSpec((None, seq, dd), lambda b, h, qi: (b, 0, 2 * n_groups + h)),
            pl.BlockSpec((seq // ROW_TILE, N_GATES, ROW_TILE), lambda b, h, qi: (b, 0, 0)),
            pl.BlockSpec((tq, GATE_LANES), lambda b, h, qi: (b * (seq // tq) + qi, 0)),
        ],
        out_specs=pl.BlockSpec((None, tq, dd), lambda b, h, qi: (b, qi, h)),
        out_shape=jax.ShapeDtypeStruct((batch, seq, ATTN_WIDTH), BF16),
        scratch_shapes=[pltpu.VMEM((ATTN_HEADS_PER_STEP, 8, 128), F32)],
        compiler_params=pltpu.CompilerParams(
            dimension_semantics=("arbitrary", "arbitrary", "arbitrary"), vmem_limit_bytes=VMEM_LIMIT_BYTES),
        name="fox_attention",
    )(proj3, proj3, proj3, grow, gcol3)


def _mlstm_kernel(q_ref, k_ref, v_ref, og_ref, gcol_ref, grow_ref, nw_ref, o_ref, c_scr, m_scr):
    ci = pl.program_id(1)
    L = MLSTM_CHUNK
    dk, dv = MLSTM_QK_DIM, MLSTM_V_DIM

    @pl.when(ci == 0)
    def _():
        c_scr[...] = jnp.zeros_like(c_scr)
        m_scr[...] = jnp.zeros_like(m_scr)

    t_idx = lax.broadcasted_iota(jnp.int32, (L, L), 0)
    s_idx = lax.broadcasted_iota(jnp.int32, (L, L), 1)
    causal = s_idx <= t_idx
    nt = (((1,), (1,)), ((), ()))
    heads = range(MLSTM_HEADS)
    units = [(c, hh) for c in range(q_ref.shape[0] // L) for hh in heads]

    rows = {u: slice(u[0] * L, (u[0] + 1) * L) for u in units}
    q = {u: q_ref[rows[u], u[1] * dk:(u[1] + 1) * dk] for u in units}
    k = {u: k_ref[rows[u], u[1] * dk:(u[1] + 1) * dk] for u in units}
    v = {u: v_ref[rows[u], u[1] * dv:(u[1] + 1) * dv] for u in units}
    gi = {u: ATTN_HEADS + u[1] for u in units}
    gf = {u: ATTN_HEADS + MLSTM_HEADS + u[1] for u in units}
    b_row = {u: grow_ref[gf[u]:gf[u] + 1, rows[u]] for u in units}
    b_col = {u: gcol_ref[rows[u], gf[u]:gf[u] + 1] for u in units}
    a_row = {u: grow_ref[gi[u]:gi[u] + 1, rows[u]] for u in units}
    a_col = {u: gcol_ref[rows[u], gi[u]:gi[u] + 1] for u in units}
    ones = jnp.ones((L, ONES_LANES), BF16)
    v1 = {u: jnp.concatenate([v[u], ones], axis=-1) for u in units}
    qk = {u: lax.dot_general(q[u], k[u], nt, preferred_element_type=F32) for u in units}
    m_loc = {u: jnp.max(jnp.where(causal, a_row[u], -jnp.inf), axis=-1, keepdims=True) for u in units}
    s_loc = {u: qk[u] * jnp.exp(jnp.where(causal, a_row[u] - m_loc[u], -jnp.inf)) for u in units}
    pv1 = {u: jnp.dot(s_loc[u].astype(BF16), v1[u], preferred_element_type=F32) for u in units}
    pv_loc = {u: pv1[u][:, :dv] for u in units}
    d_loc = {u: pv1[u][:, dv:dv + 1] for u in units}
    a_max = {u: jnp.max(a_row[u], axis=-1, keepdims=True) for u in units}
    kw = {u: (k[u].astype(F32) * jnp.exp(a_col[u] - a_max[u])).astype(BF16) for u in units}
    upd1 = {u: lax.dot_general(kw[u], v1[u], (((0,), (0,)), ((), ())), preferred_element_type=F32)
            for u in units}

    m_prev = [m_scr[hh][:1, :1] for hh in heads]
    ct1 = [c_scr[hh] for hh in heads]
    for c in range(q_ref.shape[0] // L):
        us = [(c, hh) for hh in heads]
        qc1 = [jnp.dot(q[u], ct1[u[1]].astype(BF16), preferred_element_type=F32) for u in us]
        qc = [x[:, :dv] for x in qc1]
        qn = [x[:, dv:dv + 1] for x in qc1]
        big_m = [jnp.maximum(m_loc[u], m_prev[u[1]]) for u in us]
        alpha = [jnp.exp(m_loc[u] - big_m[hh]) for hh, u in enumerate(us)]
        beta = [jnp.exp(m_prev[hh] - big_m[hh]) for hh in heads]
        num = [alpha[hh] * pv_loc[u] + beta[hh] * qc[hh] for hh, u in enumerate(us)]
        den = [alpha[hh] * d_loc[u] + beta[hh] * qn[hh] for hh, u in enumerate(us)]
        dmax = [jnp.maximum(jnp.abs(den[hh]), jnp.exp(-(b_col[u] + big_m[hh]))) for hh, u in enumerate(us)]
        ms = [jnp.mean(num[hh] * num[hh], axis=-1, keepdims=True) for hh in heads]
        for hh, u in enumerate(us):
            hn = num[hh] * lax.rsqrt(ms[hh] + EPS * (dmax[hh] * dmax[hh]))
            og = og_ref[rows[u], hh * dv:(hh + 1) * dv].astype(F32)
            out = (hn * nw_ref[:, hh * dv:(hh + 1) * dv]) * _sigmoid(og)
            o_ref[rows[u], hh * dv:(hh + 1) * dv] = out.astype(o_ref.dtype)
        m_last = [jnp.maximum(a_max[u], m_prev[u[1]]) for u in us]
        wc = [jnp.exp(m_prev[hh] - m_last[hh]) for hh in heads]
        gamma = [jnp.exp(a_max[u] - m_last[hh]) for hh, u in enumerate(us)]
        ct1 = [wc[hh] * ct1[hh] + gamma[hh] * upd1[u] for hh, u in enumerate(us)]
        m_prev = [b_row[u][:, L - 1:L] + m_last[hh] for hh, u in enumerate(us)]

    for hh in heads:
        c_scr[hh] = ct1[hh]
        m_scr[hh] = jnp.broadcast_to(m_prev[hh], m_scr.shape[1:])


def _mlstm(proj3, gcol3, grow, nw, *, batch, seq):
    L = MLSTM_CHUNK * MLSTM_CHUNKS_PER_STEP
    assert L == ROW_TILE
    grid = (batch, seq // L)
    qk_w, v_w = MLSTM_QK_WIDTH, MLSTM_WIDTH
    base = 3 * ATTN_WIDTH
    return pl.pallas_call(
        _mlstm_kernel,
        grid=grid,
        in_specs=[
            pl.BlockSpec((None, L, qk_w), lambda b, c: (b, c, base // qk_w)),
            pl.BlockSpec((None, L, qk_w), lambda b, c: (b, c, base // qk_w + 1)),
            pl.BlockSpec((None, L, v_w), lambda b, c: (b, c, (base + 2 * qk_w) // v_w)),
            pl.BlockSpec((None, L, v_w), lambda b, c: (b, c, (base + 2 * qk_w) // v_w + 1)),
            pl.BlockSpec((L, GATE_LANES), lambda b, c: (b * (seq // L) + c, 0)),
            pl.BlockSpec((None, N_GATES, L), lambda b, c: (b * (seq // L) + c, 0, 0)),
            pl.BlockSpec((1, v_w), lambda b, c: (0, 0)),
        ],
        out_specs=pl.BlockSpec((None, L, v_w), lambda b, c: (b, c, 0)),
        out_shape=jax.ShapeDtypeStruct((batch, seq, v_w), BF16),
        scratch_shapes=[
            pltpu.VMEM((MLSTM_HEADS, MLSTM_QK_DIM, MLSTM_V_DIM + ONES_LANES), F32),
            pltpu.VMEM((MLSTM_HEADS, 8, 128), F32),
        ],
        compiler_params=pltpu.CompilerParams(
            dimension_semantics=("arbitrary", "arbitrary"), vmem_limit_bytes=VMEM_LIMIT_BYTES),
        name="mlstm",
    )(proj3, proj3, proj3, proj3, gcol3, grow, nw)


def _out_proj_kernel(x_ref, a_ref, m_ref, wa_ref, wm_ref, o_ref):
    y = jnp.dot(a_ref[...], wa_ref[...], preferred_element_type=F32)
    y = y + jnp.dot(m_ref[...], wm_ref[...], preferred_element_type=F32)
    o_ref[...] = x_ref[...] + y


def _out_proj(x2d, attn2d, mlstm2d, w_attn, w_mlstm):
    T, D = x2d.shape
    tm = ROW_TILE
    const = lambda i: (0, 0)
    return pl.pallas_call(
        _out_proj_kernel,
        grid=(T // tm,),
        in_specs=[
            pl.BlockSpec((tm, D), lambda i: (i, 0)),
            pl.BlockSpec((tm, attn2d.shape[1]), lambda i: (i, 0)),
            pl.BlockSpec((tm, mlstm2d.shape[1]), lambda i: (i, 0)),
            pl.BlockSpec(w_attn.shape, const),
            pl.BlockSpec(w_mlstm.shape, const),
        ],
        out_specs=pl.BlockSpec((tm, D), lambda i: (i, 0)),
        out_shape=jax.ShapeDtypeStruct((T, D), F32),
        compiler_params=pltpu.CompilerParams(
            dimension_semantics=("arbitrary",), vmem_limit_bytes=VMEM_LIMIT_BYTES),
        name="out_proj",
    )(x2d, attn2d, mlstm2d, w_attn, w_mlstm)


def _ffn_kernel(x_ref, wn_ref, wg_ref, wu_ref, wd_ref, o_ref, h_scr):
    f = pl.program_id(1)

    @pl.when(f == 0)
    def _():
        xf = x_ref[...]
        ms = jnp.mean(xf * xf, axis=-1, keepdims=True)
        h_scr[...] = ((xf * lax.rsqrt(ms + EPS)) * wn_ref[...]).astype(BF16)
        o_ref[...] = xf

    hb = h_scr[...]
    g = jnp.dot(hb, wg_ref[...], preferred_element_type=F32)
    u = jnp.dot(hb, wu_ref[...], preferred_element_type=F32)
    a = (g * _sigmoid(g)) * u
    o_ref[...] += jnp.dot(a.astype(BF16), wd_ref[...], preferred_element_type=F32)


def _ffn(x2d, wn, wg, wu, wd):
    T, D = x2d.shape
    F = wg.shape[1]
    tm, tf = FFN_ROW_TILE, FFN_COL_TILE
    return pl.pallas_call(
        _ffn_kernel,
        grid=(T // tm, F // tf),
        in_specs=[
            pl.BlockSpec((tm, D), lambda i, f: (i, 0)),
            pl.BlockSpec((1, D), lambda i, f: (0, 0)),
            pl.BlockSpec((D, tf), lambda i, f: (0, f)),
            pl.BlockSpec((D, tf), lambda i, f: (0, f)),
            pl.BlockSpec((tf, D), lambda i, f: (f, 0)),
        ],
        out_specs=pl.BlockSpec((tm, D), lambda i, f: (i, 0)),
        out_shape=jax.ShapeDtypeStruct((T, D), F32),
        scratch_shapes=[pltpu.VMEM((tm, D), BF16)],
        compiler_params=pltpu.CompilerParams(
            dimension_semantics=("arbitrary", "arbitrary"), vmem_limit_bytes=VMEM_LIMIT_BYTES),
        name="ffn",
    )(x2d, wn, wg, wu, wd)


def _ple_kernel(x_ref, p_ref, wn_ref, wgate_ref, wproj_ref, wpost_ref, o_ref):
    xf = x_ref[...]
    ms = jnp.mean(xf * xf, axis=-1, keepdims=True)
    hb = ((xf * lax.rsqrt(ms + EPS)) * wn_ref[...]).astype(BF16)
    e = jnp.dot(p_ref[...].astype(BF16), wproj_ref[...], preferred_element_type=F32)
    ems = jnp.mean(e * e, axis=-1, keepdims=True)
    e = (e * lax.rsqrt(ems + EPS)) * wpost_ref[...]
    tn = PROJ_COL_TILE
    for c0 in range(0, o_ref.shape[1], tn):
        gate = _sigmoid(jnp.dot(hb, wgate_ref[:, c0:c0 + tn], preferred_element_type=F32))
        o_ref[:, c0:c0 + tn] = xf[:, c0:c0 + tn] + gate * e[:, c0:c0 + tn]


def _ple(x2d, p2d, wn, wgate, wproj, wpost):
    T, D = x2d.shape
    P = p2d.shape[1]
    tm = ROW_TILE
    const = lambda i: (0, 0)
    return pl.pallas_call(
        _ple_kernel,
        grid=(T // tm,),
        in_specs=[
            pl.BlockSpec((tm, D), lambda i: (i, 0)),
            pl.BlockSpec((tm, P), lambda i: (i, 0)),
            pl.BlockSpec((1, D), const),
            pl.BlockSpec((D, D), const),
            pl.BlockSpec((P, D), const),
            pl.BlockSpec((1, D), const),
        ],
        out_specs=pl.BlockSpec((tm, D), lambda i: (i, 0)),
        out_shape=jax.ShapeDtypeStruct((T, D), F32),
        compiler_params=pltpu.CompilerParams(
            dimension_semantics=("arbitrary",), vmem_limit_bytes=VMEM_LIMIT_BYTES),
        name="ple",
    )(x2d, p2d, wn, wgate, wproj, wpost)


def _layer(x2d, p2d, batch, seq, w_norm_mix, w_in, fox_f_bias, q_norm_w, k_norm_w, mlstm_conv_w, mlstm_conv_b,
           mlstm_i_bias, mlstm_f_bias, mlstm_out_norm_w, w_out, w_norm_ffn, w_ffn_gate, w_ffn_up, w_ffn_down,
           w_norm_ple, w_ple_gate, w_ple_proj, w_ple_post_norm):
    D = x2d.shape[1]
    A, QK, MV = ATTN_WIDTH, MLSTM_QK_WIDTH, MLSTM_WIDTH
    o_af = 3 * A
    o_m = o_af + ATTN_HEADS
    o_mi = o_m + 2 * QK + MV
    o_mo = o_mi + 2 * MLSTM_HEADS
    w_t = w_in.T.astype(BF16)
    w_attn, w_mlstm, w_ogate = w_t[:o_af], w_t[o_m:o_mi], w_t[o_mo:]
    wgt = jnp.concatenate([w_t[o_af:o_m], w_t[o_mi:o_mo]], axis=0)
    gbt = jnp.concatenate([fox_f_bias, mlstm_i_bias, mlstm_f_bias]).astype(F32)[:, None]
    cw = mlstm_conv_w.reshape(CONV_WIDTH, 2, QK).transpose(1, 0, 2)
    cb = mlstm_conv_b.reshape(2, 1, QK)

    proj, gcol, grow = _in_proj(x2d, w_norm_mix[None, :], w_attn, w_mlstm, w_ogate, wgt, gbt, q_norm_w[None, :],
                                k_norm_w[None, :], cw, cb, batch=batch, seq=seq)
    proj3 = proj.reshape(batch, seq, proj.shape[1])
    attn = _fox_attention(proj3, grow, gcol, batch=batch, seq=seq)
    mlstm = _mlstm(proj3, gcol, grow, mlstm_out_norm_w[None, :], batch=batch, seq=seq)

    x1 = _out_proj(x2d, attn.reshape(batch * seq, A), mlstm.reshape(batch * seq, MV),
                   w_out[:A].astype(BF16), w_out[A:].astype(BF16))
    x2 = _ffn(x1, w_norm_ffn[None, :], w_ffn_gate.astype(BF16), w_ffn_up.astype(BF16), w_ffn_down.astype(BF16))
    x3 = _ple(x2, p2d, w_norm_ple[None, :], w_ple_gate.astype(BF16), w_ple_proj.astype(BF16),
              w_ple_post_norm[None, :])
    return x3


def kernel(x, p, w_norm_mix, w_in, fox_f_bias, q_norm_w, k_norm_w, mlstm_conv_w, mlstm_conv_b, mlstm_i_bias,
           mlstm_f_bias, mlstm_out_norm_w, w_out, w_norm_ffn, w_ffn_gate, w_ffn_up, w_ffn_down, w_norm_ple,
           w_ple_gate, w_ple_proj, w_ple_post_norm):
    B, S, D = x.shape
    depth = w_in.shape[0]
    x2d = x.reshape(B * S, D)
    for i in range(depth):
        x2d = _layer(x2d, p[i].reshape(B * S, p.shape[-1]), B, S, w_norm_mix[i], w_in[i], fox_f_bias[i],
                     q_norm_w[i], k_norm_w[i], mlstm_conv_w[i], mlstm_conv_b[i], mlstm_i_bias[i], mlstm_f_bias[i],
                     mlstm_out_norm_w[i], w_out[i], w_norm_ffn[i], w_ffn_gate[i], w_ffn_up[i], w_ffn_down[i],
                     w_norm_ple[i], w_ple_gate[i], w_ple_proj[i], w_ple_post_norm[i])
    return x2d.reshape(B, S, D)
```

```python
import functools
import math

import jax
import jax.numpy as jnp
from jax import lax
from jax.experimental import pallas as pl
from jax.experimental.pallas import tpu as pltpu

F32 = jnp.float32
BF16 = jnp.bfloat16
EPS = 1e-6

ATTN_HEADS = 8
ATTN_HEAD_DIM = 128
MLSTM_HEADS = 4
MLSTM_QK_DIM = 128
MLSTM_V_DIM = 256
CONV_WIDTH = 4
ATTN_WIDTH = ATTN_HEADS * ATTN_HEAD_DIM
MLSTM_QK_WIDTH = MLSTM_HEADS * MLSTM_QK_DIM
MLSTM_WIDTH = MLSTM_HEADS * MLSTM_V_DIM
N_GATES = ATTN_HEADS + 2 * MLSTM_HEADS
GATE_LANES = 128
CONV_HALO = 8
BF16_SUBLANES = 16

VMEM_LIMIT_BYTES = 56 * 1024 * 1024

ROW_TILE = 512
PROJ_COL_TILE = 512
MLSTM_CHUNK = 256
ONES_LANES = 128
MLSTM_CHUNKS_PER_STEP = 2
ATTN_Q_TILE = 1024
ATTN_KV_TILE = 512
ATTN_HEADS_PER_STEP = 2
LOG2E = math.log2(math.e)
FOX_MAX_BOUND = 40.0
FFN_COL_TILE = 512
FFN_ROW_TILE = 1024

_T_AQ, _T_AK, _T_AV, _T_MQ, _T_MK, _T_MV, _T_MO, _T_END = 0, 2, 4, 6, 7, 8, 10, 12


def _log_sigmoid(z):
    return jnp.minimum(z, 0.0) - jnp.log1p(jnp.exp(-jnp.abs(z)))


def _sigmoid(z):
    return 1.0 / (1.0 + jnp.exp(-z))


def _split3(v):
    hi = v.astype(BF16)
    r1 = v - hi.astype(F32)
    mid = r1.astype(BF16)
    lo = (r1 - mid.astype(F32)).astype(BF16)
    return hi, mid, lo


def _lane_prefix_sum(v, period):
    axis = v.ndim - 1
    pos = lax.broadcasted_iota(jnp.int32, v.shape, axis) & (period - 1)
    shift = 1
    while shift < period:
        v = v + jnp.where(pos >= shift, pltpu.roll(v, shift, axis=axis), 0.0)
        shift *= 2
    return v


def _in_proj_kernel(x_ref, wn_ref, wa_ref, wm_ref, wo_ref, wgt_ref, gbt_ref, qn_ref, kn_ref, cw_ref, cb_ref,
                    proj_ref, gcol_ref, grow_ref, h_scr, carry_scr, conv_scr, *, n_tiles, tiles_per_seq, chunk):
    i = pl.program_id(0)
    tm = x_ref.shape[0]
    tn = PROJ_COL_TILE
    seq_start = ((i - 1) % tiles_per_seq) == 0
    slot_in, slot_out = [0], [0]

    def tile_acc(t):
        w_ref, t0 = (wa_ref, _T_AQ) if t < _T_MQ else (wm_ref, _T_MQ) if t < _T_MO else (wo_ref, _T_MO)
        return lax.dot_general(h_scr[slot_in[0]], w_ref[(t - t0) * tn:(t - t0 + 1) * tn, :],
                               (((1,), (1,)), ((), ())), preferred_element_type=F32)

    def store(t, val):
        proj_ref[:, t * tn:(t + 1) * tn] = val.astype(BF16)

    def head_norm(acc, w):
        outs = []
        for hh in range(tn // ATTN_HEAD_DIM):
            a = acc[:, hh * ATTN_HEAD_DIM:(hh + 1) * ATTN_HEAD_DIM]
            ms = jnp.mean(a * a, axis=-1, keepdims=True)
            outs.append((a * lax.rsqrt(ms + EPS)) * w)
        return jnp.concatenate(outs, axis=-1)

    def conv_silu(acc, which, scale):
        buf = conv_scr.at[which]
        buf[0:CONV_HALO, :] = jnp.where(seq_start, 0.0, buf[tm:tm + CONV_HALO, :])
        buf[CONV_HALO:CONV_HALO + tm, :] = acc
        cw = cw_ref[which]
        y = cb_ref[which] + cw[CONV_WIDTH - 1:CONV_WIDTH, :] * acc
        for tap in range(CONV_WIDTH - 1):
            off = CONV_HALO - (CONV_WIDTH - 1) + tap
            y = y + cw[tap:tap + 1, :] * buf[off:off + tm, :]
        y = y * _sigmoid(y)
        return y * scale if scale != 1.0 else y

    def norm_and_gates():
        xf = x_ref[...]
        ms = jnp.mean(xf * xf, axis=-1, keepdims=True)
        hb = ((xf * lax.rsqrt(ms + EPS)) * wn_ref[...]).astype(BF16)

        gr = lax.dot_general(wgt_ref[...], hb, (((1,), (1,)), ((), ())), preferred_element_type=F32) + gbt_ref[...]
        row = lax.broadcasted_iota(jnp.int32, gr.shape, 0)
        is_in_gate = (row >= ATTN_HEADS) & (row < ATTN_HEADS + MLSTM_HEADS)
        gr = jnp.where(is_in_gate, gr, _log_sigmoid(gr))
        cs_chunk = _lane_prefix_sum(gr, chunk)
        cs_full = _lane_prefix_sum(gr, tm)
        carry = jnp.where((i % tiles_per_seq) == 0, 0.0, carry_scr[...])
        cs_full = cs_full + carry[:, :1]
        b_on_in_rows = pltpu.roll(cs_chunk, N_GATES - MLSTM_HEADS, axis=0)
        gates = jnp.where(row < ATTN_HEADS, cs_full * LOG2E, jnp.where(is_in_gate, gr - b_on_in_rows, cs_chunk))
        padded = jnp.concatenate([gates, jnp.zeros((GATE_LANES - gates.shape[0], tm), F32)], axis=0)
        h_scr[slot_out[0]] = hb
        carry_scr[...] = jnp.broadcast_to(cs_full[:, tm - 1:tm], carry_scr.shape)
        grow_ref[...] = gates
        gcol_ref[...] = padded.T

    def project(between):
        qw = qn_ref[...] * (LOG2E * ATTN_HEAD_DIM ** -0.5)
        heavy = [lambda: store(_T_MQ, conv_silu(tile_acc(_T_MQ), 0, MLSTM_QK_DIM ** -0.5)),
                 lambda: store(_T_MK, conv_silu(tile_acc(_T_MK), 1, 1.0))]
        heavy += [functools.partial(lambda t: store(t, head_norm(tile_acc(t), qw)), t) for t in range(_T_AQ, _T_AK)]
        heavy += [functools.partial(lambda t: store(t, head_norm(tile_acc(t), kn_ref[...])), t)
                  for t in range(_T_AK, _T_AV)]
        plain = [functools.partial(lambda t: store(t, tile_acc(t)), t)
                 for t in list(range(_T_AV, _T_MQ)) + list(range(_T_MV, _T_END))]
        for n, (hv, pn) in enumerate(zip(heavy, plain)):
            hv()
            pn()
            if n == 1:
                between()

    @pl.when(i == 0)
    def _():
        carry_scr[...] = jnp.zeros_like(carry_scr)
        conv_scr[...] = jnp.zeros_like(conv_scr)
        slot_out[0] = 0
        norm_and_gates()

    for parity in range(2):
        @pl.when((i > 0) & (i % 2 == parity))
        def _():
            slot_in[0], slot_out[0] = 1 - parity, parity
            project(norm_and_gates)


def _in_proj(x2d, wn, w_parts, wgt, gbt, qn, kn, cw, cb, *, batch, seq):
    T, D = x2d.shape
    tm, tn = ROW_TILE, PROJ_COL_TILE
    n_cols = sum(w.shape[0] for w in w_parts)
    assert [w.shape[0] // tn for w in w_parts] == [_T_MQ - _T_AQ, _T_MO - _T_MQ, _T_END - _T_MO]
    n_tiles = T // tm
    tiles_per_seq = seq // tm
    const = lambda i: (0, 0)
    const3 = lambda i: (0, 0, 0)
    tile = lambda i: jnp.minimum(i, n_tiles - 1)
    kern = functools.partial(_in_proj_kernel, n_tiles=n_tiles, tiles_per_seq=tiles_per_seq, chunk=MLSTM_CHUNK)
    return pl.pallas_call(
        kern,
        grid=(n_tiles + 1,),
        in_specs=[
            pl.BlockSpec((tm, D), lambda i: (tile(i), 0)),
            pl.BlockSpec((1, D), const),
            *[pl.BlockSpec(w.shape, const, pipeline_mode=pl.Buffered(1)) for w in w_parts],
            pl.BlockSpec((N_GATES, D), const),
            pl.BlockSpec((N_GATES, 1), const),
            pl.BlockSpec((1, ATTN_HEAD_DIM), const),
            pl.BlockSpec((1, ATTN_HEAD_DIM), const),
            pl.BlockSpec((2, CONV_WIDTH, tn), const3),
            pl.BlockSpec((2, 1, tn), const3),
        ],
        out_specs=[
            pl.BlockSpec((tm, n_cols), lambda i: (jnp.maximum(i - 1, 0), 0)),
            pl.BlockSpec((tm, GATE_LANES), lambda i: (i, 0)),
            pl.BlockSpec((None, N_GATES, tm), lambda i: (i, 0, 0)),
        ],
        out_shape=[
            jax.ShapeDtypeStruct((T, n_cols), BF16),
            jax.ShapeDtypeStruct((T + tm, GATE_LANES), F32),
            jax.ShapeDtypeStruct((n_tiles + 1, N_GATES, tm), F32),
        ],
        scratch_shapes=[
            pltpu.VMEM((2, tm, D), BF16),
            pltpu.VMEM((N_GATES, GATE_LANES), F32),
            pltpu.VMEM((2, CONV_HALO + tm, tn), F32),
        ],
        compiler_params=pltpu.CompilerParams(
            dimension_semantics=("arbitrary",), vmem_limit_bytes=VMEM_LIMIT_BYTES),
        name="in_proj",
    )(x2d, wn, *w_parts, wgt, gbt, qn, kn, cw, cb)


def _fox_kernel(q_ref, k_ref, v_ref, c_ref, ccol_ref, *rest, tk, cast_blocks):
    n_cast = len(cast_blocks)
    w32_refs, (o_ref, *w16_refs), kmax_scr = rest[:n_cast], rest[n_cast:2 * n_cast + 1], rest[-1]
    hg = pl.program_id(1)
    qi = pl.program_id(2)
    step = (pl.program_id(0) * pl.num_programs(1) + hg) * pl.num_programs(2) + qi
    for w32, w16, n_blocks in zip(w32_refs, w16_refs, cast_blocks):
        @pl.when(step < n_blocks)
        def _():
            w16[...] = w32[...].astype(w16.dtype)

    tq = q_ref.shape[0]
    d = ATTN_HEAD_DIM
    n_heads = q_ref.shape[1] // d
    n_diag = tq // tk
    qs = [q_ref[:, g * d:(g + 1) * d] for g in range(n_heads)]

    @pl.when(qi == 0)
    def _():
        for g in range(n_heads):
            kf = k_ref[:, g * d:(g + 1) * d].astype(F32)
            k2 = jnp.max(jnp.sum(kf * kf, axis=-1, keepdims=True), axis=0, keepdims=True)
            kmax_scr[g] = jnp.broadcast_to(k2, kmax_scr.shape[1:])

    lane = lax.broadcasted_iota(jnp.int32, ccol_ref.shape, 1)
    ccol = ccol_ref[...]
    mis, bounds = [], []
    for g in range(n_heads):
        qf = qs[g].astype(F32)
        bound = jnp.sqrt(jnp.sum(qf * qf, axis=-1, keepdims=True) * kmax_scr[g][:1, :1])
        ci = jnp.sum(jnp.where(lane == hg * n_heads + g, ccol, 0.0), axis=-1, keepdims=True)
        mis.append(bound - ci)
        bounds.append(jnp.max(bound))
    worst = functools.reduce(jnp.maximum, bounds)

    def load_block(j, g):
        start = j * tk if isinstance(j, int) else pl.multiple_of(j * tk, tk)
        k = k_ref[pl.ds(start, tk), g * d:(g + 1) * d]
        v = v_ref[pl.ds(start, tk), g * d:(g + 1) * d]
        per_tile = c_ref.shape[2] // tk
        cj = c_ref[j // per_tile, pl.ds(hg * n_heads + g, 1), pl.ds((j % per_tile) * tk, tk)]
        return k, v, cj

    def causal_mask(r0):
        r = lax.broadcasted_iota(jnp.int32, (tq - r0, tk), 0)
        c = lax.broadcasted_iota(jnp.int32, (tq - r0, tk), 1)
        return c <= r

    def rejoin(old, new, r0):
        return jnp.concatenate([old[:r0], new], axis=0) if r0 else new

    def finish(carry):
        for g in range(n_heads):
            l, acc = carry[g][-2:]
            l = jnp.sum(l, axis=-1, keepdims=True)
            o_ref[:, g * d:(g + 1) * d] = (acc / l).astype(o_ref.dtype)

    def scores(j, g, r0):
        k = load_block(j, g)[0]
        return lax.dot_general(qs[g][r0:], k, (((1,), (1,)), ((), ())), preferred_element_type=F32)

    def consume(j, g, s, l, acc, diag):
        r0 = 0 if diag is None else diag * tk
        _, v, cj = load_block(j, g)
        e = (s - cj) - mis[g][r0:]
        if diag is not None:
            e = jnp.where(causal_mask(r0), e, -jnp.inf)
        p = jnp.exp2(e)
        l_new = l[r0:] + functools.reduce(jnp.add, [p[:, c0:c0 + d] for c0 in range(0, tk, d)])
        acc_new = acc[r0:] + jnp.dot(p.astype(BF16), v, preferred_element_type=F32)
        return rejoin(l, l_new, r0), rejoin(acc, acc_new, r0)

    def bounded_step(j, carry, diag):
        r0 = 0 if diag is None else diag * tk
        return tuple(consume(j, g, scores(j, g, r0), *carry[g], diag) for g in range(n_heads))


    def online_step(j, carry, diag):
        r0 = 0 if diag is None else diag * tk
        out = []
        for g in range(n_heads):
            m, l, acc = carry[g]
            k, v, cj = load_block(j, g)
            s = lax.dot_general(qs[g][r0:], k, (((1,), (1,)), ((), ())), preferred_element_type=F32) - cj
            if diag is not None:
                s = jnp.where(causal_mask(r0), s, -jnp.inf)
            m_new = jnp.maximum(m[r0:], jnp.max(s, axis=-1, keepdims=True))
            alpha = jnp.exp2(m[r0:] - m_new)
            p = jnp.exp2(s - m_new)
            l_new = alpha * l[r0:] + jnp.sum(p, axis=-1, keepdims=True)
            acc_new = alpha * acc[r0:] + jnp.dot(p.astype(BF16), v, preferred_element_type=F32)
            out.append((rejoin(m, m_new, r0), rejoin(l, l_new, r0), rejoin(acc, acc_new, r0)))
        return tuple(out)

    def run(step, init):
        carry = lax.fori_loop(0, qi * n_diag, lambda j, c: step(j, c, None), init)
        for jj in range(n_diag):
            carry = step(qi * n_diag + jj, carry, jj)
        finish(carry)

    zeros = (jnp.zeros((tq, 1), F32), jnp.zeros((tq, d), F32))

    def run_unrolled(step, init):
        for c in range(k_ref.shape[0] // tq):
            @pl.when(qi == c)
            def _():
                carry = init
                for j in range(c * n_diag):
                    carry = step(j, carry, None)
                for jj in range(n_diag):
                    carry = step(c * n_diag + jj, carry, jj)
                finish(carry)

    @pl.when(worst <= FOX_MAX_BOUND)
    def _():
        run_unrolled(bounded_step,
                     tuple((jnp.zeros((tq, d), F32), jnp.zeros((tq, d), F32)) for _ in range(n_heads)))

    @pl.when(jnp.logical_not(worst <= FOX_MAX_BOUND))
    def _():
        run(online_step, tuple((jnp.full((tq, 1), -jnp.inf, F32),) + zeros for _ in range(n_heads)))


def _cast_block_rows(rows, n_steps):
    for rb in range(BF16_SUBLANES, rows + 1, BF16_SUBLANES):
        if rows % rb == 0 and rows // rb <= n_steps:
            return rb
    raise ValueError(f"cannot split {rows} rows over {n_steps} steps")


def _fox_attention(proj3, grow, gcol3, f32_weights, *, batch, seq):
    tq, tk = ATTN_Q_TILE, ATTN_KV_TILE
    assert tq % tk == 0
    dd = ATTN_HEAD_DIM * ATTN_HEADS_PER_STEP
    n_groups = ATTN_HEADS // ATTN_HEADS_PER_STEP
    nq = seq // tq
    grid = (batch, n_groups, nq)
    n_steps = batch * n_groups * nq
    cast_rows = [_cast_block_rows(w.shape[0], n_steps) for w in f32_weights]
    cast_blocks = tuple(w.shape[0] // rb for w, rb in zip(f32_weights, cast_rows))

    def cast_spec(w, rb, n_blocks):
        return pl.BlockSpec((rb, w.shape[1]), lambda b, h, qi: (jnp.minimum((b * n_groups + h) * nq + qi,
                                                                            n_blocks - 1), 0))

    cast_specs = [cast_spec(w, rb, nb) for w, rb, nb in zip(f32_weights, cast_rows, cast_blocks)]
    out = pl.pallas_call(
        functools.partial(_fox_kernel, tk=tk, cast_blocks=cast_blocks),
        grid=grid,
        in_specs=[
            pl.BlockSpec((None, tq, dd), lambda b, h, qi: (b, qi, h)),
            pl.BlockSpec((None, seq, dd), lambda b, h, qi: (b, 0, n_groups + h)),
            pl.BlockSpec((None, seq, dd), lambda b, h, qi: (b, 0, 2 * n_groups + h)),
            pl.BlockSpec((seq // ROW_TILE, N_GATES, ROW_TILE), lambda b, h, qi: (b, 0, 0)),
            pl.BlockSpec((tq, GATE_LANES), lambda b, h, qi: (b * (seq // tq) + qi, 0)),
            *cast_specs,
        ],
        out_specs=[pl.BlockSpec((None, tq, dd), lambda b, h, qi: (b, qi, h)), *cast_specs],
        out_shape=[jax.ShapeDtypeStruct((batch, seq, ATTN_WIDTH), BF16),
                   *[jax.ShapeDtypeStruct(w.shape, BF16) for w in f32_weights]],
        scratch_shapes=[pltpu.VMEM((ATTN_HEADS_PER_STEP, 8, 128), F32)],
        compiler_params=pltpu.CompilerParams(
            dimension_semantics=("arbitrary", "arbitrary", "arbitrary"), vmem_limit_bytes=VMEM_LIMIT_BYTES),
        name="fox_attention",
    )(proj3, proj3, proj3, grow, gcol3, *f32_weights)
    return out[0], out[1:]


def _mlstm_kernel(q_ref, k_ref, v_ref, og_ref, gcol_ref, grow_ref, nw_ref, o_ref, c_scr, m_scr):
    ci = pl.program_id(1)
    L = MLSTM_CHUNK
    dk, dv = MLSTM_QK_DIM, MLSTM_V_DIM

    @pl.when(ci == 0)
    def _():
        c_scr[...] = jnp.zeros_like(c_scr)
        m_scr[...] = jnp.zeros_like(m_scr)

    t_idx = lax.broadcasted_iota(jnp.int32, (L, L), 0)
    s_idx = lax.broadcasted_iota(jnp.int32, (L, L), 1)
    causal = s_idx <= t_idx
    nt = (((1,), (1,)), ((), ()))
    heads = range(MLSTM_HEADS)
    units = [(c, hh) for c in range(q_ref.shape[0] // L) for hh in heads]

    rows = {u: slice(u[0] * L, (u[0] + 1) * L) for u in units}
    q = {u: q_ref[rows[u], u[1] * dk:(u[1] + 1) * dk] for u in units}
    k = {u: k_ref[rows[u], u[1] * dk:(u[1] + 1) * dk] for u in units}
    v = {u: v_ref[rows[u], u[1] * dv:(u[1] + 1) * dv] for u in units}
    gi = {u: ATTN_HEADS + u[1] for u in units}
    gf = {u: ATTN_HEADS + MLSTM_HEADS + u[1] for u in units}
    b_row = {u: grow_ref[gf[u]:gf[u] + 1, rows[u]] for u in units}
    b_col = {u: gcol_ref[rows[u], gf[u]:gf[u] + 1] for u in units}
    a_row = {u: grow_ref[gi[u]:gi[u] + 1, rows[u]] for u in units}
    a_col = {u: gcol_ref[rows[u], gi[u]:gi[u] + 1] for u in units}
    ones = jnp.ones((L, ONES_LANES), BF16)
    v1 = {u: jnp.concatenate([v[u], ones], axis=-1) for u in units}
    qk = {u: lax.dot_general(q[u], k[u], nt, preferred_element_type=F32) for u in units}
    m_loc = {u: jnp.max(jnp.where(causal, a_row[u], -jnp.inf), axis=-1, keepdims=True) for u in units}
    s_loc = {u: qk[u] * jnp.exp(jnp.where(causal, a_row[u] - m_loc[u], -jnp.inf)) for u in units}
    pv1 = {u: jnp.dot(s_loc[u].astype(BF16), v1[u], preferred_element_type=F32) for u in units}
    pv_loc = {u: pv1[u][:, :dv] for u in units}
    d_loc = {u: pv1[u][:, dv:dv + 1] for u in units}
    a_max = {u: jnp.max(a_row[u], axis=-1, keepdims=True) for u in units}
    kw = {u: (k[u].astype(F32) * jnp.exp(a_col[u] - a_max[u])).astype(BF16) for u in units}
    upd1 = {u: lax.dot_general(kw[u], v1[u], (((0,), (0,)), ((), ())), preferred_element_type=F32)
            for u in units}

    m_prev = [m_scr[hh][:1, :1] for hh in heads]
    ct1 = [c_scr[hh] for hh in heads]
    for c in range(q_ref.shape[0] // L):
        us = [(c, hh) for hh in heads]
        qc1 = [jnp.dot(q[u], ct1[u[1]].astype(BF16), preferred_element_type=F32) for u in us]
        qc = [x[:, :dv] for x in qc1]
        qn = [x[:, dv:dv + 1] for x in qc1]
        big_m = [jnp.maximum(m_loc[u], m_prev[u[1]]) for u in us]
        alpha = [jnp.exp(m_loc[u] - big_m[hh]) for hh, u in enumerate(us)]
        beta = [jnp.exp(m_prev[hh] - big_m[hh]) for hh in heads]
        num = [alpha[hh] * pv_loc[u] + beta[hh] * qc[hh] for hh, u in enumerate(us)]
        den = [alpha[hh] * d_loc[u] + beta[hh] * qn[hh] for hh, u in enumerate(us)]
        dmax = [jnp.maximum(jnp.abs(den[hh]), jnp.exp(-(b_col[u] + big_m[hh]))) for hh, u in enumerate(us)]
        ms = [jnp.mean(num[hh] * num[hh], axis=-1, keepdims=True) for hh in heads]
        for hh, u in enumerate(us):
            hn = num[hh] * lax.rsqrt(ms[hh] + EPS * (dmax[hh] * dmax[hh]))
            og = og_ref[rows[u], hh * dv:(hh + 1) * dv].astype(F32)
            out = (hn * nw_ref[:, hh * dv:(hh + 1) * dv]) * _sigmoid(og)
            o_ref[rows[u], hh * dv:(hh + 1) * dv] = out.astype(o_ref.dtype)
        m_last = [jnp.maximum(a_max[u], m_prev[u[1]]) for u in us]
        wc = [jnp.exp(m_prev[hh] - m_last[hh]) for hh in heads]
        gamma = [jnp.exp(a_max[u] - m_last[hh]) for hh, u in enumerate(us)]
        ct1 = [wc[hh] * ct1[hh] + gamma[hh] * upd1[u] for hh, u in enumerate(us)]
        m_prev = [b_row[u][:, L - 1:L] + m_last[hh] for hh, u in enumerate(us)]

    for hh in heads:
        c_scr[hh] = ct1[hh]
        m_scr[hh] = jnp.broadcast_to(m_prev[hh], m_scr.shape[1:])


def _mlstm(proj3, gcol3, grow, nw, *, batch, seq):
    L = MLSTM_CHUNK * MLSTM_CHUNKS_PER_STEP
    assert L == ROW_TILE
    grid = (batch, seq // L)
    qk_w, v_w = MLSTM_QK_WIDTH, MLSTM_WIDTH
    base = 3 * ATTN_WIDTH
    return pl.pallas_call(
        _mlstm_kernel,
        grid=grid,
        in_specs=[
            pl.BlockSpec((None, L, qk_w), lambda b, c: (b, c, base // qk_w)),
            pl.BlockSpec((None, L, qk_w), lambda b, c: (b, c, base // qk_w + 1)),
            pl.BlockSpec((None, L, v_w), lambda b, c: (b, c, (base + 2 * qk_w) // v_w)),
            pl.BlockSpec((None, L, v_w), lambda b, c: (b, c, (base + 2 * qk_w) // v_w + 1)),
            pl.BlockSpec((L, GATE_LANES), lambda b, c: (b * (seq // L) + c, 0)),
            pl.BlockSpec((None, N_GATES, L), lambda b, c: (b * (seq // L) + c, 0, 0)),
            pl.BlockSpec((1, v_w), lambda b, c: (0, 0)),
        ],
        out_specs=pl.BlockSpec((None, L, v_w), lambda b, c: (b, c, 0)),
        out_shape=jax.ShapeDtypeStruct((batch, seq, v_w), BF16),
        scratch_shapes=[
            pltpu.VMEM((MLSTM_HEADS, MLSTM_QK_DIM, MLSTM_V_DIM + ONES_LANES), F32),
            pltpu.VMEM((MLSTM_HEADS, 8, 128), F32),
        ],
        compiler_params=pltpu.CompilerParams(
            dimension_semantics=("arbitrary", "arbitrary"), vmem_limit_bytes=VMEM_LIMIT_BYTES),
        name="mlstm",
    )(proj3, proj3, proj3, proj3, gcol3, grow, nw)


def _out_proj_kernel(x_ref, a_ref, m_ref, wa_ref, wm_ref, o_ref):
    y = jnp.dot(a_ref[...], wa_ref[...], preferred_element_type=F32)
    y = y + jnp.dot(m_ref[...], wm_ref[...], preferred_element_type=F32)
    o_ref[...] = x_ref[...] + y


def _out_proj(x2d, attn2d, mlstm2d, w_attn, w_mlstm):
    T, D = x2d.shape
    tm = ROW_TILE
    const = lambda i: (0, 0)
    return pl.pallas_call(
        _out_proj_kernel,
        grid=(T // tm,),
        in_specs=[
            pl.BlockSpec((tm, D), lambda i: (i, 0)),
            pl.BlockSpec((tm, attn2d.shape[1]), lambda i: (i, 0)),
            pl.BlockSpec((tm, mlstm2d.shape[1]), lambda i: (i, 0)),
            pl.BlockSpec(w_attn.shape, const),
            pl.BlockSpec(w_mlstm.shape, const),
        ],
        out_specs=pl.BlockSpec((tm, D), lambda i: (i, 0)),
        out_shape=jax.ShapeDtypeStruct((T, D), F32),
        compiler_params=pltpu.CompilerParams(
            dimension_semantics=("arbitrary",), vmem_limit_bytes=VMEM_LIMIT_BYTES),
        name="out_proj",
    )(x2d, attn2d, mlstm2d, w_attn, w_mlstm)


def _ffn_kernel(x_ref, wn_ref, wg_ref, wu_ref, wd_ref, o_ref, h_scr):
    f = pl.program_id(1)

    @pl.when(f == 0)
    def _():
        xf = x_ref[...]
        ms = jnp.mean(xf * xf, axis=-1, keepdims=True)
        h_scr[...] = ((xf * lax.rsqrt(ms + EPS)) * wn_ref[...]).astype(BF16)
        o_ref[...] = xf

    hb = h_scr[...]
    g = jnp.dot(hb, wg_ref[...], preferred_element_type=F32)
    u = jnp.dot(hb, wu_ref[...], preferred_element_type=F32)
    a = (g * _sigmoid(g)) * u
    o_ref[...] += jnp.dot(a.astype(BF16), wd_ref[...], preferred_element_type=F32)


def _ffn(x2d, wn, wg, wu, wd):
    T, D = x2d.shape
    F = wg.shape[1]
    tm, tf = FFN_ROW_TILE, FFN_COL_TILE
    return pl.pallas_call(
        _ffn_kernel,
        grid=(T // tm, F // tf),
        in_specs=[
            pl.BlockSpec((tm, D), lambda i, f: (i, 0)),
            pl.BlockSpec((1, D), lambda i, f: (0, 0)),
            pl.BlockSpec((D, tf), lambda i, f: (0, f)),
            pl.BlockSpec((D, tf), lambda i, f: (0, f)),
            pl.BlockSpec((tf, D), lambda i, f: (f, 0)),
        ],
        out_specs=pl.BlockSpec((tm, D), lambda i, f: (i, 0)),
        out_shape=jax.ShapeDtypeStruct((T, D), F32),
        scratch_shapes=[pltpu.VMEM((tm, D), BF16)],
        compiler_params=pltpu.CompilerParams(
            dimension_semantics=("arbitrary", "arbitrary"), vmem_limit_bytes=VMEM_LIMIT_BYTES),
        name="ffn",
    )(x2d, wn, wg, wu, wd)


def _ple_kernel(x_ref, p_ref, wn_ref, wgate_ref, wproj_ref, wpost_ref, o_ref):
    xf = x_ref[...]
    ms = jnp.mean(xf * xf, axis=-1, keepdims=True)
    hb = ((xf * lax.rsqrt(ms + EPS)) * wn_ref[...]).astype(BF16)
    e = jnp.dot(p_ref[...].astype(BF16), wproj_ref[...], preferred_element_type=F32)
    ems = jnp.mean(e * e, axis=-1, keepdims=True)
    e = (e * lax.rsqrt(ems + EPS)) * wpost_ref[...]
    tn = PROJ_COL_TILE
    for c0 in range(0, o_ref.shape[1], tn):
        gate = _sigmoid(jnp.dot(hb, wgate_ref[:, c0:c0 + tn], preferred_element_type=F32))
        o_ref[:, c0:c0 + tn] = xf[:, c0:c0 + tn] + gate * e[:, c0:c0 + tn]


def _ple(x2d, p2d, wn, wgate, wproj, wpost):
    T, D = x2d.shape
    P = p2d.shape[1]
    tm = ROW_TILE
    const = lambda i: (0, 0)
    return pl.pallas_call(
        _ple_kernel,
        grid=(T // tm,),
        in_specs=[
            pl.BlockSpec((tm, D), lambda i: (i, 0)),
            pl.BlockSpec((tm, P), lambda i: (i, 0)),
            pl.BlockSpec((1, D), const),
            pl.BlockSpec((D, D), const),
            pl.BlockSpec((P, D), const),
            pl.BlockSpec((1, D), const),
        ],
        out_specs=pl.BlockSpec((tm, D), lambda i: (i, 0)),
        out_shape=jax.ShapeDtypeStruct((T, D), F32),
        compiler_params=pltpu.CompilerParams(
            dimension_semantics=("arbitrary",), vmem_limit_bytes=VMEM_LIMIT_BYTES),
        name="ple",
    )(x2d, p2d, wn, wgate, wproj, wpost)


def _layer(x2d, p2d, batch, seq, w_norm_mix, w_in, fox_f_bias, q_norm_w, k_norm_w, mlstm_conv_w, mlstm_conv_b,
           mlstm_i_bias, mlstm_f_bias, mlstm_out_norm_w, w_out, w_norm_ffn, w_ffn_gate, w_ffn_up, w_ffn_down,
           w_norm_ple, w_ple_gate, w_ple_proj, w_ple_post_norm):
    D = x2d.shape[1]
    A, QK, MV = ATTN_WIDTH, MLSTM_QK_WIDTH, MLSTM_WIDTH
    o_af = 3 * A
    o_m = o_af + ATTN_HEADS
    o_mi = o_m + 2 * QK + MV
    o_mo = o_mi + 2 * MLSTM_HEADS
    w_t = w_in.T.astype(BF16)
    w_parts = (w_t[:o_af], w_t[o_m:o_mi], w_t[o_mo:])
    wgt = jnp.concatenate([w_t[o_af:o_m], w_t[o_mi:o_mo]], axis=0)
    gbt = jnp.concatenate([fox_f_bias, mlstm_i_bias, mlstm_f_bias]).astype(F32)[:, None]
    cw = mlstm_conv_w.reshape(CONV_WIDTH, 2, QK).transpose(1, 0, 2)
    cb = mlstm_conv_b.reshape(2, 1, QK)

    proj, gcol, grow = _in_proj(x2d, w_norm_mix[None, :], w_parts, wgt, gbt, q_norm_w[None, :],
                                k_norm_w[None, :], cw, cb, batch=batch, seq=seq)
    proj3 = proj.reshape(batch, seq, proj.shape[1])
    attn, (wg16, wu16, wd16, wpg16) = _fox_attention(
        proj3, grow, gcol, (w_ffn_gate, w_ffn_up, w_ffn_down, w_ple_gate), batch=batch, seq=seq)
    mlstm = _mlstm(proj3, gcol, grow, mlstm_out_norm_w[None, :], batch=batch, seq=seq)

    x1 = _out_proj(x2d, attn.reshape(batch * seq, A), mlstm.reshape(batch * seq, MV),
                   w_out[:A].astype(BF16), w_out[A:].astype(BF16))
    x2 = _ffn(x1, w_norm_ffn[None, :], wg16, wu16, wd16)
    x3 = _ple(x2, p2d, w_norm_ple[None, :], wpg16, w_ple_proj.astype(BF16), w_ple_post_norm[None, :])
    return x3


def kernel(x, p, w_norm_mix, w_in, fox_f_bias, q_norm_w, k_norm_w, mlstm_conv_w, mlstm_conv_b, mlstm_i_bias,
           mlstm_f_bias, mlstm_out_norm_w, w_out, w_norm_ffn, w_ffn_gate, w_ffn_up, w_ffn_down, w_norm_ple,
           w_ple_gate, w_ple_proj, w_ple_post_norm):
    B, S, D = x.shape
    depth = w_in.shape[0]
    x2d = x.reshape(B * S, D)
    for i in range(depth):
        x2d = _layer(x2d, p[i].reshape(B * S, p.shape[-1]), B, S, w_norm_mix[i], w_in[i], fox_f_bias[i],
                     q_norm_w[i], k_norm_w[i], mlstm_conv_w[i], mlstm_conv_b[i], mlstm_i_bias[i], mlstm_f_bias[i],
                     mlstm_out_norm_w[i], w_out[i], w_norm_ffn[i], w_ffn_gate[i], w_ffn_up[i], w_ffn_down[i],
                     w_norm_ple[i], w_ple_gate[i], w_ple_proj[i], w_ple_post_norm[i])
    return x2d.reshape(B, S, D)
```

```python
import functools
import math

import jax
import jax.numpy as jnp
from jax import lax
from jax.experimental import pallas as pl
from jax.experimental.pallas import tpu as pltpu

F32 = jnp.float32
BF16 = jnp.bfloat16
EPS = 1e-6

ATTN_HEADS = 8
ATTN_HEAD_DIM = 128
MLSTM_HEADS = 4
MLSTM_QK_DIM = 128
MLSTM_V_DIM = 256
CONV_WIDTH = 4
ATTN_WIDTH = ATTN_HEADS * ATTN_HEAD_DIM
MLSTM_QK_WIDTH = MLSTM_HEADS * MLSTM_QK_DIM
MLSTM_WIDTH = MLSTM_HEADS * MLSTM_V_DIM
N_GATES = ATTN_HEADS + 2 * MLSTM_HEADS
GATE_LANES = 128
CONV_HALO = 8
BF16_SUBLANES = 16

VMEM_LIMIT_BYTES = 56 * 1024 * 1024

ROW_TILE = 512
PROJ_COL_TILE = 512
MLSTM_CHUNK = 256
MLSTM_CHUNKS_PER_STEP = 2
ONES_LANES = 128
ATTN_Q_TILE = 1024
ATTN_KV_TILE = 512
ATTN_HEADS_PER_STEP = 2
FFN_COL_TILE = 512
FFN_ROW_TILE = 1024
LOG2E = math.log2(math.e)
FOX_MAX_BOUND = 40.0

_T_AQ, _T_AK, _T_AV, _T_MQ, _T_MK, _T_MV, _T_MO, _T_END = 0, 2, 4, 6, 7, 8, 10, 12


def _log_sigmoid(z):
    return jnp.minimum(z, 0.0) - jnp.log1p(jnp.exp(-jnp.abs(z)))


def _sigmoid(z):
    return 1.0 / (1.0 + jnp.exp(-z))


def _lane_prefix_sum(v, period):
    axis = v.ndim - 1
    pos = lax.broadcasted_iota(jnp.int32, v.shape, axis) & (period - 1)
    shift = 1
    while shift < period:
        v = v + jnp.where(pos >= shift, pltpu.roll(v, shift, axis=axis), 0.0)
        shift *= 2
    return v


def _in_proj_kernel(x_ref, wn_ref, wa_ref, wm_ref, wo_ref, wgt_ref, gbt_ref, qn_ref, kn_ref, cw_ref, cb_ref,
                    proj_ref, gcol_ref, grow_ref, h_scr, carry_scr, conv_scr, *, n_tiles, tiles_per_seq, chunk):
    i = pl.program_id(0)
    tm = x_ref.shape[0]
    tn = PROJ_COL_TILE
    seq_start = ((i - 1) % tiles_per_seq) == 0
    slot_in, slot_out = [0], [0]

    def tile_acc(t):
        w_ref, t0 = (wa_ref, _T_AQ) if t < _T_MQ else (wm_ref, _T_MQ) if t < _T_MO else (wo_ref, _T_MO)
        return lax.dot_general(h_scr[slot_in[0]], w_ref[(t - t0) * tn:(t - t0 + 1) * tn, :],
                               (((1,), (1,)), ((), ())), preferred_element_type=F32)

    def store(t, val):
        proj_ref[:, t * tn:(t + 1) * tn] = val.astype(BF16)

    def head_norm(acc, w):
        outs = []
        for hh in range(tn // ATTN_HEAD_DIM):
            a = acc[:, hh * ATTN_HEAD_DIM:(hh + 1) * ATTN_HEAD_DIM]
            ms = jnp.mean(a * a, axis=-1, keepdims=True)
            outs.append((a * lax.rsqrt(ms + EPS)) * w)
        return jnp.concatenate(outs, axis=-1)

    def conv_silu(acc, which, scale):
        buf = conv_scr.at[which]
        buf[0:CONV_HALO, :] = jnp.where(seq_start, 0.0, buf[tm:tm + CONV_HALO, :])
        buf[CONV_HALO:CONV_HALO + tm, :] = acc
        cw = cw_ref[which]
        y = cb_ref[which] + cw[CONV_WIDTH - 1:CONV_WIDTH, :] * acc
        for tap in range(CONV_WIDTH - 1):
            off = CONV_HALO - (CONV_WIDTH - 1) + tap
            y = y + cw[tap:tap + 1, :] * buf[off:off + tm, :]
        y = y * _sigmoid(y)
        return y * scale if scale != 1.0 else y

    def norm_and_gates():
        xf = x_ref[...]
        ms = jnp.mean(xf * xf, axis=-1, keepdims=True)
        hb = ((xf * lax.rsqrt(ms + EPS)) * wn_ref[...]).astype(BF16)

        gr = lax.dot_general(wgt_ref[...], hb, (((1,), (1,)), ((), ())), preferred_element_type=F32) + gbt_ref[...]
        row = lax.broadcasted_iota(jnp.int32, gr.shape, 0)
        is_in_gate = (row >= ATTN_HEADS) & (row < ATTN_HEADS + MLSTM_HEADS)
        gr = jnp.where(is_in_gate, gr, _log_sigmoid(gr))
        cs_chunk = _lane_prefix_sum(gr, chunk)
        cs_full = _lane_prefix_sum(gr, tm)
        carry = jnp.where((i % tiles_per_seq) == 0, 0.0, carry_scr[...])
        cs_full = cs_full + carry[:, :1]
        b_on_in_rows = pltpu.roll(cs_chunk, N_GATES - MLSTM_HEADS, axis=0)
        gates = jnp.where(row < ATTN_HEADS, cs_full * LOG2E, jnp.where(is_in_gate, gr - b_on_in_rows, cs_chunk))
        padded = jnp.concatenate([gates, jnp.zeros((GATE_LANES - gates.shape[0], tm), F32)], axis=0)
        h_scr[slot_out[0]] = hb
        carry_scr[...] = jnp.broadcast_to(cs_full[:, tm - 1:tm], carry_scr.shape)
        grow_ref[...] = gates
        gcol_ref[...] = padded.T

    def project(between):
        qw = qn_ref[...] * (LOG2E * ATTN_HEAD_DIM ** -0.5)
        heavy = [lambda: store(_T_MQ, conv_silu(tile_acc(_T_MQ), 0, MLSTM_QK_DIM ** -0.5)),
                 lambda: store(_T_MK, conv_silu(tile_acc(_T_MK), 1, 1.0))]
        heavy += [functools.partial(lambda t: store(t, head_norm(tile_acc(t), qw)), t) for t in range(_T_AQ, _T_AK)]
        heavy += [functools.partial(lambda t: store(t, head_norm(tile_acc(t), kn_ref[...])), t)
                  for t in range(_T_AK, _T_AV)]
        plain = [functools.partial(lambda t: store(t, tile_acc(t)), t)
                 for t in list(range(_T_AV, _T_MQ)) + list(range(_T_MV, _T_END))]
        for n, (hv, pn) in enumerate(zip(heavy, plain)):
            hv()
            pn()
            if n == 1:
                between()

    @pl.when(i == 0)
    def _():
        carry_scr[...] = jnp.zeros_like(carry_scr)
        conv_scr[...] = jnp.zeros_like(conv_scr)
        slot_out[0] = 0
        norm_and_gates()

    for parity in range(2):
        @pl.when((i > 0) & (i % 2 == parity))
        def _():
            slot_in[0], slot_out[0] = 1 - parity, parity
            project(norm_and_gates)


def _in_proj(x2d, wn, w_parts, wgt, gbt, qn, kn, cw, cb, *, batch, seq):
    T, D = x2d.shape
    tm, tn = ROW_TILE, PROJ_COL_TILE
    n_cols = sum(w.shape[0] for w in w_parts)
    assert [w.shape[0] // tn for w in w_parts] == [_T_MQ - _T_AQ, _T_MO - _T_MQ, _T_END - _T_MO]
    n_tiles = T // tm
    tiles_per_seq = seq // tm
    const = lambda i: (0, 0)
    const3 = lambda i: (0, 0, 0)
    tile = lambda i: jnp.minimum(i, n_tiles - 1)
    kern = functools.partial(_in_proj_kernel, n_tiles=n_tiles, tiles_per_seq=tiles_per_seq, chunk=MLSTM_CHUNK)
    return pl.pallas_call(
        kern,
        grid=(n_tiles + 1,),
        in_specs=[
            pl.BlockSpec((tm, D), lambda i: (tile(i), 0)),
            pl.BlockSpec((1, D), const),
            *[pl.BlockSpec(w.shape, const, pipeline_mode=pl.Buffered(1)) for w in w_parts],
            pl.BlockSpec((N_GATES, D), const),
            pl.BlockSpec((N_GATES, 1), const),
            pl.BlockSpec((1, ATTN_HEAD_DIM), const),
            pl.BlockSpec((1, ATTN_HEAD_DIM), const),
            pl.BlockSpec((2, CONV_WIDTH, tn), const3),
            pl.BlockSpec((2, 1, tn), const3),
        ],
        out_specs=[
            pl.BlockSpec((tm, n_cols), lambda i: (jnp.maximum(i - 1, 0), 0)),
            pl.BlockSpec((tm, GATE_LANES), lambda i: (i, 0)),
            pl.BlockSpec((None, N_GATES, tm), lambda i: (i, 0, 0)),
        ],
        out_shape=[
            jax.ShapeDtypeStruct((T, n_cols), BF16),
            jax.ShapeDtypeStruct((T + tm, GATE_LANES), F32),
            jax.ShapeDtypeStruct((n_tiles + 1, N_GATES, tm), F32),
        ],
        scratch_shapes=[
            pltpu.VMEM((2, tm, D), BF16),
            pltpu.VMEM((N_GATES, GATE_LANES), F32),
            pltpu.VMEM((2, CONV_HALO + tm, tn), F32),
        ],
        compiler_params=pltpu.CompilerParams(
            dimension_semantics=("arbitrary",), vmem_limit_bytes=VMEM_LIMIT_BYTES),
        name="in_proj",
    )(x2d, wn, *w_parts, wgt, gbt, qn, kn, cw, cb)


def _fox_kernel(q_ref, k_ref, v_ref, c_ref, ccol_ref, *rest, tk, cast_blocks, n_steps):
    n_cast = len(cast_blocks)
    w32_refs, (o_ref, *w16_refs), kmax_scr = rest[:n_cast], rest[n_cast:2 * n_cast + 1], rest[-1]
    hg = pl.program_id(1)
    qi = pl.program_id(2)
    step = (pl.program_id(0) * pl.num_programs(1) + hg) * pl.num_programs(2) + qi
    for w32, w16, n_blocks in zip(w32_refs, w16_refs, cast_blocks):
        if n_blocks == n_steps:
            w16[...] = w32[...].astype(w16.dtype)
        else:
            @pl.when(step < n_blocks)
            def _():
                w16[...] = w32[...].astype(w16.dtype)

    tq = q_ref.shape[0]
    d = ATTN_HEAD_DIM
    n_heads = q_ref.shape[1] // d
    n_diag = tq // tk
    qs = [q_ref[:, g * d:(g + 1) * d] for g in range(n_heads)]

    @pl.when(qi == 0)
    def _():
        for g in range(n_heads):
            kf = k_ref[:, g * d:(g + 1) * d].astype(F32)
            k2 = jnp.max(jnp.sum(kf * kf, axis=-1, keepdims=True), axis=0, keepdims=True)
            kmax_scr[g] = jnp.broadcast_to(k2, kmax_scr.shape[1:])

    lane = lax.broadcasted_iota(jnp.int32, ccol_ref.shape, 1)
    ccol = ccol_ref[...]
    mis, bounds = [], []
    for g in range(n_heads):
        qf = qs[g].astype(F32)
        bound = jnp.sqrt(jnp.sum(qf * qf, axis=-1, keepdims=True) * kmax_scr[g][:1, :1])
        ci = jnp.sum(jnp.where(lane == hg * n_heads + g, ccol, 0.0), axis=-1, keepdims=True)
        mis.append(bound - ci)
        bounds.append(jnp.max(bound))
    worst = functools.reduce(jnp.maximum, bounds)

    def load_block(j, g):
        start = j * tk if isinstance(j, int) else pl.multiple_of(j * tk, tk)
        k = k_ref[pl.ds(start, tk), g * d:(g + 1) * d]
        v = v_ref[pl.ds(start, tk), g * d:(g + 1) * d]
        per_tile = c_ref.shape[2] // tk
        cj = c_ref[j // per_tile, pl.ds(hg * n_heads + g, 1), pl.ds((j % per_tile) * tk, tk)]
        return k, v, cj

    def causal_mask(r0):
        r = lax.broadcasted_iota(jnp.int32, (tq - r0, tk), 0)
        c = lax.broadcasted_iota(jnp.int32, (tq - r0, tk), 1)
        return c <= r

    def rejoin(old, new, r0):
        return jnp.concatenate([old[:r0], new], axis=0) if r0 else new

    def finish(carry):
        for g in range(n_heads):
            l, acc = carry[g][-2:]
            l = jnp.sum(l, axis=-1, keepdims=True)
            o_ref[:, g * d:(g + 1) * d] = (acc / l).astype(o_ref.dtype)

    def scores(j, g, r0):
        k = load_block(j, g)[0]
        return lax.dot_general(qs[g][r0:], k, (((1,), (1,)), ((), ())), preferred_element_type=F32)

    def consume(j, g, s, l, acc, diag):
        r0 = 0 if diag is None else diag * tk
        _, v, cj = load_block(j, g)
        e = (s - cj) - mis[g][r0:]
        if diag is not None:
            e = jnp.where(causal_mask(r0), e, -jnp.inf)
        p = jnp.exp2(e)
        l_new = l[r0:] + functools.reduce(jnp.add, [p[:, c0:c0 + d] for c0 in range(0, tk, d)])
        acc_new = acc[r0:] + jnp.dot(p.astype(BF16), v, preferred_element_type=F32)
        return rejoin(l, l_new, r0), rejoin(acc, acc_new, r0)

    def bounded_step(j, carry, diag):
        r0 = 0 if diag is None else diag * tk
        return tuple(consume(j, g, scores(j, g, r0), *carry[g], diag) for g in range(n_heads))


    def online_step(j, carry, diag):
        r0 = 0 if diag is None else diag * tk
        out = []
        for g in range(n_heads):
            m, l, acc = carry[g]
            k, v, cj = load_block(j, g)
            s = lax.dot_general(qs[g][r0:], k, (((1,), (1,)), ((), ())), preferred_element_type=F32) - cj
            if diag is not None:
                s = jnp.where(causal_mask(r0), s, -jnp.inf)
            m_new = jnp.maximum(m[r0:], jnp.max(s, axis=-1, keepdims=True))
            alpha = jnp.exp2(m[r0:] - m_new)
            p = jnp.exp2(s - m_new)
            l_new = alpha * l[r0:] + jnp.sum(p, axis=-1, keepdims=True)
            acc_new = alpha * acc[r0:] + jnp.dot(p.astype(BF16), v, preferred_element_type=F32)
            out.append((rejoin(m, m_new, r0), rejoin(l, l_new, r0), rejoin(acc, acc_new, r0)))
        return tuple(out)

    def run(step, init):
        carry = lax.fori_loop(0, qi * n_diag, lambda j, c: step(j, c, None), init)
        for jj in range(n_diag):
            carry = step(qi * n_diag + jj, carry, jj)
        finish(carry)

    zeros = (jnp.zeros((tq, 1), F32), jnp.zeros((tq, d), F32))

    def run_unrolled(step, init):
        for c in range(k_ref.shape[0] // tq):
            @pl.when(qi == c)
            def _():
                carry = init
                for j in range(c * n_diag):
                    carry = step(j, carry, None)
                for jj in range(n_diag):
                    carry = step(c * n_diag + jj, carry, jj)
                finish(carry)

    @pl.when(worst <= FOX_MAX_BOUND)
    def _():
        run_unrolled(bounded_step,
                     tuple((jnp.zeros((tq, d), F32), jnp.zeros((tq, d), F32)) for _ in range(n_heads)))

    @pl.when(jnp.logical_not(worst <= FOX_MAX_BOUND))
    def _():
        run(online_step, tuple((jnp.full((tq, 1), -jnp.inf, F32),) + zeros for _ in range(n_heads)))


def _cast_block_rows(rows, n_steps):
    for rb in range(BF16_SUBLANES, rows + 1, BF16_SUBLANES):
        if rows % rb == 0 and rows // rb <= n_steps:
            return rb
    raise ValueError(f"cannot split {rows} rows over {n_steps} steps")


def _fox_attention(proj3, grow, gcol3, f32_weights, *, batch, seq):
    tq, tk = ATTN_Q_TILE, ATTN_KV_TILE
    assert tq % tk == 0
    dd = ATTN_HEAD_DIM * ATTN_HEADS_PER_STEP
    n_groups = ATTN_HEADS // ATTN_HEADS_PER_STEP
    nq = seq // tq
    grid = (batch, n_groups, nq)
    n_steps = batch * n_groups * nq
    cast_rows = [_cast_block_rows(w.shape[0], n_steps) for w in f32_weights]
    cast_blocks = tuple(w.shape[0] // rb for w, rb in zip(f32_weights, cast_rows))

    def cast_spec(w, rb, n_blocks):
        return pl.BlockSpec((rb, w.shape[1]), lambda b, h, qi: (jnp.minimum((b * n_groups + h) * nq + qi,
                                                                            n_blocks - 1), 0))

    cast_specs = [cast_spec(w, rb, nb) for w, rb, nb in zip(f32_weights, cast_rows, cast_blocks)]
    out = pl.pallas_call(
        functools.partial(_fox_kernel, tk=tk, cast_blocks=cast_blocks, n_steps=n_steps),
        grid=grid,
        in_specs=[
            pl.BlockSpec((None, tq, dd), lambda b, h, qi: (b, qi, h)),
            pl.BlockSpec((None, seq, dd), lambda b, h, qi: (b, 0, n_groups + h)),
            pl.BlockSpec((None, seq, dd), lambda b, h, qi: (b, 0, 2 * n_groups + h)),
            pl.BlockSpec((seq // ROW_TILE, N_GATES, ROW_TILE), lambda b, h, qi: (b, 0, 0)),
            pl.BlockSpec((tq, GATE_LANES), lambda b, h, qi: (b * (seq // tq) + qi, 0)),
            *cast_specs,
        ],
        out_specs=[pl.BlockSpec((None, tq, dd), lambda b, h, qi: (b, qi, h)), *cast_specs],
        out_shape=[jax.ShapeDtypeStruct((batch, seq, ATTN_WIDTH), BF16),
                   *[jax.ShapeDtypeStruct(w.shape, BF16) for w in f32_weights]],
        scratch_shapes=[pltpu.VMEM((ATTN_HEADS_PER_STEP, 8, 128), F32)],
        compiler_params=pltpu.CompilerParams(
            dimension_semantics=("arbitrary", "arbitrary", "arbitrary"), vmem_limit_bytes=VMEM_LIMIT_BYTES),
        name="fox_attention",
    )(proj3, proj3, proj3, grow, gcol3, *f32_weights)
    return out[0], out[1:]


def _mlstm_kernel(q_ref, k_ref, v_ref, og_ref, gcol_ref, grow_ref, nw_ref, o_ref, c_scr, m_scr):
    ci = pl.program_id(1)
    L = MLSTM_CHUNK
    dk, dv = MLSTM_QK_DIM, MLSTM_V_DIM

    @pl.when(ci == 0)
    def _():
        c_scr[...] = jnp.zeros_like(c_scr)
        m_scr[...] = jnp.zeros_like(m_scr)

    t_idx = lax.broadcasted_iota(jnp.int32, (L, L), 0)
    s_idx = lax.broadcasted_iota(jnp.int32, (L, L), 1)
    causal = s_idx <= t_idx
    nt = (((1,), (1,)), ((), ()))
    heads = range(MLSTM_HEADS)
    units = [(c, hh) for c in range(q_ref.shape[0] // L) for hh in heads]

    rows = {u: slice(u[0] * L, (u[0] + 1) * L) for u in units}
    q = {u: q_ref[rows[u], u[1] * dk:(u[1] + 1) * dk] for u in units}
    k = {u: k_ref[rows[u], u[1] * dk:(u[1] + 1) * dk] for u in units}
    v = {u: v_ref[rows[u], u[1] * dv:(u[1] + 1) * dv] for u in units}
    gi = {u: ATTN_HEADS + u[1] for u in units}
    gf = {u: ATTN_HEADS + MLSTM_HEADS + u[1] for u in units}
    b_row = {u: grow_ref[gf[u]:gf[u] + 1, rows[u]] for u in units}
    b_col = {u: gcol_ref[rows[u], gf[u]:gf[u] + 1] for u in units}
    a_row = {u: grow_ref[gi[u]:gi[u] + 1, rows[u]] for u in units}
    a_col = {u: gcol_ref[rows[u], gi[u]:gi[u] + 1] for u in units}
    ones = jnp.ones((L, ONES_LANES), BF16)
    v1 = {u: jnp.concatenate([v[u], ones], axis=-1) for u in units}
    qk = {u: lax.dot_general(q[u], k[u], nt, preferred_element_type=F32) for u in units}
    m_loc = {u: jnp.max(jnp.where(causal, a_row[u], -jnp.inf), axis=-1, keepdims=True) for u in units}
    s_loc = {u: qk[u] * jnp.exp(jnp.where(causal, a_row[u] - m_loc[u], -jnp.inf)) for u in units}
    pv1 = {u: jnp.dot(s_loc[u].astype(BF16), v1[u], preferred_element_type=F32) for u in units}
    pv_loc = {u: pv1[u][:, :dv] for u in units}
    d_loc = {u: pv1[u][:, dv:dv + 1] for u in units}
    a_max = {u: jnp.max(a_row[u], axis=-1, keepdims=True) for u in units}
    kw = {u: (k[u].astype(F32) * jnp.exp(a_col[u] - a_max[u])).astype(BF16) for u in units}
    upd1 = {u: lax.dot_general(kw[u], v1[u], (((0,), (0,)), ((), ())), preferred_element_type=F32)
            for u in units}

    m_prev = [m_scr[hh][:1, :1] for hh in heads]
    ct1 = [c_scr[hh] for hh in heads]
    for c in range(q_ref.shape[0] // L):
        us = [(c, hh) for hh in heads]
        qc1 = [jnp.dot(q[u], ct1[u[1]].astype(BF16), preferred_element_type=F32) for u in us]
        qc = [x[:, :dv] for x in qc1]
        qn = [x[:, dv:dv + 1] for x in qc1]
        big_m = [jnp.maximum(m_loc[u], m_prev[u[1]]) for u in us]
        alpha = [jnp.exp(m_loc[u] - big_m[hh]) for hh, u in enumerate(us)]
        beta = [jnp.exp(m_prev[hh] - big_m[hh]) for hh in heads]
        num = [alpha[hh] * pv_loc[u] + beta[hh] * qc[hh] for hh, u in enumerate(us)]
        den = [alpha[hh] * d_loc[u] + beta[hh] * qn[hh] for hh, u in enumerate(us)]
        dmax = [jnp.maximum(jnp.abs(den[hh]), jnp.exp(-(b_col[u] + big_m[hh]))) for hh, u in enumerate(us)]
        ms = [jnp.mean(num[hh] * num[hh], axis=-1, keepdims=True) for hh in heads]
        for hh, u in enumerate(us):
            hn = num[hh] * lax.rsqrt(ms[hh] + EPS * (dmax[hh] * dmax[hh]))
            og = og_ref[rows[u], hh * dv:(hh + 1) * dv].astype(F32)
            out = (hn * nw_ref[:, hh * dv:(hh + 1) * dv]) * _sigmoid(og)
            o_ref[rows[u], hh * dv:(hh + 1) * dv] = out.astype(o_ref.dtype)
        m_last = [jnp.maximum(a_max[u], m_prev[u[1]]) for u in us]
        wc = [jnp.exp(m_prev[hh] - m_last[hh]) for hh in heads]
        gamma = [jnp.exp(a_max[u] - m_last[hh]) for hh, u in enumerate(us)]
        ct1 = [wc[hh] * ct1[hh] + gamma[hh] * upd1[u] for hh, u in enumerate(us)]
        m_prev = [b_row[u][:, L - 1:L] + m_last[hh] for hh, u in enumerate(us)]

    for hh in heads:
        c_scr[hh] = ct1[hh]
        m_scr[hh] = jnp.broadcast_to(m_prev[hh], m_scr.shape[1:])


def _mlstm(proj3, gcol3, grow, nw, *, batch, seq):
    L = MLSTM_CHUNK * MLSTM_CHUNKS_PER_STEP
    assert L == ROW_TILE
    grid = (batch, seq // L)
    qk_w, v_w = MLSTM_QK_WIDTH, MLSTM_WIDTH
    base = 3 * ATTN_WIDTH
    return pl.pallas_call(
        _mlstm_kernel,
        grid=grid,
        in_specs=[
            pl.BlockSpec((None, L, qk_w), lambda b, c: (b, c, base // qk_w)),
            pl.BlockSpec((None, L, qk_w), lambda b, c: (b, c, base // qk_w + 1)),
            pl.BlockSpec((None, L, v_w), lambda b, c: (b, c, (base + 2 * qk_w) // v_w)),
            pl.BlockSpec((None, L, v_w), lambda b, c: (b, c, (base + 2 * qk_w) // v_w + 1)),
            pl.BlockSpec((L, GATE_LANES), lambda b, c: (b * (seq // L) + c, 0)),
            pl.BlockSpec((None, N_GATES, L), lambda b, c: (b * (seq // L) + c, 0, 0)),
            pl.BlockSpec((1, v_w), lambda b, c: (0, 0)),
        ],
        out_specs=pl.BlockSpec((None, L, v_w), lambda b, c: (b, c, 0)),
        out_shape=jax.ShapeDtypeStruct((batch, seq, v_w), BF16),
        scratch_shapes=[
            pltpu.VMEM((MLSTM_HEADS, MLSTM_QK_DIM, MLSTM_V_DIM + ONES_LANES), F32),
            pltpu.VMEM((MLSTM_HEADS, 8, 128), F32),
        ],
        compiler_params=pltpu.CompilerParams(
            dimension_semantics=("arbitrary", "arbitrary"), vmem_limit_bytes=VMEM_LIMIT_BYTES),
        name="mlstm",
    )(proj3, proj3, proj3, proj3, gcol3, grow, nw)


def _out_proj_kernel(x_ref, a_ref, m_ref, wa_ref, wm_ref, o_ref):
    y = jnp.dot(a_ref[...], wa_ref[...], preferred_element_type=F32)
    y = y + jnp.dot(m_ref[...], wm_ref[...], preferred_element_type=F32)
    o_ref[...] = x_ref[...] + y


def _out_proj(x2d, attn2d, mlstm2d, w_out):
    T, D = x2d.shape
    tm = ROW_TILE
    half = attn2d.shape[1]
    assert mlstm2d.shape[1] == half and w_out.shape == (2 * half, D)
    return pl.pallas_call(
        _out_proj_kernel,
        grid=(T // tm,),
        in_specs=[
            pl.BlockSpec((tm, D), lambda i: (i, 0)),
            pl.BlockSpec((tm, attn2d.shape[1]), lambda i: (i, 0)),
            pl.BlockSpec((tm, mlstm2d.shape[1]), lambda i: (i, 0)),
            pl.BlockSpec((half, D), lambda i: (0, 0)),
            pl.BlockSpec((half, D), lambda i: (1, 0)),
        ],
        out_specs=pl.BlockSpec((tm, D), lambda i: (i, 0)),
        out_shape=jax.ShapeDtypeStruct((T, D), F32),
        compiler_params=pltpu.CompilerParams(
            dimension_semantics=("arbitrary",), vmem_limit_bytes=VMEM_LIMIT_BYTES),
        name="out_proj",
    )(x2d, attn2d, mlstm2d, w_out, w_out)


def _ffn_kernel(x_ref, wn_ref, wg_ref, wu_ref, wd_ref, o_ref, h_scr):
    f = pl.program_id(1)

    @pl.when(f == 0)
    def _():
        xf = x_ref[...]
        ms = jnp.mean(xf * xf, axis=-1, keepdims=True)
        h_scr[...] = ((xf * lax.rsqrt(ms + EPS)) * wn_ref[...]).astype(BF16)
        o_ref[...] = xf

    hb = h_scr[...]
    g = jnp.dot(hb, wg_ref[...], preferred_element_type=F32)
    u = jnp.dot(hb, wu_ref[...], preferred_element_type=F32)
    a = (g * _sigmoid(g)) * u
    o_ref[...] += jnp.dot(a.astype(BF16), wd_ref[...], preferred_element_type=F32)


def _ffn(x2d, wn, wg, wu, wd):
    T, D = x2d.shape
    F = wg.shape[1]
    tm, tf = FFN_ROW_TILE, FFN_COL_TILE
    return pl.pallas_call(
        _ffn_kernel,
        grid=(T // tm, F // tf),
        in_specs=[
            pl.BlockSpec((tm, D), lambda i, f: (i, 0)),
            pl.BlockSpec((1, D), lambda i, f: (0, 0)),
            pl.BlockSpec((D, tf), lambda i, f: (0, f)),
            pl.BlockSpec((D, tf), lambda i, f: (0, f)),
            pl.BlockSpec((tf, D), lambda i, f: (f, 0)),
        ],
        out_specs=pl.BlockSpec((tm, D), lambda i, f: (i, 0)),
        out_shape=jax.ShapeDtypeStruct((T, D), F32),
        scratch_shapes=[pltpu.VMEM((tm, D), BF16)],
        compiler_params=pltpu.CompilerParams(
            dimension_semantics=("arbitrary", "arbitrary"), vmem_limit_bytes=VMEM_LIMIT_BYTES),
        name="ffn",
    )(x2d, wn, wg, wu, wd)


def _ple_kernel(x_ref, p_ref, wn_ref, wgate_ref, wproj_ref, wpost_ref, o_ref):
    xf = x_ref[...]
    ms = jnp.mean(xf * xf, axis=-1, keepdims=True)
    hb = ((xf * lax.rsqrt(ms + EPS)) * wn_ref[...]).astype(BF16)
    e = jnp.dot(p_ref[...].astype(BF16), wproj_ref[...], preferred_element_type=F32)
    ems = jnp.mean(e * e, axis=-1, keepdims=True)
    e = (e * lax.rsqrt(ems + EPS)) * wpost_ref[...]
    tn = PROJ_COL_TILE
    for c0 in range(0, o_ref.shape[1], tn):
        gate = _sigmoid(jnp.dot(hb, wgate_ref[:, c0:c0 + tn], preferred_element_type=F32))
        o_ref[:, c0:c0 + tn] = xf[:, c0:c0 + tn] + gate * e[:, c0:c0 + tn]


def _ple(x2d, p2d, wn, wgate, wproj, wpost):
    T, D = x2d.shape
    P = p2d.shape[1]
    tm = ROW_TILE
    const = lambda i: (0, 0)
    return pl.pallas_call(
        _ple_kernel,
        grid=(T // tm,),
        in_specs=[
            pl.BlockSpec((tm, D), lambda i: (i, 0)),
            pl.BlockSpec((tm, P), lambda i: (i, 0)),
            pl.BlockSpec((1, D), const),
            pl.BlockSpec((D, D), const),
            pl.BlockSpec((P, D), const),
            pl.BlockSpec((1, D), const),
        ],
        out_specs=pl.BlockSpec((tm, D), lambda i: (i, 0)),
        out_shape=jax.ShapeDtypeStruct((T, D), F32),
        compiler_params=pltpu.CompilerParams(
            dimension_semantics=("arbitrary",), vmem_limit_bytes=VMEM_LIMIT_BYTES),
        name="ple",
    )(x2d, p2d, wn, wgate, wproj, wpost)


def _layer(x2d, p2d, batch, seq, w_norm_mix, w_in, fox_f_bias, q_norm_w, k_norm_w, mlstm_conv_w, mlstm_conv_b,
           mlstm_i_bias, mlstm_f_bias, mlstm_out_norm_w, w_out, w_norm_ffn, w_ffn_gate, w_ffn_up, w_ffn_down,
           w_norm_ple, w_ple_gate, w_ple_proj, w_ple_post_norm):
    D = x2d.shape[1]
    A, QK, MV = ATTN_WIDTH, MLSTM_QK_WIDTH, MLSTM_WIDTH
    o_af = 3 * A
    o_m = o_af + ATTN_HEADS
    o_mi = o_m + 2 * QK + MV
    o_mo = o_mi + 2 * MLSTM_HEADS
    w_t = w_in.T.astype(BF16)
    w_parts = (w_t[:o_af], w_t[o_m:o_mi], w_t[o_mo:])
    wgt = jnp.concatenate([w_t[o_af:o_m], w_t[o_mi:o_mo]], axis=0)
    gbt = jnp.concatenate([fox_f_bias, mlstm_i_bias, mlstm_f_bias]).astype(F32)[:, None]
    cw = mlstm_conv_w.reshape(CONV_WIDTH, 2, QK).transpose(1, 0, 2)
    cb = mlstm_conv_b.reshape(2, 1, QK)

    proj, gcol, grow = _in_proj(x2d, w_norm_mix[None, :], w_parts, wgt, gbt, q_norm_w[None, :],
                                k_norm_w[None, :], cw, cb, batch=batch, seq=seq)
    proj3 = proj.reshape(batch, seq, proj.shape[1])
    attn, (wg16, wu16, wd16, wpg16, wo16) = _fox_attention(
        proj3, grow, gcol, (w_ffn_gate, w_ffn_up, w_ffn_down, w_ple_gate, w_out), batch=batch, seq=seq)
    mlstm = _mlstm(proj3, gcol, grow, mlstm_out_norm_w[None, :], batch=batch, seq=seq)

    x1 = _out_proj(x2d, attn.reshape(batch * seq, A), mlstm.reshape(batch * seq, MV), wo16)
    x2 = _ffn(x1, w_norm_ffn[None, :], wg16, wu16, wd16)
    x3 = _ple(x2, p2d, w_norm_ple[None, :], wpg16, w_ple_proj.astype(BF16), w_ple_post_norm[None, :])
    return x3


def kernel(x, p, w_norm_mix, w_in, fox_f_bias, q_norm_w, k_norm_w, mlstm_conv_w, mlstm_conv_b, mlstm_i_bias,
           mlstm_f_bias, mlstm_out_norm_w, w_out, w_norm_ffn, w_ffn_gate, w_ffn_up, w_ffn_down, w_norm_ple,
           w_ple_gate, w_ple_proj, w_ple_post_norm):
    B, S, D = x.shape
    depth = w_in.shape[0]
    x2d = x.reshape(B * S, D)
    for i in range(depth):
        x2d = _layer(x2d, p[i].reshape(B * S, p.shape[-1]), B, S, w_norm_mix[i], w_in[i], fox_f_bias[i],
                     q_norm_w[i], k_norm_w[i], mlstm_conv_w[i], mlstm_conv_b[i], mlstm_i_bias[i], mlstm_f_bias[i],
                     mlstm_out_norm_w[i], w_out[i], w_norm_ffn[i], w_ffn_gate[i], w_ffn_up[i], w_ffn_down[i],
                     w_norm_ple[i], w_ple_gate[i], w_ple_proj[i], w_ple_post_norm[i])
    return x2d.reshape(B, S, D)
```

```python
import functools
import math

import jax
import jax.numpy as jnp
from jax import lax
from jax.experimental import pallas as pl
from jax.experimental.pallas import tpu as pltpu

F32 = jnp.float32
BF16 = jnp.bfloat16
EPS = 1e-6

ATTN_HEADS = 8
ATTN_HEAD_DIM = 128
MLSTM_HEADS = 4
MLSTM_QK_DIM = 128
MLSTM_V_DIM = 256
CONV_WIDTH = 4
ATTN_WIDTH = ATTN_HEADS * ATTN_HEAD_DIM
MLSTM_QK_WIDTH = MLSTM_HEADS * MLSTM_QK_DIM
MLSTM_WIDTH = MLSTM_HEADS * MLSTM_V_DIM
N_GATES = ATTN_HEADS + 2 * MLSTM_HEADS
GATE_LANES = 128
CONV_HALO = 8
BF16_SUBLANES = 16

VMEM_LIMIT_BYTES = 56 * 1024 * 1024

ROW_TILE = 512
PROJ_COL_TILE = 512
MLSTM_CHUNK = 256
MLSTM_CHUNKS_PER_STEP = 2
ONES_LANES = 128
ATTN_Q_TILE = 1024
ATTN_KV_TILE = 512
ATTN_HEADS_PER_STEP = 2
FFN_COL_TILE = 512
FFN_ROW_TILE = 1024
LOG2E = math.log2(math.e)
FOX_MAX_BOUND = 40.0

_T_AQ, _T_AK, _T_AV, _T_MQ, _T_MK, _T_MV, _T_MO, _T_END = 0, 2, 4, 6, 7, 8, 10, 12


def _log_sigmoid(z):
    return jnp.minimum(z, 0.0) - jnp.log1p(jnp.exp(-jnp.abs(z)))


def _sigmoid(z):
    return 1.0 / (1.0 + jnp.exp(-z))


def _lane_prefix_sum(v, period):
    axis = v.ndim - 1
    pos = lax.broadcasted_iota(jnp.int32, v.shape, axis) & (period - 1)
    shift = 1
    while shift < period:
        v = v + jnp.where(pos >= shift, pltpu.roll(v, shift, axis=axis), 0.0)
        shift *= 2
    return v


def _in_proj_kernel(x_ref, wn_ref, wa_ref, wm_ref, wo_ref, wgt_ref, gbt_ref, qn_ref, kn_ref, cw_ref, cb_ref,
                    proj_ref, gcol_ref, grow_ref, h_scr, carry_scr, conv_scr, *, n_tiles, tiles_per_seq, chunk):
    i = pl.program_id(0)
    tm = x_ref.shape[0]
    tn = PROJ_COL_TILE
    seq_start = ((i - 1) % tiles_per_seq) == 0
    slot_in, slot_out = [0], [0]

    def tile_acc(t):
        w_ref, t0 = (wa_ref, _T_AQ) if t < _T_MQ else (wm_ref, _T_MQ) if t < _T_MO else (wo_ref, _T_MO)
        return lax.dot_general(h_scr[slot_in[0]], w_ref[(t - t0) * tn:(t - t0 + 1) * tn, :],
                               (((1,), (1,)), ((), ())), preferred_element_type=F32)

    def store(t, val):
        proj_ref[:, t * tn:(t + 1) * tn] = val.astype(BF16)

    def head_norm(acc, w):
        outs = []
        for hh in range(tn // ATTN_HEAD_DIM):
            a = acc[:, hh * ATTN_HEAD_DIM:(hh + 1) * ATTN_HEAD_DIM]
            ms = jnp.mean(a * a, axis=-1, keepdims=True)
            outs.append((a * lax.rsqrt(ms + EPS)) * w)
        return jnp.concatenate(outs, axis=-1)

    def conv_silu(acc, which, scale):
        halo = conv_scr.at[which]
        cw = cw_ref[which]
        top = lax.broadcasted_iota(jnp.int32, (CONV_HALO, tn), 0) == 0
        z = cw[0:1, :] * acc
        for tap in range(1, CONV_WIDTH):
            carried = jnp.where(seq_start, 0.0, halo[tap - 1])
            halo[tap - 1] = jnp.broadcast_to(z[tm - 1:tm, :], (CONV_HALO, tn))
            shifted = pltpu.roll(z, 1, axis=0)
            shifted = jnp.concatenate([jnp.where(top, carried, shifted[:CONV_HALO]), shifted[CONV_HALO:]], axis=0)
            z = cw[tap:tap + 1, :] * acc + shifted
        y = cb_ref[which] + z
        y = y * _sigmoid(y)
        return y * scale if scale != 1.0 else y

    def norm_and_gates():
        xf = x_ref[...]
        ms = jnp.mean(xf * xf, axis=-1, keepdims=True)
        hb = ((xf * lax.rsqrt(ms + EPS)) * wn_ref[...]).astype(BF16)

        gr = lax.dot_general(wgt_ref[...], hb, (((1,), (1,)), ((), ())), preferred_element_type=F32) + gbt_ref[...]
        row = lax.broadcasted_iota(jnp.int32, gr.shape, 0)
        is_in_gate = (row >= ATTN_HEADS) & (row < ATTN_HEADS + MLSTM_HEADS)
        gr = jnp.where(is_in_gate, gr, _log_sigmoid(gr))
        cs_chunk = _lane_prefix_sum(gr, chunk)
        cs_full = _lane_prefix_sum(gr, tm)
        carry = jnp.where((i % tiles_per_seq) == 0, 0.0, carry_scr[...])
        cs_full = cs_full + carry[:, :1]
        b_on_in_rows = pltpu.roll(cs_chunk, N_GATES - MLSTM_HEADS, axis=0)
        gates = jnp.where(row < ATTN_HEADS, cs_full * LOG2E, jnp.where(is_in_gate, gr - b_on_in_rows, cs_chunk))
        padded = jnp.concatenate([gates, jnp.zeros((GATE_LANES - gates.shape[0], tm), F32)], axis=0)
        h_scr[slot_out[0]] = hb
        carry_scr[...] = jnp.broadcast_to(cs_full[:, tm - 1:tm], carry_scr.shape)
        grow_ref[...] = gates
        gcol_ref[...] = padded.T

    def project(between):
        qw = qn_ref[...] * (LOG2E * ATTN_HEAD_DIM ** -0.5)
        heavy = [lambda: store(_T_MQ, conv_silu(tile_acc(_T_MQ), 0, MLSTM_QK_DIM ** -0.5)),
                 lambda: store(_T_MK, conv_silu(tile_acc(_T_MK), 1, 1.0))]
        heavy += [functools.partial(lambda t: store(t, head_norm(tile_acc(t), qw)), t) for t in range(_T_AQ, _T_AK)]
        heavy += [functools.partial(lambda t: store(t, head_norm(tile_acc(t), kn_ref[...])), t)
                  for t in range(_T_AK, _T_AV)]
        plain = [functools.partial(lambda t: store(t, tile_acc(t)), t)
                 for t in list(range(_T_AV, _T_MQ)) + list(range(_T_MV, _T_END))]
        for n, (hv, pn) in enumerate(zip(heavy, plain)):
            hv()
            pn()
            if n == 1:
                between()

    @pl.when(i == 0)
    def _():
        carry_scr[...] = jnp.zeros_like(carry_scr)
        conv_scr[...] = jnp.zeros_like(conv_scr)
        slot_out[0] = 0
        norm_and_gates()

    for parity in range(2):
        @pl.when((i > 0) & (i % 2 == parity))
        def _():
            slot_in[0], slot_out[0] = 1 - parity, parity
            project(norm_and_gates)


def _in_proj(x2d, wn, w_parts, wgt, gbt, qn, kn, cw, cb, *, batch, seq):
    T, D = x2d.shape
    tm, tn = ROW_TILE, PROJ_COL_TILE
    n_cols = sum(w.shape[0] for w in w_parts)
    assert [w.shape[0] // tn for w in w_parts] == [_T_MQ - _T_AQ, _T_MO - _T_MQ, _T_END - _T_MO]
    n_tiles = T // tm
    tiles_per_seq = seq // tm
    const = lambda i: (0, 0)
    const3 = lambda i: (0, 0, 0)
    tile = lambda i: jnp.minimum(i, n_tiles - 1)
    kern = functools.partial(_in_proj_kernel, n_tiles=n_tiles, tiles_per_seq=tiles_per_seq, chunk=MLSTM_CHUNK)
    return pl.pallas_call(
        kern,
        grid=(n_tiles + 1,),
        in_specs=[
            pl.BlockSpec((tm, D), lambda i: (tile(i), 0)),
            pl.BlockSpec((1, D), const),
            *[pl.BlockSpec(w.shape, const, pipeline_mode=pl.Buffered(1)) for w in w_parts],
            pl.BlockSpec((N_GATES, D), const),
            pl.BlockSpec((N_GATES, 1), const),
            pl.BlockSpec((1, ATTN_HEAD_DIM), const),
            pl.BlockSpec((1, ATTN_HEAD_DIM), const),
            pl.BlockSpec((2, CONV_WIDTH, tn), const3),
            pl.BlockSpec((2, 1, tn), const3),
        ],
        out_specs=[
            pl.BlockSpec((tm, n_cols), lambda i: (jnp.maximum(i - 1, 0), 0)),
            pl.BlockSpec((tm, GATE_LANES), lambda i: (i, 0)),
            pl.BlockSpec((None, N_GATES, tm), lambda i: (i, 0, 0)),
        ],
        out_shape=[
            jax.ShapeDtypeStruct((T, n_cols), BF16),
            jax.ShapeDtypeStruct((T + tm, GATE_LANES), F32),
            jax.ShapeDtypeStruct((n_tiles + 1, N_GATES, tm), F32),
        ],
        scratch_shapes=[
            pltpu.VMEM((2, tm, D), BF16),
            pltpu.VMEM((N_GATES, GATE_LANES), F32),
            pltpu.VMEM((2, CONV_WIDTH - 1, CONV_HALO, tn), F32),
        ],
        compiler_params=pltpu.CompilerParams(
            dimension_semantics=("arbitrary",), vmem_limit_bytes=VMEM_LIMIT_BYTES),
        name="in_proj",
    )(x2d, wn, *w_parts, wgt, gbt, qn, kn, cw, cb)


def _fox_kernel(q_ref, k_ref, v_ref, c_ref, ccol_ref, *rest, tk, cast_blocks, n_steps):
    n_cast = len(cast_blocks)
    w32_refs, (o_ref, *w16_refs), kmax_scr = rest[:n_cast], rest[n_cast:2 * n_cast + 1], rest[-1]
    hg = pl.program_id(1)
    qi = pl.program_id(2)
    step = (pl.program_id(0) * pl.num_programs(1) + hg) * pl.num_programs(2) + qi
    for w32, w16, n_blocks in zip(w32_refs, w16_refs, cast_blocks):
        if n_blocks == n_steps:
            w16[...] = w32[...].astype(w16.dtype)
        else:
            @pl.when(step < n_blocks)
            def _():
                w16[...] = w32[...].astype(w16.dtype)

    tq = q_ref.shape[0]
    d = ATTN_HEAD_DIM
    n_heads = q_ref.shape[1] // d
    n_diag = tq // tk
    qs = [q_ref[:, g * d:(g + 1) * d] for g in range(n_heads)]

    @pl.when(qi == 0)
    def _():
        for g in range(n_heads):
            kf = k_ref[:, g * d:(g + 1) * d].astype(F32)
            k2 = jnp.max(jnp.sum(kf * kf, axis=-1, keepdims=True), axis=0, keepdims=True)
            kmax_scr[g] = jnp.broadcast_to(k2, kmax_scr.shape[1:])

    lane = lax.broadcasted_iota(jnp.int32, ccol_ref.shape, 1)
    ccol = ccol_ref[...]
    mis, bounds = [], []
    for g in range(n_heads):
        qf = qs[g].astype(F32)
        bound = jnp.sqrt(jnp.sum(qf * qf, axis=-1, keepdims=True) * kmax_scr[g][:1, :1])
        ci = jnp.sum(jnp.where(lane == hg * n_heads + g, ccol, 0.0), axis=-1, keepdims=True)
        mis.append(bound - ci)
        bounds.append(jnp.max(bound))
    worst = functools.reduce(jnp.maximum, bounds)

    def load_block(j, g):
        start = j * tk if isinstance(j, int) else pl.multiple_of(j * tk, tk)
        k = k_ref[pl.ds(start, tk), g * d:(g + 1) * d]
        v = v_ref[pl.ds(start, tk), g * d:(g + 1) * d]
        per_tile = c_ref.shape[2] // tk
        cj = c_ref[j // per_tile, pl.ds(hg * n_heads + g, 1), pl.ds((j % per_tile) * tk, tk)]
        return k, v, cj

    def causal_mask(r0):
        r = lax.broadcasted_iota(jnp.int32, (tq - r0, tk), 0)
        c = lax.broadcasted_iota(jnp.int32, (tq - r0, tk), 1)
        return c <= r

    def rejoin(old, new, r0):
        return jnp.concatenate([old[:r0], new], axis=0) if r0 else new

    def finish(carry):
        for g in range(n_heads):
            l, acc = carry[g][-2:]
            l = jnp.sum(l, axis=-1, keepdims=True)
            o_ref[:, g * d:(g + 1) * d] = (acc / l).astype(o_ref.dtype)

    def scores(j, g, r0):
        k = load_block(j, g)[0]
        return lax.dot_general(qs[g][r0:], k, (((1,), (1,)), ((), ())), preferred_element_type=F32)

    def consume(j, g, s, l, acc, diag):
        r0 = 0 if diag is None else diag * tk
        _, v, cj = load_block(j, g)
        e = (s - cj) - mis[g][r0:]
        if diag is not None:
            e = jnp.where(causal_mask(r0), e, -jnp.inf)
        p = jnp.exp2(e)
        l_new = l[r0:] + functools.reduce(jnp.add, [p[:, c0:c0 + d] for c0 in range(0, tk, d)])
        acc_new = acc[r0:] + jnp.dot(p.astype(BF16), v, preferred_element_type=F32)
        return rejoin(l, l_new, r0), rejoin(acc, acc_new, r0)

    def bounded_step(j, carry, diag):
        r0 = 0 if diag is None else diag * tk
        return tuple(consume(j, g, scores(j, g, r0), *carry[g], diag) for g in range(n_heads))


    def online_step(j, carry, diag):
        r0 = 0 if diag is None else diag * tk
        out = []
        for g in range(n_heads):
            m, l, acc = carry[g]
            k, v, cj = load_block(j, g)
            s = lax.dot_general(qs[g][r0:], k, (((1,), (1,)), ((), ())), preferred_element_type=F32) - cj
            if diag is not None:
                s = jnp.where(causal_mask(r0), s, -jnp.inf)
            m_new = jnp.maximum(m[r0:], jnp.max(s, axis=-1, keepdims=True))
            alpha = jnp.exp2(m[r0:] - m_new)
            p = jnp.exp2(s - m_new)
            l_new = alpha * l[r0:] + jnp.sum(p, axis=-1, keepdims=True)
            acc_new = alpha * acc[r0:] + jnp.dot(p.astype(BF16), v, preferred_element_type=F32)
            out.append((rejoin(m, m_new, r0), rejoin(l, l_new, r0), rejoin(acc, acc_new, r0)))
        return tuple(out)

    def run(step, init):
        carry = lax.fori_loop(0, qi * n_diag, lambda j, c: step(j, c, None), init)
        for jj in range(n_diag):
            carry = step(qi * n_diag + jj, carry, jj)
        finish(carry)

    zeros = (jnp.zeros((tq, 1), F32), jnp.zeros((tq, d), F32))

    def run_unrolled(step, init):
        for c in range(k_ref.shape[0] // tq):
            @pl.when(qi == c)
            def _():
                carry = init
                for j in range(c * n_diag):
                    carry = step(j, carry, None)
                for jj in range(n_diag):
                    carry = step(c * n_diag + jj, carry, jj)
                finish(carry)

    @pl.when(worst <= FOX_MAX_BOUND)
    def _():
        run_unrolled(bounded_step,
                     tuple((jnp.zeros((tq, d), F32), jnp.zeros((tq, d), F32)) for _ in range(n_heads)))

    @pl.when(jnp.logical_not(worst <= FOX_MAX_BOUND))
    def _():
        run(online_step, tuple((jnp.full((tq, 1), -jnp.inf, F32),) + zeros for _ in range(n_heads)))


def _cast_block_rows(rows, n_steps):
    for rb in range(BF16_SUBLANES, rows + 1, BF16_SUBLANES):
        if rows % rb == 0 and rows // rb <= n_steps:
            return rb
    raise ValueError(f"cannot split {rows} rows over {n_steps} steps")


def _fox_attention(proj3, grow, gcol3, f32_weights, *, batch, seq):
    tq, tk = ATTN_Q_TILE, ATTN_KV_TILE
    assert tq % tk == 0
    dd = ATTN_HEAD_DIM * ATTN_HEADS_PER_STEP
    n_groups = ATTN_HEADS // ATTN_HEADS_PER_STEP
    nq = seq // tq
    grid = (batch, n_groups, nq)
    n_steps = batch * n_groups * nq
    cast_rows = [_cast_block_rows(w.shape[0], n_steps) for w in f32_weights]
    cast_blocks = tuple(w.shape[0] // rb for w, rb in zip(f32_weights, cast_rows))

    def cast_spec(w, rb, n_blocks):
        return pl.BlockSpec((rb, w.shape[1]), lambda b, h, qi: (jnp.minimum((b * n_groups + h) * nq + qi,
                                                                            n_blocks - 1), 0))

    cast_specs = [cast_spec(w, rb, nb) for w, rb, nb in zip(f32_weights, cast_rows, cast_blocks)]
    out = pl.pallas_call(
        functools.partial(_fox_kernel, tk=tk, cast_blocks=cast_blocks, n_steps=n_steps),
        grid=grid,
        in_specs=[
            pl.BlockSpec((None, tq, dd), lambda b, h, qi: (b, qi, h)),
            pl.BlockSpec((None, seq, dd), lambda b, h, qi: (b, 0, n_groups + h)),
            pl.BlockSpec((None, seq, dd), lambda b, h, qi: (b, 0, 2 * n_groups + h)),
            pl.BlockSpec((seq // ROW_TILE, N_GATES, ROW_TILE), lambda b, h, qi: (b, 0, 0)),
            pl.BlockSpec((tq, GATE_LANES), lambda b, h, qi: (b * (seq // tq) + qi, 0)),
            *cast_specs,
        ],
        out_specs=[pl.BlockSpec((None, tq, dd), lambda b, h, qi: (b, qi, h)), *cast_specs],
        out_shape=[jax.ShapeDtypeStruct((batch, seq, ATTN_WIDTH), BF16),
                   *[jax.ShapeDtypeStruct(w.shape, BF16) for w in f32_weights]],
        scratch_shapes=[pltpu.VMEM((ATTN_HEADS_PER_STEP, 8, 128), F32)],
        compiler_params=pltpu.CompilerParams(
            dimension_semantics=("arbitrary", "arbitrary", "arbitrary"), vmem_limit_bytes=VMEM_LIMIT_BYTES),
        name="fox_attention",
    )(proj3, proj3, proj3, grow, gcol3, *f32_weights)
    return out[0], out[1:]


def _mlstm_kernel(q_ref, k_ref, v_ref, og_ref, gcol_ref, grow_ref, nw_ref, o_ref, c_scr, m_scr):
    ci = pl.program_id(1)
    L = MLSTM_CHUNK
    dk, dv = MLSTM_QK_DIM, MLSTM_V_DIM

    @pl.when(ci == 0)
    def _():
        c_scr[...] = jnp.zeros_like(c_scr)
        m_scr[...] = jnp.zeros_like(m_scr)

    t_idx = lax.broadcasted_iota(jnp.int32, (L, L), 0)
    s_idx = lax.broadcasted_iota(jnp.int32, (L, L), 1)
    causal = s_idx <= t_idx
    nt = (((1,), (1,)), ((), ()))
    heads = range(MLSTM_HEADS)
    units = [(c, hh) for c in range(q_ref.shape[0] // L) for hh in heads]

    rows = {u: slice(u[0] * L, (u[0] + 1) * L) for u in units}
    q = {u: q_ref[rows[u], u[1] * dk:(u[1] + 1) * dk] for u in units}
    k = {u: k_ref[rows[u], u[1] * dk:(u[1] + 1) * dk] for u in units}
    v = {u: v_ref[rows[u], u[1] * dv:(u[1] + 1) * dv] for u in units}
    gi = {u: ATTN_HEADS + u[1] for u in units}
    gf = {u: ATTN_HEADS + MLSTM_HEADS + u[1] for u in units}
    gt = grow_ref.shape[2]
    grow_at = {u: (u[0] * L // gt, slice(u[0] * L % gt, u[0] * L % gt + L)) for u in units}
    b_row = {u: grow_ref[grow_at[u][0], gf[u]:gf[u] + 1, grow_at[u][1]] for u in units}
    b_col = {u: gcol_ref[rows[u], gf[u]:gf[u] + 1] for u in units}
    a_row = {u: grow_ref[grow_at[u][0], gi[u]:gi[u] + 1, grow_at[u][1]] for u in units}
    a_col = {u: gcol_ref[rows[u], gi[u]:gi[u] + 1] for u in units}
    ones = jnp.ones((L, ONES_LANES), BF16)
    v1 = {u: jnp.concatenate([v[u], ones], axis=-1) for u in units}
    qk = {u: lax.dot_general(q[u], k[u], nt, preferred_element_type=F32) for u in units}
    m_loc = {u: jnp.max(jnp.where(causal, a_row[u], -jnp.inf), axis=-1, keepdims=True) for u in units}
    s_loc = {u: qk[u] * jnp.exp(jnp.where(causal, a_row[u] - m_loc[u], -jnp.inf)) for u in units}
    pv1 = {u: jnp.dot(s_loc[u].astype(BF16), v1[u], preferred_element_type=F32) for u in units}
    pv_loc = {u: pv1[u][:, :dv] for u in units}
    d_loc = {u: pv1[u][:, dv:dv + 1] for u in units}
    a_max = {u: jnp.max(a_row[u], axis=-1, keepdims=True) for u in units}
    kw = {u: (k[u].astype(F32) * jnp.exp(a_col[u] - a_max[u])).astype(BF16) for u in units}
    upd1 = {u: lax.dot_general(kw[u], v1[u], (((0,), (0,)), ((), ())), preferred_element_type=F32)
            for u in units}

    m_prev = [m_scr[hh][:1, :1] for hh in heads]
    ct1 = [c_scr[hh] for hh in heads]
    for c in range(q_ref.shape[0] // L):
        us = [(c, hh) for hh in heads]
        qc1 = [jnp.dot(q[u], ct1[u[1]].astype(BF16), preferred_element_type=F32) for u in us]
        qc = [x[:, :dv] for x in qc1]
        qn = [x[:, dv:dv + 1] for x in qc1]
        big_m = [jnp.maximum(m_loc[u], m_prev[u[1]]) for u in us]
        alpha = [jnp.exp(m_loc[u] - big_m[hh]) for hh, u in enumerate(us)]
        beta = [jnp.exp(m_prev[hh] - big_m[hh]) for hh in heads]
        num = [alpha[hh] * pv_loc[u] + beta[hh] * qc[hh] for hh, u in enumerate(us)]
        den = [alpha[hh] * d_loc[u] + beta[hh] * qn[hh] for hh, u in enumerate(us)]
        dmax = [jnp.maximum(jnp.abs(den[hh]), jnp.exp(-(b_col[u] + big_m[hh]))) for hh, u in enumerate(us)]
        ms = [jnp.mean(num[hh] * num[hh], axis=-1, keepdims=True) for hh in heads]
        for hh, u in enumerate(us):
            hn = num[hh] * lax.rsqrt(ms[hh] + EPS * (dmax[hh] * dmax[hh]))
            og = og_ref[rows[u], hh * dv:(hh + 1) * dv].astype(F32)
            out = (hn * nw_ref[:, hh * dv:(hh + 1) * dv]) * _sigmoid(og)
            o_ref[rows[u], hh * dv:(hh + 1) * dv] = out.astype(o_ref.dtype)
        m_last = [jnp.maximum(a_max[u], m_prev[u[1]]) for u in us]
        wc = [jnp.exp(m_prev[hh] - m_last[hh]) for hh in heads]
        gamma = [jnp.exp(a_max[u] - m_last[hh]) for hh, u in enumerate(us)]
        ct1 = [wc[hh] * ct1[hh] + gamma[hh] * upd1[u] for hh, u in enumerate(us)]
        m_prev = [b_row[u][:, L - 1:L] + m_last[hh] for hh, u in enumerate(us)]

    for hh in heads:
        c_scr[hh] = ct1[hh]
        m_scr[hh] = jnp.broadcast_to(m_prev[hh], m_scr.shape[1:])


def _mlstm(proj3, gcol3, grow, nw, *, batch, seq):
    L = MLSTM_CHUNK * MLSTM_CHUNKS_PER_STEP
    assert L % ROW_TILE == 0
    grid = (batch, seq // L)
    qk_w, v_w = MLSTM_QK_WIDTH, MLSTM_WIDTH
    base = 3 * ATTN_WIDTH
    return pl.pallas_call(
        _mlstm_kernel,
        grid=grid,
        in_specs=[
            pl.BlockSpec((None, L, qk_w), lambda b, c: (b, c, base // qk_w)),
            pl.BlockSpec((None, L, qk_w), lambda b, c: (b, c, base // qk_w + 1)),
            pl.BlockSpec((None, L, v_w), lambda b, c: (b, c, (base + 2 * qk_w) // v_w)),
            pl.BlockSpec((None, L, v_w), lambda b, c: (b, c, (base + 2 * qk_w) // v_w + 1)),
            pl.BlockSpec((L, GATE_LANES), lambda b, c: (b * (seq // L) + c, 0)),
            pl.BlockSpec((L // ROW_TILE, N_GATES, ROW_TILE), lambda b, c: (b * (seq // L) + c, 0, 0)),
            pl.BlockSpec((1, v_w), lambda b, c: (0, 0)),
        ],
        out_specs=pl.BlockSpec((None, L, v_w), lambda b, c: (b, c, 0)),
        out_shape=jax.ShapeDtypeStruct((batch, seq, v_w), BF16),
        scratch_shapes=[
            pltpu.VMEM((MLSTM_HEADS, MLSTM_QK_DIM, MLSTM_V_DIM + ONES_LANES), F32),
            pltpu.VMEM((MLSTM_HEADS, 8, 128), F32),
        ],
        compiler_params=pltpu.CompilerParams(
            dimension_semantics=("arbitrary", "arbitrary"), vmem_limit_bytes=VMEM_LIMIT_BYTES),
        name="mlstm",
    )(proj3, proj3, proj3, proj3, gcol3, grow, nw)


def _out_proj_kernel(x_ref, a_ref, m_ref, wa_ref, wm_ref, o_ref):
    y = jnp.dot(a_ref[...], wa_ref[...], preferred_element_type=F32)
    y = y + jnp.dot(m_ref[...], wm_ref[...], preferred_element_type=F32)
    o_ref[...] = x_ref[...] + y


def _out_proj(x2d, attn2d, mlstm2d, w_out):
    T, D = x2d.shape
    tm = ROW_TILE
    half = attn2d.shape[1]
    assert mlstm2d.shape[1] == half and w_out.shape == (2 * half, D)
    return pl.pallas_call(
        _out_proj_kernel,
        grid=(T // tm,),
        in_specs=[
            pl.BlockSpec((tm, D), lambda i: (i, 0)),
            pl.BlockSpec((tm, attn2d.shape[1]), lambda i: (i, 0)),
            pl.BlockSpec((tm, mlstm2d.shape[1]), lambda i: (i, 0)),
            pl.BlockSpec((half, D), lambda i: (0, 0)),
            pl.BlockSpec((half, D), lambda i: (1, 0)),
        ],
        out_specs=pl.BlockSpec((tm, D), lambda i: (i, 0)),
        out_shape=jax.ShapeDtypeStruct((T, D), F32),
        compiler_params=pltpu.CompilerParams(
            dimension_semantics=("arbitrary",), vmem_limit_bytes=VMEM_LIMIT_BYTES),
        name="out_proj",
    )(x2d, attn2d, mlstm2d, w_out, w_out)


def _ffn_kernel(x_ref, wn_ref, wg_ref, wu_ref, wd_ref, o_ref, h_scr):
    f = pl.program_id(1)

    @pl.when(f == 0)
    def _():
        xf = x_ref[...]
        ms = jnp.mean(xf * xf, axis=-1, keepdims=True)
        h_scr[...] = ((xf * lax.rsqrt(ms + EPS)) * wn_ref[...]).astype(BF16)
        o_ref[...] = xf

    hb = h_scr[...]
    g = jnp.dot(hb, wg_ref[...], preferred_element_type=F32)
    u = jnp.dot(hb, wu_ref[...], preferred_element_type=F32)
    a = (g * _sigmoid(g)) * u
    o_ref[...] += jnp.dot(a.astype(BF16), wd_ref[...], preferred_element_type=F32)


def _ffn(x2d, wn, wg, wu, wd):
    T, D = x2d.shape
    F = wg.shape[1]
    tm, tf = FFN_ROW_TILE, FFN_COL_TILE
    return pl.pallas_call(
        _ffn_kernel,
        grid=(T // tm, F // tf),
        in_specs=[
            pl.BlockSpec((tm, D), lambda i, f: (i, 0)),
            pl.BlockSpec((1, D), lambda i, f: (0, 0)),
            pl.BlockSpec((D, tf), lambda i, f: (0, f)),
            pl.BlockSpec((D, tf), lambda i, f: (0, f)),
            pl.BlockSpec((tf, D), lambda i, f: (f, 0)),
        ],
        out_specs=pl.BlockSpec((tm, D), lambda i, f: (i, 0)),
        out_shape=jax.ShapeDtypeStruct((T, D), F32),
        scratch_shapes=[pltpu.VMEM((tm, D), BF16)],
        compiler_params=pltpu.CompilerParams(
            dimension_semantics=("arbitrary", "arbitrary"), vmem_limit_bytes=VMEM_LIMIT_BYTES),
        name="ffn",
    )(x2d, wn, wg, wu, wd)


def _ple_kernel(x_ref, p_ref, wn_ref, wgate_ref, wproj_ref, wpost_ref, o_ref):
    xf = x_ref[...]
    ms = jnp.mean(xf * xf, axis=-1, keepdims=True)
    hb = ((xf * lax.rsqrt(ms + EPS)) * wn_ref[...]).astype(BF16)
    e = jnp.dot(p_ref[...].astype(BF16), wproj_ref[...], preferred_element_type=F32)
    ems = jnp.mean(e * e, axis=-1, keepdims=True)
    e = (e * lax.rsqrt(ems + EPS)) * wpost_ref[...]
    tn = PROJ_COL_TILE
    for c0 in range(0, o_ref.shape[1], tn):
        gate = _sigmoid(jnp.dot(hb, wgate_ref[:, c0:c0 + tn], preferred_element_type=F32))
        o_ref[:, c0:c0 + tn] = xf[:, c0:c0 + tn] + gate * e[:, c0:c0 + tn]


def _ple(x2d, p2d, wn, wgate, wproj, wpost):
    T, D = x2d.shape
    P = p2d.shape[1]
    tm = ROW_TILE
    const = lambda i: (0, 0)
    return pl.pallas_call(
        _ple_kernel,
        grid=(T // tm,),
        in_specs=[
            pl.BlockSpec((tm, D), lambda i: (i, 0)),
            pl.BlockSpec((tm, P), lambda i: (i, 0)),
            pl.BlockSpec((1, D), const),
            pl.BlockSpec((D, D), const),
            pl.BlockSpec((P, D), const),
            pl.BlockSpec((1, D), const),
        ],
        out_specs=pl.BlockSpec((tm, D), lambda i: (i, 0)),
        out_shape=jax.ShapeDtypeStruct((T, D), F32),
        compiler_params=pltpu.CompilerParams(
            dimension_semantics=("arbitrary",), vmem_limit_bytes=VMEM_LIMIT_BYTES),
        name="ple",
    )(x2d, p2d, wn, wgate, wproj, wpost)


def _layer(x2d, p2d, batch, seq, w_norm_mix, w_in, fox_f_bias, q_norm_w, k_norm_w, mlstm_conv_w, mlstm_conv_b,
           mlstm_i_bias, mlstm_f_bias, mlstm_out_norm_w, w_out, w_norm_ffn, w_ffn_gate, w_ffn_up, w_ffn_down,
           w_norm_ple, w_ple_gate, w_ple_proj, w_ple_post_norm):
    D = x2d.shape[1]
    A, QK, MV = ATTN_WIDTH, MLSTM_QK_WIDTH, MLSTM_WIDTH
    o_af = 3 * A
    o_m = o_af + ATTN_HEADS
    o_mi = o_m + 2 * QK + MV
    o_mo = o_mi + 2 * MLSTM_HEADS
    w_t = w_in.T.astype(BF16)
    w_parts = (w_t[:o_af], w_t[o_m:o_mi], w_t[o_mo:])
    wgt = jnp.concatenate([w_t[o_af:o_m], w_t[o_mi:o_mo]], axis=0)
    gbt = jnp.concatenate([fox_f_bias, mlstm_i_bias, mlstm_f_bias]).astype(F32)[:, None]
    cw = mlstm_conv_w.reshape(CONV_WIDTH, 2, QK).transpose(1, 0, 2)
    cb = mlstm_conv_b.reshape(2, 1, QK)

    proj, gcol, grow = _in_proj(x2d, w_norm_mix[None, :], w_parts, wgt, gbt, q_norm_w[None, :],
                                k_norm_w[None, :], cw, cb, batch=batch, seq=seq)
    proj3 = proj.reshape(batch, seq, proj.shape[1])
    attn, (wg16, wu16, wd16, wpg16, wo16) = _fox_attention(
        proj3, grow, gcol, (w_ffn_gate, w_ffn_up, w_ffn_down, w_ple_gate, w_out), batch=batch, seq=seq)
    mlstm = _mlstm(proj3, gcol, grow, mlstm_out_norm_w[None, :], batch=batch, seq=seq)

    x1 = _out_proj(x2d, attn.reshape(batch * seq, A), mlstm.reshape(batch * seq, MV), wo16)
    x2 = _ffn(x1, w_norm_ffn[None, :], wg16, wu16, wd16)
    x3 = _ple(x2, p2d, w_norm_ple[None, :], wpg16, w_ple_proj.astype(BF16), w_ple_post_norm[None, :])
    return x3


def kernel(x, p, w_norm_mix, w_in, fox_f_bias, q_norm_w, k_norm_w, mlstm_conv_w, mlstm_conv_b, mlstm_i_bias,
           mlstm_f_bias, mlstm_out_norm_w, w_out, w_norm_ffn, w_ffn_gate, w_ffn_up, w_ffn_down, w_norm_ple,
           w_ple_gate, w_ple_proj, w_ple_post_norm):
    B, S, D = x.shape
    depth = w_in.shape[0]
    x2d = x.reshape(B * S, D)
    for i in range(depth):
        x2d = _layer(x2d, p[i].reshape(B * S, p.shape[-1]), B, S, w_norm_mix[i], w_in[i], fox_f_bias[i],
                     q_norm_w[i], k_norm_w[i], mlstm_conv_w[i], mlstm_conv_b[i], mlstm_i_bias[i], mlstm_f_bias[i],
                     mlstm_out_norm_w[i], w_out[i], w_norm_ffn[i], w_ffn_gate[i], w_ffn_up[i], w_ffn_down[i],
                     w_norm_ple[i], w_ple_gate[i], w_ple_proj[i], w_ple_post_norm[i])
    return x2d.reshape(B, S, D)
```

```python
import functools
import math

import jax
import jax.numpy as jnp
from jax import lax
from jax.experimental import pallas as pl
from jax.experimental.pallas import tpu as pltpu

F32 = jnp.float32
BF16 = jnp.bfloat16
EPS = 1e-6

ATTN_HEADS = 8
ATTN_HEAD_DIM = 128
MLSTM_HEADS = 4
MLSTM_QK_DIM = 128
MLSTM_V_DIM = 256
CONV_WIDTH = 4
ATTN_WIDTH = ATTN_HEADS * ATTN_HEAD_DIM
MLSTM_QK_WIDTH = MLSTM_HEADS * MLSTM_QK_DIM
MLSTM_WIDTH = MLSTM_HEADS * MLSTM_V_DIM
N_GATES = ATTN_HEADS + 2 * MLSTM_HEADS
GATE_LANES = 128
CONV_HALO = 8
BF16_SUBLANES = 16

VMEM_LIMIT_BYTES = 56 * 1024 * 1024

ROW_TILE = 512
PROJ_COL_TILE = 256
MLSTM_CHUNK = 256
MLSTM_CHUNKS_PER_STEP = 2
ONES_LANES = 128
ATTN_Q_TILE = 1024
ATTN_KV_TILE = 512
ATTN_HEADS_PER_STEP = 2
FFN_COL_TILE = 512
FFN_ROW_TILE = 1024
LOG2E = math.log2(math.e)
FOX_MAX_BOUND = 40.0

_T_AQ, _T_AK, _T_AV, _T_MQ, _T_MK, _T_MV, _T_MO, _T_END = (
    w // PROJ_COL_TILE for w in (0, ATTN_WIDTH, 2 * ATTN_WIDTH, 3 * ATTN_WIDTH, 3 * ATTN_WIDTH + MLSTM_QK_WIDTH,
                                 3 * ATTN_WIDTH + 2 * MLSTM_QK_WIDTH, 3 * ATTN_WIDTH + 2 * MLSTM_QK_WIDTH + MLSTM_WIDTH,
                                 3 * ATTN_WIDTH + 2 * MLSTM_QK_WIDTH + 2 * MLSTM_WIDTH))


def _log_sigmoid(z):
    return jnp.minimum(z, 0.0) - jnp.log1p(jnp.exp(-jnp.abs(z)))


def _sigmoid(z):
    return 1.0 / (1.0 + jnp.exp(-z))


def _lane_prefix_sum(v, period):
    axis = v.ndim - 1
    pos = lax.broadcasted_iota(jnp.int32, v.shape, axis) & (period - 1)
    shift = 1
    while shift < period:
        v = v + jnp.where(pos >= shift, pltpu.roll(v, shift, axis=axis), 0.0)
        shift *= 2
    return v


def _in_proj_kernel(x_ref, wn_ref, wa_ref, wm_ref, wo_ref, wgt_ref, gbt_ref, qn_ref, kn_ref, cw_ref, cb_ref,
                    proj_ref, gcol_ref, grow_ref, h_scr, carry_scr, conv_scr, *, n_tiles, tiles_per_seq, chunk):
    i = pl.program_id(0)
    tm = x_ref.shape[0]
    tn = PROJ_COL_TILE
    seq_start = ((i - 1) % tiles_per_seq) == 0
    slot_in, slot_out = [0], [0]

    def tile_acc(t):
        w_ref, t0 = (wa_ref, _T_AQ) if t < _T_MQ else (wm_ref, _T_MQ) if t < _T_MO else (wo_ref, _T_MO)
        return lax.dot_general(h_scr[slot_in[0]], w_ref[(t - t0) * tn:(t - t0 + 1) * tn, :],
                               (((1,), (1,)), ((), ())), preferred_element_type=F32)

    def store(t, val):
        proj_ref[:, t * tn:(t + 1) * tn] = val.astype(BF16)

    def head_norm(acc, w):
        outs = []
        for hh in range(tn // ATTN_HEAD_DIM):
            a = acc[:, hh * ATTN_HEAD_DIM:(hh + 1) * ATTN_HEAD_DIM]
            ms = jnp.mean(a * a, axis=-1, keepdims=True)
            outs.append((a * lax.rsqrt(ms + EPS)) * w)
        return jnp.concatenate(outs, axis=-1)

    def conv_silu(acc, t):
        which = 0 if t < _T_MK else 1
        cols = pl.ds((t - (_T_MQ, _T_MK)[which]) * tn, tn)
        top = lax.broadcasted_iota(jnp.int32, (CONV_HALO, tn), 0) == 0
        z = cw_ref[which, 0:1, cols] * acc
        for tap in range(1, CONV_WIDTH):
            carried = jnp.where(seq_start, 0.0, conv_scr[which, tap - 1, :, cols])
            conv_scr[which, tap - 1, :, cols] = jnp.broadcast_to(z[tm - 1:tm, :], (CONV_HALO, tn))
            shifted = pltpu.roll(z, 1, axis=0)
            shifted = jnp.concatenate([jnp.where(top, carried, shifted[:CONV_HALO]), shifted[CONV_HALO:]], axis=0)
            z = cw_ref[which, tap:tap + 1, cols] * acc + shifted
        y = cb_ref[which, :, cols] + z
        y = y * _sigmoid(y)
        return y * MLSTM_QK_DIM ** -0.5 if which == 0 else y

    def norm_and_gates():
        xf = x_ref[...]
        ms = jnp.mean(xf * xf, axis=-1, keepdims=True)
        hb = ((xf * lax.rsqrt(ms + EPS)) * wn_ref[...]).astype(BF16)

        gr = lax.dot_general(wgt_ref[...], hb, (((1,), (1,)), ((), ())), preferred_element_type=F32) + gbt_ref[...]
        row = lax.broadcasted_iota(jnp.int32, gr.shape, 0)
        is_in_gate = (row >= ATTN_HEADS) & (row < ATTN_HEADS + MLSTM_HEADS)
        gr = jnp.where(is_in_gate, gr, _log_sigmoid(gr))
        cs_chunk = _lane_prefix_sum(gr, chunk)
        cs_full = _lane_prefix_sum(gr, tm)
        carry = jnp.where((i % tiles_per_seq) == 0, 0.0, carry_scr[...])
        cs_full = cs_full + carry[:, :1]
        b_on_in_rows = pltpu.roll(cs_chunk, N_GATES - MLSTM_HEADS, axis=0)
        gates = jnp.where(row < ATTN_HEADS, cs_full * LOG2E, jnp.where(is_in_gate, gr - b_on_in_rows, cs_chunk))
        padded = jnp.concatenate([gates, jnp.zeros((GATE_LANES - gates.shape[0], tm), F32)], axis=0)
        h_scr[slot_out[0]] = hb
        carry_scr[...] = jnp.broadcast_to(cs_full[:, tm - 1:tm], carry_scr.shape)
        grow_ref[...] = gates
        gcol_ref[...] = padded.T

    def project(between):
        qw = qn_ref[...] * (LOG2E * ATTN_HEAD_DIM ** -0.5)
        heavy = [functools.partial(lambda t: store(t, conv_silu(tile_acc(t), t)), t) for t in range(_T_MQ, _T_MV)]
        heavy += [functools.partial(lambda t: store(t, head_norm(tile_acc(t), qw)), t) for t in range(_T_AQ, _T_AK)]
        heavy += [functools.partial(lambda t: store(t, head_norm(tile_acc(t), kn_ref[...])), t)
                  for t in range(_T_AK, _T_AV)]
        plain = [functools.partial(lambda t: store(t, tile_acc(t)), t)
                 for t in list(range(_T_AV, _T_MQ)) + list(range(_T_MV, _T_END))]
        assert len(heavy) == len(plain)
        for n, (hv, pn) in enumerate(zip(heavy, plain)):
            hv()
            pn()
            if n == len(heavy) // 6:
                between()

    @pl.when(i == 0)
    def _():
        carry_scr[...] = jnp.zeros_like(carry_scr)
        conv_scr[...] = jnp.zeros_like(conv_scr)
        slot_out[0] = 0
        norm_and_gates()

    for parity in range(2):
        @pl.when((i > 0) & (i % 2 == parity))
        def _():
            slot_in[0], slot_out[0] = 1 - parity, parity
            project(norm_and_gates)


def _in_proj(x2d, wn, w_parts, wgt, gbt, qn, kn, cw, cb, *, batch, seq):
    T, D = x2d.shape
    tm, tn = ROW_TILE, PROJ_COL_TILE
    n_cols = sum(w.shape[0] for w in w_parts)
    assert [w.shape[0] // tn for w in w_parts] == [_T_MQ - _T_AQ, _T_MO - _T_MQ, _T_END - _T_MO]
    n_tiles = T // tm
    tiles_per_seq = seq // tm
    const = lambda i: (0, 0)
    const3 = lambda i: (0, 0, 0)
    tile = lambda i: jnp.minimum(i, n_tiles - 1)
    kern = functools.partial(_in_proj_kernel, n_tiles=n_tiles, tiles_per_seq=tiles_per_seq, chunk=MLSTM_CHUNK)
    return pl.pallas_call(
        kern,
        grid=(n_tiles + 1,),
        in_specs=[
            pl.BlockSpec((tm, D), lambda i: (tile(i), 0)),
            pl.BlockSpec((1, D), const),
            *[pl.BlockSpec(w.shape, const, pipeline_mode=pl.Buffered(1)) for w in w_parts],
            pl.BlockSpec((N_GATES, D), const),
            pl.BlockSpec((N_GATES, 1), const),
            pl.BlockSpec((1, ATTN_HEAD_DIM), const),
            pl.BlockSpec((1, ATTN_HEAD_DIM), const),
            pl.BlockSpec((2, CONV_WIDTH, MLSTM_QK_WIDTH), const3),
            pl.BlockSpec((2, 1, MLSTM_QK_WIDTH), const3),
        ],
        out_specs=[
            pl.BlockSpec((tm, n_cols), lambda i: (jnp.maximum(i - 1, 0), 0)),
            pl.BlockSpec((tm, GATE_LANES), lambda i: (i, 0)),
            pl.BlockSpec((None, N_GATES, tm), lambda i: (i, 0, 0)),
        ],
        out_shape=[
            jax.ShapeDtypeStruct((T, n_cols), BF16),
            jax.ShapeDtypeStruct((T + tm, GATE_LANES), F32),
            jax.ShapeDtypeStruct((n_tiles + 1, N_GATES, tm), F32),
        ],
        scratch_shapes=[
            pltpu.VMEM((2, tm, D), BF16),
            pltpu.VMEM((N_GATES, GATE_LANES), F32),
            pltpu.VMEM((2, CONV_WIDTH - 1, CONV_HALO, MLSTM_QK_WIDTH), F32),
        ],
        compiler_params=pltpu.CompilerParams(
            dimension_semantics=("arbitrary",), vmem_limit_bytes=VMEM_LIMIT_BYTES),
        name="in_proj",
    )(x2d, wn, *w_parts, wgt, gbt, qn, kn, cw, cb)


def _fox_kernel(q_ref, k_ref, v_ref, c_ref, ccol_ref, *rest, tk, cast_blocks, n_steps):
    n_cast = len(cast_blocks)
    w32_refs, (o_ref, *w16_refs), kmax_scr = rest[:n_cast], rest[n_cast:2 * n_cast + 1], rest[-1]
    hg = pl.program_id(1)
    qi = pl.program_id(2)
    step = (pl.program_id(0) * pl.num_programs(1) + hg) * pl.num_programs(2) + qi
    for w32, w16, n_blocks in zip(w32_refs, w16_refs, cast_blocks):
        if n_blocks == n_steps:
            w16[...] = w32[...].astype(w16.dtype)
        else:
            @pl.when(step < n_blocks)
            def _():
                w16[...] = w32[...].astype(w16.dtype)

    tq = q_ref.shape[0]
    d = ATTN_HEAD_DIM
    n_heads = q_ref.shape[1] // d
    n_diag = tq // tk
    qs = [q_ref[:, g * d:(g + 1) * d] for g in range(n_heads)]

    @pl.when(qi == 0)
    def _():
        for g in range(n_heads):
            kf = k_ref[:, g * d:(g + 1) * d].astype(F32)
            k2 = jnp.max(jnp.sum(kf * kf, axis=-1, keepdims=True), axis=0, keepdims=True)
            kmax_scr[g] = jnp.broadcast_to(k2, kmax_scr.shape[1:])

    lane = lax.broadcasted_iota(jnp.int32, ccol_ref.shape, 1)
    ccol = ccol_ref[...]
    mis, bounds = [], []
    for g in range(n_heads):
        qf = qs[g].astype(F32)
        bound = jnp.sqrt(jnp.sum(qf * qf, axis=-1, keepdims=True) * kmax_scr[g][:1, :1])
        ci = jnp.sum(jnp.where(lane == hg * n_heads + g, ccol, 0.0), axis=-1, keepdims=True)
        mis.append(bound - ci)
        bounds.append(jnp.max(bound))
    worst = functools.reduce(jnp.maximum, bounds)

    def load_block(j, g):
        start = j * tk if isinstance(j, int) else pl.multiple_of(j * tk, tk)
        k = k_ref[pl.ds(start, tk), g * d:(g + 1) * d]
        v = v_ref[pl.ds(start, tk), g * d:(g + 1) * d]
        per_tile = c_ref.shape[2] // tk
        cj = c_ref[j // per_tile, pl.ds(hg * n_heads + g, 1), pl.ds((j % per_tile) * tk, tk)]
        return k, v, cj

    def causal_mask(r0):
        r = lax.broadcasted_iota(jnp.int32, (tq - r0, tk), 0)
        c = lax.broadcasted_iota(jnp.int32, (tq - r0, tk), 1)
        return c <= r

    def rejoin(old, new, r0):
        return jnp.concatenate([old[:r0], new], axis=0) if r0 else new

    def finish(carry):
        for g in range(n_heads):
            l, acc = carry[g][-2:]
            l = jnp.sum(l, axis=-1, keepdims=True)
            o_ref[:, g * d:(g + 1) * d] = (acc / l).astype(o_ref.dtype)

    def scores(j, g, r0):
        k = load_block(j, g)[0]
        return lax.dot_general(qs[g][r0:], k, (((1,), (1,)), ((), ())), preferred_element_type=F32)

    def consume(j, g, s, l, acc, diag):
        r0 = 0 if diag is None else diag * tk
        _, v, cj = load_block(j, g)
        e = (s - cj) - mis[g][r0:]
        if diag is not None:
            e = jnp.where(causal_mask(r0), e, -jnp.inf)
        p = jnp.exp2(e)
        l_new = l[r0:] + functools.reduce(jnp.add, [p[:, c0:c0 + d] for c0 in range(0, tk, d)])
        acc_new = acc[r0:] + jnp.dot(p.astype(BF16), v, preferred_element_type=F32)
        return rejoin(l, l_new, r0), rejoin(acc, acc_new, r0)

    def bounded_step(j, carry, diag):
        r0 = 0 if diag is None else diag * tk
        return tuple(consume(j, g, scores(j, g, r0), *carry[g], diag) for g in range(n_heads))


    def online_step(j, carry, diag):
        r0 = 0 if diag is None else diag * tk
        out = []
        for g in range(n_heads):
            m, l, acc = carry[g]
            k, v, cj = load_block(j, g)
            s = lax.dot_general(qs[g][r0:], k, (((1,), (1,)), ((), ())), preferred_element_type=F32) - cj
            if diag is not None:
                s = jnp.where(causal_mask(r0), s, -jnp.inf)
            m_new = jnp.maximum(m[r0:], jnp.max(s, axis=-1, keepdims=True))
            alpha = jnp.exp2(m[r0:] - m_new)
            p = jnp.exp2(s - m_new)
            l_new = alpha * l[r0:] + jnp.sum(p, axis=-1, keepdims=True)
            acc_new = alpha * acc[r0:] + jnp.dot(p.astype(BF16), v, preferred_element_type=F32)
            out.append((rejoin(m, m_new, r0), rejoin(l, l_new, r0), rejoin(acc, acc_new, r0)))
        return tuple(out)

    def run(step, init):
        carry = lax.fori_loop(0, qi * n_diag, lambda j, c: step(j, c, None), init)
        for jj in range(n_diag):
            carry = step(qi * n_diag + jj, carry, jj)
        finish(carry)

    zeros = (jnp.zeros((tq, 1), F32), jnp.zeros((tq, d), F32))

    def run_unrolled(step, init):
        for c in range(k_ref.shape[0] // tq):
            @pl.when(qi == c)
            def _():
                carry = init
                for j in range(c * n_diag):
                    carry = step(j, carry, None)
                for jj in range(n_diag):
                    carry = step(c * n_diag + jj, carry, jj)
                finish(carry)

    @pl.when(worst <= FOX_MAX_BOUND)
    def _():
        run_unrolled(bounded_step,
                     tuple((jnp.zeros((tq, d), F32), jnp.zeros((tq, d), F32)) for _ in range(n_heads)))

    @pl.when(jnp.logical_not(worst <= FOX_MAX_BOUND))
    def _():
        run(online_step, tuple((jnp.full((tq, 1), -jnp.inf, F32),) + zeros for _ in range(n_heads)))


def _cast_block_rows(rows, n_steps):
    for rb in range(BF16_SUBLANES, rows + 1, BF16_SUBLANES):
        if rows % rb == 0 and rows // rb <= n_steps:
            return rb
    raise ValueError(f"cannot split {rows} rows over {n_steps} steps")


def _fox_attention(proj3, grow, gcol3, f32_weights, *, batch, seq):
    tq, tk = ATTN_Q_TILE, ATTN_KV_TILE
    assert tq % tk == 0
    dd = ATTN_HEAD_DIM * ATTN_HEADS_PER_STEP
    n_groups = ATTN_HEADS // ATTN_HEADS_PER_STEP
    nq = seq // tq
    grid = (batch, n_groups, nq)
    n_steps = batch * n_groups * nq
    cast_rows = [_cast_block_rows(w.shape[0], n_steps) for w in f32_weights]
    cast_blocks = tuple(w.shape[0] // rb for w, rb in zip(f32_weights, cast_rows))

    def cast_spec(w, rb, n_blocks):
        return pl.BlockSpec((rb, w.shape[1]), lambda b, h, qi: (jnp.minimum((b * n_groups + h) * nq + qi,
                                                                            n_blocks - 1), 0))

    cast_specs = [cast_spec(w, rb, nb) for w, rb, nb in zip(f32_weights, cast_rows, cast_blocks)]
    out = pl.pallas_call(
        functools.partial(_fox_kernel, tk=tk, cast_blocks=cast_blocks, n_steps=n_steps),
        grid=grid,
        in_specs=[
            pl.BlockSpec((None, tq, dd), lambda b, h, qi: (b, qi, h)),
            pl.BlockSpec((None, seq, dd), lambda b, h, qi: (b, 0, n_groups + h)),
            pl.BlockSpec((None, seq, dd), lambda b, h, qi: (b, 0, 2 * n_groups + h)),
            pl.BlockSpec((seq // ROW_TILE, N_GATES, ROW_TILE), lambda b, h, qi: (b, 0, 0)),
            pl.BlockSpec((tq, GATE_LANES), lambda b, h, qi: (b * (seq // tq) + qi, 0)),
            *cast_specs,
        ],
        out_specs=[pl.BlockSpec((None, tq, dd), lambda b, h, qi: (b, qi, h)), *cast_specs],
        out_shape=[jax.ShapeDtypeStruct((batch, seq, ATTN_WIDTH), BF16),
                   *[jax.ShapeDtypeStruct(w.shape, BF16) for w in f32_weights]],
        scratch_shapes=[pltpu.VMEM((ATTN_HEADS_PER_STEP, 8, 128), F32)],
        compiler_params=pltpu.CompilerParams(
            dimension_semantics=("arbitrary", "arbitrary", "arbitrary"), vmem_limit_bytes=VMEM_LIMIT_BYTES),
        name="fox_attention",
    )(proj3, proj3, proj3, grow, gcol3, *f32_weights)
    return out[0], out[1:]


def _mlstm_kernel(q_ref, k_ref, v_ref, og_ref, gcol_ref, grow_ref, nw_ref, o_ref, c_scr, m_scr):
    ci = pl.program_id(1)
    L = MLSTM_CHUNK
    dk, dv = MLSTM_QK_DIM, MLSTM_V_DIM

    @pl.when(ci == 0)
    def _():
        c_scr[...] = jnp.zeros_like(c_scr)
        m_scr[...] = jnp.zeros_like(m_scr)

    t_idx = lax.broadcasted_iota(jnp.int32, (L, L), 0)
    s_idx = lax.broadcasted_iota(jnp.int32, (L, L), 1)
    causal = s_idx <= t_idx
    nt = (((1,), (1,)), ((), ()))
    heads = range(MLSTM_HEADS)
    units = [(c, hh) for c in range(q_ref.shape[0] // L) for hh in heads]

    rows = {u: slice(u[0] * L, (u[0] + 1) * L) for u in units}
    q = {u: q_ref[rows[u], u[1] * dk:(u[1] + 1) * dk] for u in units}
    k = {u: k_ref[rows[u], u[1] * dk:(u[1] + 1) * dk] for u in units}
    v = {u: v_ref[rows[u], u[1] * dv:(u[1] + 1) * dv] for u in units}
    gi = {u: ATTN_HEADS + u[1] for u in units}
    gf = {u: ATTN_HEADS + MLSTM_HEADS + u[1] for u in units}
    gt = grow_ref.shape[2]
    grow_at = {u: (u[0] * L // gt, slice(u[0] * L % gt, u[0] * L % gt + L)) for u in units}
    b_row = {u: grow_ref[grow_at[u][0], gf[u]:gf[u] + 1, grow_at[u][1]] for u in units}
    b_col = {u: gcol_ref[rows[u], gf[u]:gf[u] + 1] for u in units}
    a_row = {u: grow_ref[grow_at[u][0], gi[u]:gi[u] + 1, grow_at[u][1]] for u in units}
    a_col = {u: gcol_ref[rows[u], gi[u]:gi[u] + 1] for u in units}
    ones = jnp.ones((L, ONES_LANES), BF16)
    v1 = {u: jnp.concatenate([v[u], ones], axis=-1) for u in units}
    qk = {u: lax.dot_general(q[u], k[u], nt, preferred_element_type=F32) for u in units}
    m_loc = {u: jnp.max(jnp.where(causal, a_row[u], -jnp.inf), axis=-1, keepdims=True) for u in units}
    s_loc = {u: qk[u] * jnp.exp(jnp.where(causal, a_row[u] - m_loc[u], -jnp.inf)) for u in units}
    pv1 = {u: jnp.dot(s_loc[u].astype(BF16), v1[u], preferred_element_type=F32) for u in units}
    pv_loc = {u: pv1[u][:, :dv] for u in units}
    d_loc = {u: pv1[u][:, dv:dv + 1] for u in units}
    a_max = {u: jnp.max(a_row[u], axis=-1, keepdims=True) for u in units}
    kw = {u: (k[u].astype(F32) * jnp.exp(a_col[u] - a_max[u])).astype(BF16) for u in units}
    upd1 = {u: lax.dot_general(kw[u], v1[u], (((0,), (0,)), ((), ())), preferred_element_type=F32)
            for u in units}

    m_prev = [m_scr[hh][:1, :1] for hh in heads]
    ct1 = [c_scr[hh] for hh in heads]
    for c in range(q_ref.shape[0] // L):
        us = [(c, hh) for hh in heads]
        qc1 = [jnp.dot(q[u], ct1[u[1]].astype(BF16), preferred_element_type=F32) for u in us]
        qc = [x[:, :dv] for x in qc1]
        qn = [x[:, dv:dv + 1] for x in qc1]
        big_m = [jnp.maximum(m_loc[u], m_prev[u[1]]) for u in us]
        alpha = [jnp.exp(m_loc[u] - big_m[hh]) for hh, u in enumerate(us)]
        beta = [jnp.exp(m_prev[hh] - big_m[hh]) for hh in heads]
        num = [alpha[hh] * pv_loc[u] + beta[hh] * qc[hh] for hh, u in enumerate(us)]
        den = [alpha[hh] * d_loc[u] + beta[hh] * qn[hh] for hh, u in enumerate(us)]
        dmax = [jnp.maximum(jnp.abs(den[hh]), jnp.exp(-(b_col[u] + big_m[hh]))) for hh, u in enumerate(us)]
        ms = [jnp.mean(num[hh] * num[hh], axis=-1, keepdims=True) for hh in heads]
        for hh, u in enumerate(us):
            hn = num[hh] * lax.rsqrt(ms[hh] + EPS * (dmax[hh] * dmax[hh]))
            og = og_ref[rows[u], hh * dv:(hh + 1) * dv].astype(F32)
            out = (hn * nw_ref[:, hh * dv:(hh + 1) * dv]) * _sigmoid(og)
            o_ref[rows[u], hh * dv:(hh + 1) * dv] = out.astype(o_ref.dtype)
        m_last = [jnp.maximum(a_max[u], m_prev[u[1]]) for u in us]
        wc = [jnp.exp(m_prev[hh] - m_last[hh]) for hh in heads]
        gamma = [jnp.exp(a_max[u] - m_last[hh]) for hh, u in enumerate(us)]
        ct1 = [wc[hh] * ct1[hh] + gamma[hh] * upd1[u] for hh, u in enumerate(us)]
        m_prev = [b_row[u][:, L - 1:L] + m_last[hh] for hh, u in enumerate(us)]

    for hh in heads:
        c_scr[hh] = ct1[hh]
        m_scr[hh] = jnp.broadcast_to(m_prev[hh], m_scr.shape[1:])


def _mlstm(proj3, gcol3, grow, nw, *, batch, seq):
    L = MLSTM_CHUNK * MLSTM_CHUNKS_PER_STEP
    assert L % ROW_TILE == 0
    grid = (batch, seq // L)
    qk_w, v_w = MLSTM_QK_WIDTH, MLSTM_WIDTH
    base = 3 * ATTN_WIDTH
    return pl.pallas_call(
        _mlstm_kernel,
        grid=grid,
        in_specs=[
            pl.BlockSpec((None, L, qk_w), lambda b, c: (b, c, base // qk_w)),
            pl.BlockSpec((None, L, qk_w), lambda b, c: (b, c, base // qk_w + 1)),
            pl.BlockSpec((None, L, v_w), lambda b, c: (b, c, (base + 2 * qk_w) // v_w)),
            pl.BlockSpec((None, L, v_w), lambda b, c: (b, c, (base + 2 * qk_w) // v_w + 1)),
            pl.BlockSpec((L, GATE_LANES), lambda b, c: (b * (seq // L) + c, 0)),
            pl.BlockSpec((L // ROW_TILE, N_GATES, ROW_TILE), lambda b, c: (b * (seq // L) + c, 0, 0)),
            pl.BlockSpec((1, v_w), lambda b, c: (0, 0)),
        ],
        out_specs=pl.BlockSpec((None, L, v_w), lambda b, c: (b, c, 0)),
        out_shape=jax.ShapeDtypeStruct((batch, seq, v_w), BF16),
        scratch_shapes=[
            pltpu.VMEM((MLSTM_HEADS, MLSTM_QK_DIM, MLSTM_V_DIM + ONES_LANES), F32),
            pltpu.VMEM((MLSTM_HEADS, 8, 128), F32),
        ],
        compiler_params=pltpu.CompilerParams(
            dimension_semantics=("arbitrary", "arbitrary"), vmem_limit_bytes=VMEM_LIMIT_BYTES),
        name="mlstm",
    )(proj3, proj3, proj3, proj3, gcol3, grow, nw)


def _out_proj_kernel(x_ref, a_ref, m_ref, wa_ref, wm_ref, o_ref):
    y = jnp.dot(a_ref[...], wa_ref[...], preferred_element_type=F32)
    y = y + jnp.dot(m_ref[...], wm_ref[...], preferred_element_type=F32)
    o_ref[...] = x_ref[...] + y


def _out_proj(x2d, attn2d, mlstm2d, w_out):
    T, D = x2d.shape
    tm = ROW_TILE
    half = attn2d.shape[1]
    assert mlstm2d.shape[1] == half and w_out.shape == (2 * half, D)
    return pl.pallas_call(
        _out_proj_kernel,
        grid=(T // tm,),
        in_specs=[
            pl.BlockSpec((tm, D), lambda i: (i, 0)),
            pl.BlockSpec((tm, attn2d.shape[1]), lambda i: (i, 0)),
            pl.BlockSpec((tm, mlstm2d.shape[1]), lambda i: (i, 0)),
            pl.BlockSpec((half, D), lambda i: (0, 0)),
            pl.BlockSpec((half, D), lambda i: (1, 0)),
        ],
        out_specs=pl.BlockSpec((tm, D), lambda i: (i, 0)),
        out_shape=jax.ShapeDtypeStruct((T, D), F32),
        compiler_params=pltpu.CompilerParams(
            dimension_semantics=("arbitrary",), vmem_limit_bytes=VMEM_LIMIT_BYTES),
        name="out_proj",
    )(x2d, attn2d, mlstm2d, w_out, w_out)


def _ffn_kernel(x_ref, wn_ref, wg_ref, wu_ref, wd_ref, o_ref, h_scr):
    f = pl.program_id(1)

    @pl.when(f == 0)
    def _():
        xf = x_ref[...]
        ms = jnp.mean(xf * xf, axis=-1, keepdims=True)
        h_scr[...] = ((xf * lax.rsqrt(ms + EPS)) * wn_ref[...]).astype(BF16)
        o_ref[...] = xf

    hb = h_scr[...]
    g = jnp.dot(hb, wg_ref[...], preferred_element_type=F32)
    u = jnp.dot(hb, wu_ref[...], preferred_element_type=F32)
    a = (g * _sigmoid(g)) * u
    o_ref[...] += jnp.dot(a.astype(BF16), wd_ref[...], preferred_element_type=F32)


def _ffn(x2d, wn, wg, wu, wd):
    T, D = x2d.shape
    F = wg.shape[1]
    tm, tf = FFN_ROW_TILE, FFN_COL_TILE
    return pl.pallas_call(
        _ffn_kernel,
        grid=(T // tm, F // tf),
        in_specs=[
            pl.BlockSpec((tm, D), lambda i, f: (i, 0)),
            pl.BlockSpec((1, D), lambda i, f: (0, 0)),
            pl.BlockSpec((D, tf), lambda i, f: (0, f)),
            pl.BlockSpec((D, tf), lambda i, f: (0, f)),
            pl.BlockSpec((tf, D), lambda i, f: (f, 0)),
        ],
        out_specs=pl.BlockSpec((tm, D), lambda i, f: (i, 0)),
        out_shape=jax.ShapeDtypeStruct((T, D), F32),
        scratch_shapes=[pltpu.VMEM((tm, D), BF16)],
        compiler_params=pltpu.CompilerParams(
            dimension_semantics=("arbitrary", "arbitrary"), vmem_limit_bytes=VMEM_LIMIT_BYTES),
        name="ffn",
    )(x2d, wn, wg, wu, wd)


def _ple_kernel(x_ref, p_ref, wn_ref, wgate_ref, wproj_ref, wpost_ref, o_ref):
    xf = x_ref[...]
    ms = jnp.mean(xf * xf, axis=-1, keepdims=True)
    hb = ((xf * lax.rsqrt(ms + EPS)) * wn_ref[...]).astype(BF16)
    e = jnp.dot(p_ref[...].astype(BF16), wproj_ref[...], preferred_element_type=F32)
    ems = jnp.mean(e * e, axis=-1, keepdims=True)
    e = (e * lax.rsqrt(ems + EPS)) * wpost_ref[...]
    tn = PROJ_COL_TILE
    for c0 in range(0, o_ref.shape[1], tn):
        gate = _sigmoid(jnp.dot(hb, wgate_ref[:, c0:c0 + tn], preferred_element_type=F32))
        o_ref[:, c0:c0 + tn] = xf[:, c0:c0 + tn] + gate * e[:, c0:c0 + tn]


def _ple(x2d, p2d, wn, wgate, wproj, wpost):
    T, D = x2d.shape
    P = p2d.shape[1]
    tm = ROW_TILE
    const = lambda i: (0, 0)
    return pl.pallas_call(
        _ple_kernel,
        grid=(T // tm,),
        in_specs=[
            pl.BlockSpec((tm, D), lambda i: (i, 0)),
            pl.BlockSpec((tm, P), lambda i: (i, 0)),
            pl.BlockSpec((1, D), const),
            pl.BlockSpec((D, D), const),
            pl.BlockSpec((P, D), const),
            pl.BlockSpec((1, D), const),
        ],
        out_specs=pl.BlockSpec((tm, D), lambda i: (i, 0)),
        out_shape=jax.ShapeDtypeStruct((T, D), F32),
        compiler_params=pltpu.CompilerParams(
            dimension_semantics=("arbitrary",), vmem_limit_bytes=VMEM_LIMIT_BYTES),
        name="ple",
    )(x2d, p2d, wn, wgate, wproj, wpost)


def _layer(x2d, p2d, batch, seq, w_norm_mix, w_in, fox_f_bias, q_norm_w, k_norm_w, mlstm_conv_w, mlstm_conv_b,
           mlstm_i_bias, mlstm_f_bias, mlstm_out_norm_w, w_out, w_norm_ffn, w_ffn_gate, w_ffn_up, w_ffn_down,
           w_norm_ple, w_ple_gate, w_ple_proj, w_ple_post_norm):
    D = x2d.shape[1]
    A, QK, MV = ATTN_WIDTH, MLSTM_QK_WIDTH, MLSTM_WIDTH
    o_af = 3 * A
    o_m = o_af + ATTN_HEADS
    o_mi = o_m + 2 * QK + MV
    o_mo = o_mi + 2 * MLSTM_HEADS
    w_t = w_in.T.astype(BF16)
    w_parts = (w_t[:o_af], w_t[o_m:o_mi], w_t[o_mo:])
    wgt = jnp.concatenate([w_t[o_af:o_m], w_t[o_mi:o_mo]], axis=0)
    gbt = jnp.concatenate([fox_f_bias, mlstm_i_bias, mlstm_f_bias]).astype(F32)[:, None]
    cw = mlstm_conv_w.reshape(CONV_WIDTH, 2, QK).transpose(1, 0, 2)
    cb = mlstm_conv_b.reshape(2, 1, QK)

    proj, gcol, grow = _in_proj(x2d, w_norm_mix[None, :], w_parts, wgt, gbt, q_norm_w[None, :],
                                k_norm_w[None, :], cw, cb, batch=batch, seq=seq)
    proj3 = proj.reshape(batch, seq, proj.shape[1])
    attn, (wg16, wu16, wd16, wpg16, wo16) = _fox_attention(
        proj3, grow, gcol, (w_ffn_gate, w_ffn_up, w_ffn_down, w_ple_gate, w_out), batch=batch, seq=seq)
    mlstm = _mlstm(proj3, gcol, grow, mlstm_out_norm_w[None, :], batch=batch, seq=seq)

    x1 = _out_proj(x2d, attn.reshape(batch * seq, A), mlstm.reshape(batch * seq, MV), wo16)
    x2 = _ffn(x1, w_norm_ffn[None, :], wg16, wu16, wd16)
    x3 = _ple(x2, p2d, w_norm_ple[None, :], wpg16, w_ple_proj.astype(BF16), w_ple_post_norm[None, :])
    return x3


def kernel(x, p, w_norm_mix, w_in, fox_f_bias, q_norm_w, k_norm_w, mlstm_conv_w, mlstm_conv_b, mlstm_i_bias,
           mlstm_f_bias, mlstm_out_norm_w, w_out, w_norm_ffn, w_ffn_gate, w_ffn_up, w_ffn_down, w_norm_ple,
           w_ple_gate, w_ple_proj, w_ple_post_norm):
    B, S, D = x.shape
    depth = w_in.shape[0]
    x2d = x.reshape(B * S, D)
    for i in range(depth):
        x2d = _layer(x2d, p[i].reshape(B * S, p.shape[-1]), B, S, w_norm_mix[i], w_in[i], fox_f_bias[i],
                     q_norm_w[i], k_norm_w[i], mlstm_conv_w[i], mlstm_conv_b[i], mlstm_i_bias[i], mlstm_f_bias[i],
                     mlstm_out_norm_w[i], w_out[i], w_norm_ffn[i], w_ffn_gate[i], w_ffn_up[i], w_ffn_down[i],
                     w_norm_ple[i], w_ple_gate[i], w_ple_proj[i], w_ple_post_norm[i])
    return x2d.reshape(B, S, D)
```

```python
import functools
import math

import jax
import jax.numpy as jnp
from jax import lax
from jax.experimental import pallas as pl
from jax.experimental.pallas import tpu as pltpu

F32 = jnp.float32
BF16 = jnp.bfloat16
EPS = 1e-6

ATTN_HEADS = 8
ATTN_HEAD_DIM = 128
MLSTM_HEADS = 4
MLSTM_QK_DIM = 128
MLSTM_V_DIM = 256
CONV_WIDTH = 4
ATTN_WIDTH = ATTN_HEADS * ATTN_HEAD_DIM
MLSTM_QK_WIDTH = MLSTM_HEADS * MLSTM_QK_DIM
MLSTM_WIDTH = MLSTM_HEADS * MLSTM_V_DIM
N_GATES = ATTN_HEADS + 2 * MLSTM_HEADS
GATE_LANES = 128
CONV_HALO = 8
BF16_SUBLANES = 16

VMEM_LIMIT_BYTES = 56 * 1024 * 1024

ROW_TILE = 512
PROJ_COL_TILE = 256
MLSTM_CHUNK = 256
MLSTM_CHUNKS_PER_STEP = 2
ONES_LANES = 128
ATTN_Q_TILE = 1024
ATTN_KV_TILE = 256
ATTN_HEADS_PER_STEP = 2
FFN_COL_TILE = 512
FFN_ROW_TILE = 1024
LOG2E = math.log2(math.e)
FOX_MAX_BOUND = 40.0

_T_AQ, _T_AK, _T_AV, _T_MQ, _T_MK, _T_MV, _T_MO, _T_END = (
    w // PROJ_COL_TILE for w in (0, ATTN_WIDTH, 2 * ATTN_WIDTH, 3 * ATTN_WIDTH, 3 * ATTN_WIDTH + MLSTM_QK_WIDTH,
                                 3 * ATTN_WIDTH + 2 * MLSTM_QK_WIDTH, 3 * ATTN_WIDTH + 2 * MLSTM_QK_WIDTH + MLSTM_WIDTH,
                                 3 * ATTN_WIDTH + 2 * MLSTM_QK_WIDTH + 2 * MLSTM_WIDTH))


def _log_sigmoid(z):
    return jnp.minimum(z, 0.0) - jnp.log1p(jnp.exp(-jnp.abs(z)))


def _sigmoid(z):
    return 1.0 / (1.0 + jnp.exp(-z))


def _lane_prefix_sum(v, period):
    axis = v.ndim - 1
    pos = lax.broadcasted_iota(jnp.int32, v.shape, axis) & (period - 1)
    shift = 1
    while shift < period:
        v = v + jnp.where(pos >= shift, pltpu.roll(v, shift, axis=axis), 0.0)
        shift *= 2
    return v


def _in_proj_kernel(x_ref, wn_ref, wa_ref, wm_ref, wo_ref, wgt_ref, gbt_ref, qn_ref, kn_ref, cw_ref, cb_ref,
                    proj_ref, gcol_ref, grow_ref, h_scr, carry_scr, conv_scr, *, n_tiles, tiles_per_seq, chunk):
    i = pl.program_id(0)
    tm = x_ref.shape[0]
    tn = PROJ_COL_TILE
    seq_start = ((i - 1) % tiles_per_seq) == 0
    slot_in, slot_out = [0], [0]

    def tile_acc(t):
        w_ref, t0 = (wa_ref, _T_AQ) if t < _T_MQ else (wm_ref, _T_MQ) if t < _T_MO else (wo_ref, _T_MO)
        return lax.dot_general(h_scr[slot_in[0]], w_ref[(t - t0) * tn:(t - t0 + 1) * tn, :],
                               (((1,), (1,)), ((), ())), preferred_element_type=F32)

    def store(t, val):
        proj_ref[:, t * tn:(t + 1) * tn] = val.astype(BF16)

    def head_norm(acc, w):
        outs = []
        for hh in range(tn // ATTN_HEAD_DIM):
            a = acc[:, hh * ATTN_HEAD_DIM:(hh + 1) * ATTN_HEAD_DIM]
            ms = jnp.mean(a * a, axis=-1, keepdims=True)
            outs.append((a * lax.rsqrt(ms + EPS)) * w)
        return jnp.concatenate(outs, axis=-1)

    def conv_silu(acc, t):
        which = 0 if t < _T_MK else 1
        cols = pl.ds((t - (_T_MQ, _T_MK)[which]) * tn, tn)
        top = lax.broadcasted_iota(jnp.int32, (CONV_HALO, tn), 0) == 0
        z = cw_ref[which, 0:1, cols] * acc
        for tap in range(1, CONV_WIDTH):
            carried = jnp.where(seq_start, 0.0, conv_scr[which, tap - 1, :, cols])
            conv_scr[which, tap - 1, :, cols] = jnp.broadcast_to(z[tm - 1:tm, :], (CONV_HALO, tn))
            shifted = pltpu.roll(z, 1, axis=0)
            shifted = jnp.concatenate([jnp.where(top, carried, shifted[:CONV_HALO]), shifted[CONV_HALO:]], axis=0)
            z = cw_ref[which, tap:tap + 1, cols] * acc + shifted
        y = cb_ref[which, :, cols] + z
        y = y * _sigmoid(y)
        return y * MLSTM_QK_DIM ** -0.5 if which == 0 else y

    def norm_and_gates():
        xf = x_ref[...]
        ms = jnp.mean(xf * xf, axis=-1, keepdims=True)
        hb = ((xf * lax.rsqrt(ms + EPS)) * wn_ref[...]).astype(BF16)

        gr = lax.dot_general(wgt_ref[...], hb, (((1,), (1,)), ((), ())), preferred_element_type=F32) + gbt_ref[...]
        row = lax.broadcasted_iota(jnp.int32, gr.shape, 0)
        is_in_gate = (row >= ATTN_HEADS) & (row < ATTN_HEADS + MLSTM_HEADS)
        gr = jnp.where(is_in_gate, gr, _log_sigmoid(gr))
        cs_chunk = _lane_prefix_sum(gr, chunk)
        cs_full = _lane_prefix_sum(gr, tm)
        carry = jnp.where((i % tiles_per_seq) == 0, 0.0, carry_scr[...])
        cs_full = cs_full + carry[:, :1]
        b_on_in_rows = pltpu.roll(cs_chunk, N_GATES - MLSTM_HEADS, axis=0)
        gates = jnp.where(row < ATTN_HEADS, cs_full * LOG2E, jnp.where(is_in_gate, gr - b_on_in_rows, cs_chunk))
        padded = jnp.concatenate([gates, jnp.zeros((GATE_LANES - gates.shape[0], tm), F32)], axis=0)
        h_scr[slot_out[0]] = hb
        carry_scr[...] = jnp.broadcast_to(cs_full[:, tm - 1:tm], carry_scr.shape)
        grow_ref[...] = gates
        gcol_ref[...] = padded.T

    def project(between):
        qw = qn_ref[...] * (LOG2E * ATTN_HEAD_DIM ** -0.5)
        heavy = [functools.partial(lambda t: store(t, conv_silu(tile_acc(t), t)), t) for t in range(_T_MQ, _T_MV)]
        heavy += [functools.partial(lambda t: store(t, head_norm(tile_acc(t), qw)), t) for t in range(_T_AQ, _T_AK)]
        heavy += [functools.partial(lambda t: store(t, head_norm(tile_acc(t), kn_ref[...])), t)
                  for t in range(_T_AK, _T_AV)]
        plain = [functools.partial(lambda t: store(t, tile_acc(t)), t)
                 for t in list(range(_T_AV, _T_MQ)) + list(range(_T_MV, _T_END))]
        assert len(heavy) == len(plain)
        for n, (hv, pn) in enumerate(zip(heavy, plain)):
            hv()
            pn()
            if n == len(heavy) // 6:
                between()

    @pl.when(i == 0)
    def _():
        carry_scr[...] = jnp.zeros_like(carry_scr)
        conv_scr[...] = jnp.zeros_like(conv_scr)
        slot_out[0] = 0
        norm_and_gates()

    for parity in range(2):
        @pl.when((i > 0) & (i % 2 == parity))
        def _():
            slot_in[0], slot_out[0] = 1 - parity, parity
            project(norm_and_gates)


def _in_proj(x2d, wn, w_parts, wgt, gbt, qn, kn, cw, cb, *, batch, seq):
    T, D = x2d.shape
    tm, tn = ROW_TILE, PROJ_COL_TILE
    n_cols = sum(w.shape[0] for w in w_parts)
    assert [w.shape[0] // tn for w in w_parts] == [_T_MQ - _T_AQ, _T_MO - _T_MQ, _T_END - _T_MO]
    n_tiles = T // tm
    tiles_per_seq = seq // tm
    const = lambda i: (0, 0)
    const3 = lambda i: (0, 0, 0)
    tile = lambda i: jnp.minimum(i, n_tiles - 1)
    kern = functools.partial(_in_proj_kernel, n_tiles=n_tiles, tiles_per_seq=tiles_per_seq, chunk=MLSTM_CHUNK)
    return pl.pallas_call(
        kern,
        grid=(n_tiles + 1,),
        in_specs=[
            pl.BlockSpec((tm, D), lambda i: (tile(i), 0)),
            pl.BlockSpec((1, D), const),
            *[pl.BlockSpec(w.shape, const, pipeline_mode=pl.Buffered(1)) for w in w_parts],
            pl.BlockSpec((N_GATES, D), const),
            pl.BlockSpec((N_GATES, 1), const),
            pl.BlockSpec((1, ATTN_HEAD_DIM), const),
            pl.BlockSpec((1, ATTN_HEAD_DIM), const),
            pl.BlockSpec((2, CONV_WIDTH, MLSTM_QK_WIDTH), const3),
            pl.BlockSpec((2, 1, MLSTM_QK_WIDTH), const3),
        ],
        out_specs=[
            pl.BlockSpec((tm, n_cols), lambda i: (jnp.maximum(i - 1, 0), 0)),
            pl.BlockSpec((tm, GATE_LANES), lambda i: (i, 0)),
            pl.BlockSpec((None, N_GATES, tm), lambda i: (i, 0, 0)),
        ],
        out_shape=[
            jax.ShapeDtypeStruct((T, n_cols), BF16),
            jax.ShapeDtypeStruct((T + tm, GATE_LANES), F32),
            jax.ShapeDtypeStruct((n_tiles + 1, N_GATES, tm), F32),
        ],
        scratch_shapes=[
            pltpu.VMEM((2, tm, D), BF16),
            pltpu.VMEM((N_GATES, GATE_LANES), F32),
            pltpu.VMEM((2, CONV_WIDTH - 1, CONV_HALO, MLSTM_QK_WIDTH), F32),
        ],
        compiler_params=pltpu.CompilerParams(
            dimension_semantics=("arbitrary",), vmem_limit_bytes=VMEM_LIMIT_BYTES),
        name="in_proj",
    )(x2d, wn, *w_parts, wgt, gbt, qn, kn, cw, cb)


def _fox_kernel(q_ref, k_ref, v_ref, c_ref, ccol_ref, *rest, tk, cast_blocks, n_steps):
    n_cast = len(cast_blocks)
    w32_refs, (o_ref, *w16_refs), kmax_scr = rest[:n_cast], rest[n_cast:2 * n_cast + 1], rest[-1]
    hg = pl.program_id(1)
    qi = pl.program_id(2)
    step = (pl.program_id(0) * pl.num_programs(1) + hg) * pl.num_programs(2) + qi
    for w32, w16, n_blocks in zip(w32_refs, w16_refs, cast_blocks):
        if n_blocks == n_steps:
            w16[...] = w32[...].astype(w16.dtype)
        else:
            @pl.when(step < n_blocks)
            def _():
                w16[...] = w32[...].astype(w16.dtype)

    tq = q_ref.shape[0]
    d = ATTN_HEAD_DIM
    n_heads = q_ref.shape[1] // d
    n_diag = tq // tk
    qs = [q_ref[:, g * d:(g + 1) * d] for g in range(n_heads)]

    @pl.when(qi == 0)
    def _():
        for g in range(n_heads):
            kf = k_ref[:, g * d:(g + 1) * d].astype(F32)
            k2 = jnp.max(jnp.sum(kf * kf, axis=-1, keepdims=True), axis=0, keepdims=True)
            kmax_scr[g] = jnp.broadcast_to(k2, kmax_scr.shape[1:])

    lane = lax.broadcasted_iota(jnp.int32, ccol_ref.shape, 1)
    ccol = ccol_ref[...]
    mis, bounds = [], []
    for g in range(n_heads):
        qf = qs[g].astype(F32)
        bound = jnp.sqrt(jnp.sum(qf * qf, axis=-1, keepdims=True) * kmax_scr[g][:1, :1])
        ci = jnp.sum(jnp.where(lane == hg * n_heads + g, ccol, 0.0), axis=-1, keepdims=True)
        mis.append(bound - ci)
        bounds.append(jnp.max(bound))
    worst = functools.reduce(jnp.maximum, bounds)

    def load_block(j, g):
        start = j * tk if isinstance(j, int) else pl.multiple_of(j * tk, tk)
        k = k_ref[pl.ds(start, tk), g * d:(g + 1) * d]
        v = v_ref[pl.ds(start, tk), g * d:(g + 1) * d]
        per_tile = c_ref.shape[2] // tk
        cj = c_ref[j // per_tile, pl.ds(hg * n_heads + g, 1), pl.ds((j % per_tile) * tk, tk)]
        return k, v, cj

    def causal_mask(r0):
        r = lax.broadcasted_iota(jnp.int32, (tq - r0, tk), 0)
        c = lax.broadcasted_iota(jnp.int32, (tq - r0, tk), 1)
        return c <= r

    def rejoin(old, new, r0):
        return jnp.concatenate([old[:r0], new], axis=0) if r0 else new

    def finish(carry):
        for g in range(n_heads):
            l, acc = carry[g][-2:]
            l = jnp.sum(l, axis=-1, keepdims=True)
            o_ref[:, g * d:(g + 1) * d] = (acc / l).astype(o_ref.dtype)

    def scores(j, g, r0):
        k = load_block(j, g)[0]
        return lax.dot_general(qs[g][r0:], k, (((1,), (1,)), ((), ())), preferred_element_type=F32)

    def consume(j, g, s, l, acc, diag):
        r0 = 0 if diag is None else diag * tk
        _, v, cj = load_block(j, g)
        e = (s - cj) - mis[g][r0:]
        if diag is not None:
            e = jnp.where(causal_mask(r0), e, -jnp.inf)
        p = jnp.exp2(e)
        l_new = l[r0:] + functools.reduce(jnp.add, [p[:, c0:c0 + d] for c0 in range(0, tk, d)])
        acc_new = acc[r0:] + jnp.dot(p.astype(BF16), v, preferred_element_type=F32)
        return rejoin(l, l_new, r0), rejoin(acc, acc_new, r0)

    def bounded_step(j, carry, diag):
        r0 = 0 if diag is None else diag * tk
        return tuple(consume(j, g, scores(j, g, r0), *carry[g], diag) for g in range(n_heads))


    def online_step(j, carry, diag):
        r0 = 0 if diag is None else diag * tk
        out = []
        for g in range(n_heads):
            m, l, acc = carry[g]
            k, v, cj = load_block(j, g)
            s = lax.dot_general(qs[g][r0:], k, (((1,), (1,)), ((), ())), preferred_element_type=F32) - cj
            if diag is not None:
                s = jnp.where(causal_mask(r0), s, -jnp.inf)
            m_new = jnp.maximum(m[r0:], jnp.max(s, axis=-1, keepdims=True))
            alpha = jnp.exp2(m[r0:] - m_new)
            p = jnp.exp2(s - m_new)
            l_new = alpha * l[r0:] + jnp.sum(p, axis=-1, keepdims=True)
            acc_new = alpha * acc[r0:] + jnp.dot(p.astype(BF16), v, preferred_element_type=F32)
            out.append((rejoin(m, m_new, r0), rejoin(l, l_new, r0), rejoin(acc, acc_new, r0)))
        return tuple(out)

    def run(step, init):
        carry = lax.fori_loop(0, qi * n_diag, lambda j, c: step(j, c, None), init)
        for jj in range(n_diag):
            carry = step(qi * n_diag + jj, carry, jj)
        finish(carry)

    zeros = (jnp.zeros((tq, 1), F32), jnp.zeros((tq, d), F32))

    def run_unrolled(step, init):
        for c in range(k_ref.shape[0] // tq):
            @pl.when(qi == c)
            def _():
                carry = init
                for j in range(c * n_diag):
                    carry = step(j, carry, None)
                for jj in range(n_diag):
                    carry = step(c * n_diag + jj, carry, jj)
                finish(carry)

    @pl.when(worst <= FOX_MAX_BOUND)
    def _():
        run_unrolled(bounded_step,
                     tuple((jnp.zeros((tq, d), F32), jnp.zeros((tq, d), F32)) for _ in range(n_heads)))

    @pl.when(jnp.logical_not(worst <= FOX_MAX_BOUND))
    def _():
        run(online_step, tuple((jnp.full((tq, 1), -jnp.inf, F32),) + zeros for _ in range(n_heads)))


def _cast_block_rows(rows, n_steps):
    for rb in range(BF16_SUBLANES, rows + 1, BF16_SUBLANES):
        if rows % rb == 0 and rows // rb <= n_steps:
            return rb
    raise ValueError(f"cannot split {rows} rows over {n_steps} steps")


def _fox_attention(proj3, grow, gcol3, f32_weights, *, batch, seq):
    tq, tk = ATTN_Q_TILE, ATTN_KV_TILE
    assert tq % tk == 0
    dd = ATTN_HEAD_DIM * ATTN_HEADS_PER_STEP
    n_groups = ATTN_HEADS // ATTN_HEADS_PER_STEP
    nq = seq // tq
    grid = (batch, n_groups, nq)
    n_steps = batch * n_groups * nq
    cast_rows = [_cast_block_rows(w.shape[0], n_steps) for w in f32_weights]
    cast_blocks = tuple(w.shape[0] // rb for w, rb in zip(f32_weights, cast_rows))

    def cast_spec(w, rb, n_blocks):
        return pl.BlockSpec((rb, w.shape[1]), lambda b, h, qi: (jnp.minimum((b * n_groups + h) * nq + qi,
                                                                            n_blocks - 1), 0))

    cast_specs = [cast_spec(w, rb, nb) for w, rb, nb in zip(f32_weights, cast_rows, cast_blocks)]
    out = pl.pallas_call(
        functools.partial(_fox_kernel, tk=tk, cast_blocks=cast_blocks, n_steps=n_steps),
        grid=grid,
        in_specs=[
            pl.BlockSpec((None, tq, dd), lambda b, h, qi: (b, qi, h)),
            pl.BlockSpec((None, seq, dd), lambda b, h, qi: (b, 0, n_groups + h)),
            pl.BlockSpec((None, seq, dd), lambda b, h, qi: (b, 0, 2 * n_groups + h)),
            pl.BlockSpec((seq // ROW_TILE, N_GATES, ROW_TILE), lambda b, h, qi: (b, 0, 0)),
            pl.BlockSpec((tq, GATE_LANES), lambda b, h, qi: (b * (seq // tq) + qi, 0)),
            *cast_specs,
        ],
        out_specs=[pl.BlockSpec((None, tq, dd), lambda b, h, qi: (b, qi, h)), *cast_specs],
        out_shape=[jax.ShapeDtypeStruct((batch, seq, ATTN_WIDTH), BF16),
                   *[jax.ShapeDtypeStruct(w.shape, BF16) for w in f32_weights]],
        scratch_shapes=[pltpu.VMEM((ATTN_HEADS_PER_STEP, 8, 128), F32)],
        compiler_params=pltpu.CompilerParams(
            dimension_semantics=("arbitrary", "arbitrary", "arbitrary"), vmem_limit_bytes=VMEM_LIMIT_BYTES),
        name="fox_attention",
    )(proj3, proj3, proj3, grow, gcol3, *f32_weights)
    return out[0], out[1:]


def _mlstm_kernel(q_ref, k_ref, v_ref, og_ref, gcol_ref, grow_ref, nw_ref, o_ref, c_scr, m_scr):
    ci = pl.program_id(1)
    L = MLSTM_CHUNK
    dk, dv = MLSTM_QK_DIM, MLSTM_V_DIM

    @pl.when(ci == 0)
    def _():
        c_scr[...] = jnp.zeros_like(c_scr)
        m_scr[...] = jnp.zeros_like(m_scr)

    t_idx = lax.broadcasted_iota(jnp.int32, (L, L), 0)
    s_idx = lax.broadcasted_iota(jnp.int32, (L, L), 1)
    causal = s_idx <= t_idx
    nt = (((1,), (1,)), ((), ()))
    heads = range(MLSTM_HEADS)
    units = [(c, hh) for c in range(q_ref.shape[0] // L) for hh in heads]

    rows = {u: slice(u[0] * L, (u[0] + 1) * L) for u in units}
    q = {u: q_ref[rows[u], u[1] * dk:(u[1] + 1) * dk] for u in units}
    k = {u: k_ref[rows[u], u[1] * dk:(u[1] + 1) * dk] for u in units}
    v = {u: v_ref[rows[u], u[1] * dv:(u[1] + 1) * dv] for u in units}
    gi = {u: ATTN_HEADS + u[1] for u in units}
    gf = {u: ATTN_HEADS + MLSTM_HEADS + u[1] for u in units}
    gt = grow_ref.shape[2]
    grow_at = {u: (u[0] * L // gt, slice(u[0] * L % gt, u[0] * L % gt + L)) for u in units}
    b_row = {u: grow_ref[grow_at[u][0], gf[u]:gf[u] + 1, grow_at[u][1]] for u in units}
    b_col = {u: gcol_ref[rows[u], gf[u]:gf[u] + 1] for u in units}
    a_row = {u: grow_ref[grow_at[u][0], gi[u]:gi[u] + 1, grow_at[u][1]] for u in units}
    a_col = {u: gcol_ref[rows[u], gi[u]:gi[u] + 1] for u in units}
    ones = jnp.ones((L, ONES_LANES), BF16)
    v1 = {u: jnp.concatenate([v[u], ones], axis=-1) for u in units}
    qk = {u: lax.dot_general(q[u], k[u], nt, preferred_element_type=F32) for u in units}
    m_loc = {u: jnp.max(jnp.where(causal, a_row[u], -jnp.inf), axis=-1, keepdims=True) for u in units}
    s_loc = {u: qk[u] * jnp.exp(jnp.where(causal, a_row[u] - m_loc[u], -jnp.inf)) for u in units}
    pv1 = {u: jnp.dot(s_loc[u].astype(BF16), v1[u], preferred_element_type=F32) for u in units}
    pv_loc = {u: pv1[u][:, :dv] for u in units}
    d_loc = {u: pv1[u][:, dv:dv + 1] for u in units}
    a_max = {u: jnp.max(a_row[u], axis=-1, keepdims=True) for u in units}
    kw = {u: (k[u].astype(F32) * jnp.exp(a_col[u] - a_max[u])).astype(BF16) for u in units}
    upd1 = {u: lax.dot_general(kw[u], v1[u], (((0,), (0,)), ((), ())), preferred_element_type=F32)
            for u in units}

    m_prev = [m_scr[hh][:1, :1] for hh in heads]
    ct1 = [c_scr[hh] for hh in heads]
    for c in range(q_ref.shape[0] // L):
        us = [(c, hh) for hh in heads]
        qc1 = [jnp.dot(q[u], ct1[u[1]].astype(BF16), preferred_element_type=F32) for u in us]
        qc = [x[:, :dv] for x in qc1]
        qn = [x[:, dv:dv + 1] for x in qc1]
        big_m = [jnp.maximum(m_loc[u], m_prev[u[1]]) for u in us]
        alpha = [jnp.exp(m_loc[u] - big_m[hh]) for hh, u in enumerate(us)]
        beta = [jnp.exp(m_prev[hh] - big_m[hh]) for hh in heads]
        num = [alpha[hh] * pv_loc[u] + beta[hh] * qc[hh] for hh, u in enumerate(us)]
        den = [alpha[hh] * d_loc[u] + beta[hh] * qn[hh] for hh, u in enumerate(us)]
        dmax = [jnp.maximum(jnp.abs(den[hh]), jnp.exp(-(b_col[u] + big_m[hh]))) for hh, u in enumerate(us)]
        ms = [jnp.mean(num[hh] * num[hh], axis=-1, keepdims=True) for hh in heads]
        for hh, u in enumerate(us):
            hn = num[hh] * lax.rsqrt(ms[hh] + EPS * (dmax[hh] * dmax[hh]))
            og = og_ref[rows[u], hh * dv:(hh + 1) * dv].astype(F32)
            out = (hn * nw_ref[:, hh * dv:(hh + 1) * dv]) * _sigmoid(og)
            o_ref[rows[u], hh * dv:(hh + 1) * dv] = out.astype(o_ref.dtype)
        m_last = [jnp.maximum(a_max[u], m_prev[u[1]]) for u in us]
        wc = [jnp.exp(m_prev[hh] - m_last[hh]) for hh in heads]
        gamma = [jnp.exp(a_max[u] - m_last[hh]) for hh, u in enumerate(us)]
        ct1 = [wc[hh] * ct1[hh] + gamma[hh] * upd1[u] for hh, u in enumerate(us)]
        m_prev = [b_row[u][:, L - 1:L] + m_last[hh] for hh, u in enumerate(us)]

    for hh in heads:
        c_scr[hh] = ct1[hh]
        m_scr[hh] = jnp.broadcast_to(m_prev[hh], m_scr.shape[1:])


def _mlstm(proj3, gcol3, grow, nw, *, batch, seq):
    L = MLSTM_CHUNK * MLSTM_CHUNKS_PER_STEP
    assert L % ROW_TILE == 0
    grid = (batch, seq // L)
    qk_w, v_w = MLSTM_QK_WIDTH, MLSTM_WIDTH
    base = 3 * ATTN_WIDTH
    return pl.pallas_call(
        _mlstm_kernel,
        grid=grid,
        in_specs=[
            pl.BlockSpec((None, L, qk_w), lambda b, c: (b, c, base // qk_w)),
            pl.BlockSpec((None, L, qk_w), lambda b, c: (b, c, base // qk_w + 1)),
            pl.BlockSpec((None, L, v_w), lambda b, c: (b, c, (base + 2 * qk_w) // v_w)),
            pl.BlockSpec((None, L, v_w), lambda b, c: (b, c, (base + 2 * qk_w) // v_w + 1)),
            pl.BlockSpec((L, GATE_LANES), lambda b, c: (b * (seq // L) + c, 0)),
            pl.BlockSpec((L // ROW_TILE, N_GATES, ROW_TILE), lambda b, c: (b * (seq // L) + c, 0, 0)),
            pl.BlockSpec((1, v_w), lambda b, c: (0, 0)),
        ],
        out_specs=pl.BlockSpec((None, L, v_w), lambda b, c: (b, c, 0)),
        out_shape=jax.ShapeDtypeStruct((batch, seq, v_w), BF16),
        scratch_shapes=[
            pltpu.VMEM((MLSTM_HEADS, MLSTM_QK_DIM, MLSTM_V_DIM + ONES_LANES), F32),
            pltpu.VMEM((MLSTM_HEADS, 8, 128), F32),
        ],
        compiler_params=pltpu.CompilerParams(
            dimension_semantics=("arbitrary", "arbitrary"), vmem_limit_bytes=VMEM_LIMIT_BYTES),
        name="mlstm",
    )(proj3, proj3, proj3, proj3, gcol3, grow, nw)


def _out_proj_kernel(x_ref, a_ref, m_ref, wa_ref, wm_ref, o_ref):
    y = jnp.dot(a_ref[...], wa_ref[...], preferred_element_type=F32)
    y = y + jnp.dot(m_ref[...], wm_ref[...], preferred_element_type=F32)
    o_ref[...] = x_ref[...] + y


def _out_proj(x2d, attn2d, mlstm2d, w_out):
    T, D = x2d.shape
    tm = ROW_TILE
    half = attn2d.shape[1]
    assert mlstm2d.shape[1] == half and w_out.shape == (2 * half, D)
    return pl.pallas_call(
        _out_proj_kernel,
        grid=(T // tm,),
        in_specs=[
            pl.BlockSpec((tm, D), lambda i: (i, 0)),
            pl.BlockSpec((tm, attn2d.shape[1]), lambda i: (i, 0)),
            pl.BlockSpec((tm, mlstm2d.shape[1]), lambda i: (i, 0)),
            pl.BlockSpec((half, D), lambda i: (0, 0)),
            pl.BlockSpec((half, D), lambda i: (1, 0)),
        ],
        out_specs=pl.BlockSpec((tm, D), lambda i: (i, 0)),
        out_shape=jax.ShapeDtypeStruct((T, D), F32),
        compiler_params=pltpu.CompilerParams(
            dimension_semantics=("arbitrary",), vmem_limit_bytes=VMEM_LIMIT_BYTES),
        name="out_proj",
    )(x2d, attn2d, mlstm2d, w_out, w_out)


def _ffn_kernel(x_ref, wn_ref, wg_ref, wu_ref, wd_ref, o_ref, h_scr):
    f = pl.program_id(1)

    @pl.when(f == 0)
    def _():
        xf = x_ref[...]
        ms = jnp.mean(xf * xf, axis=-1, keepdims=True)
        h_scr[...] = ((xf * lax.rsqrt(ms + EPS)) * wn_ref[...]).astype(BF16)
        o_ref[...] = xf

    hb = h_scr[...]
    g = jnp.dot(hb, wg_ref[...], preferred_element_type=F32)
    u = jnp.dot(hb, wu_ref[...], preferred_element_type=F32)
    a = (g * _sigmoid(g)) * u
    o_ref[...] += jnp.dot(a.astype(BF16), wd_ref[...], preferred_element_type=F32)


def _ffn(x2d, wn, wg, wu, wd):
    T, D = x2d.shape
    F = wg.shape[1]
    tm, tf = FFN_ROW_TILE, FFN_COL_TILE
    return pl.pallas_call(
        _ffn_kernel,
        grid=(T // tm, F // tf),
        in_specs=[
            pl.BlockSpec((tm, D), lambda i, f: (i, 0)),
            pl.BlockSpec((1, D), lambda i, f: (0, 0)),
            pl.BlockSpec((D, tf), lambda i, f: (0, f)),
            pl.BlockSpec((D, tf), lambda i, f: (0, f)),
            pl.BlockSpec((tf, D), lambda i, f: (f, 0)),
        ],
        out_specs=pl.BlockSpec((tm, D), lambda i, f: (i, 0)),
        out_shape=jax.ShapeDtypeStruct((T, D), F32),
        scratch_shapes=[pltpu.VMEM((tm, D), BF16)],
        compiler_params=pltpu.CompilerParams(
            dimension_semantics=("arbitrary", "arbitrary"), vmem_limit_bytes=VMEM_LIMIT_BYTES),
        name="ffn",
    )(x2d, wn, wg, wu, wd)


def _ple_kernel(x_ref, p_ref, wn_ref, wgate_ref, wproj_ref, wpost_ref, o_ref):
    xf = x_ref[...]
    ms = jnp.mean(xf * xf, axis=-1, keepdims=True)
    hb = ((xf * lax.rsqrt(ms + EPS)) * wn_ref[...]).astype(BF16)
    e = jnp.dot(p_ref[...].astype(BF16), wproj_ref[...], preferred_element_type=F32)
    ems = jnp.mean(e * e, axis=-1, keepdims=True)
    e = (e * lax.rsqrt(ems + EPS)) * wpost_ref[...]
    tn = PROJ_COL_TILE
    for c0 in range(0, o_ref.shape[1], tn):
        gate = _sigmoid(jnp.dot(hb, wgate_ref[:, c0:c0 + tn], preferred_element_type=F32))
        o_ref[:, c0:c0 + tn] = xf[:, c0:c0 + tn] + gate * e[:, c0:c0 + tn]


def _ple(x2d, p2d, wn, wgate, wproj, wpost):
    T, D = x2d.shape
    P = p2d.shape[1]
    tm = ROW_TILE
    const = lambda i: (0, 0)
    return pl.pallas_call(
        _ple_kernel,
        grid=(T // tm,),
        in_specs=[
            pl.BlockSpec((tm, D), lambda i: (i, 0)),
            pl.BlockSpec((tm, P), lambda i: (i, 0)),
            pl.BlockSpec((1, D), const),
            pl.BlockSpec((D, D), const),
            pl.BlockSpec((P, D), const),
            pl.BlockSpec((1, D), const),
        ],
        out_specs=pl.BlockSpec((tm, D), lambda i: (i, 0)),
        out_shape=jax.ShapeDtypeStruct((T, D), F32),
        compiler_params=pltpu.CompilerParams(
            dimension_semantics=("arbitrary",), vmem_limit_bytes=VMEM_LIMIT_BYTES),
        name="ple",
    )(x2d, p2d, wn, wgate, wproj, wpost)


def _layer(x2d, p2d, batch, seq, w_norm_mix, w_in, fox_f_bias, q_norm_w, k_norm_w, mlstm_conv_w, mlstm_conv_b,
           mlstm_i_bias, mlstm_f_bias, mlstm_out_norm_w, w_out, w_norm_ffn, w_ffn_gate, w_ffn_up, w_ffn_down,
           w_norm_ple, w_ple_gate, w_ple_proj, w_ple_post_norm):
    D = x2d.shape[1]
    A, QK, MV = ATTN_WIDTH, MLSTM_QK_WIDTH, MLSTM_WIDTH
    o_af = 3 * A
    o_m = o_af + ATTN_HEADS
    o_mi = o_m + 2 * QK + MV
    o_mo = o_mi + 2 * MLSTM_HEADS
    w_t = w_in.T.astype(BF16)
    w_parts = (w_t[:o_af], w_t[o_m:o_mi], w_t[o_mo:])
    wgt = jnp.concatenate([w_t[o_af:o_m], w_t[o_mi:o_mo]], axis=0)
    gbt = jnp.concatenate([fox_f_bias, mlstm_i_bias, mlstm_f_bias]).astype(F32)[:, None]
    cw = mlstm_conv_w.reshape(CONV_WIDTH, 2, QK).transpose(1, 0, 2)
    cb = mlstm_conv_b.reshape(2, 1, QK)

    proj, gcol, grow = _in_proj(x2d, w_norm_mix[None, :], w_parts, wgt, gbt, q_norm_w[None, :],
                                k_norm_w[None, :], cw, cb, batch=batch, seq=seq)
    proj3 = proj.reshape(batch, seq, proj.shape[1])
    attn, (wg16, wu16, wd16, wpg16, wo16) = _fox_attention(
        proj3, grow, gcol, (w_ffn_gate, w_ffn_up, w_ffn_down, w_ple_gate, w_out), batch=batch, seq=seq)
    mlstm = _mlstm(proj3, gcol, grow, mlstm_out_norm_w[None, :], batch=batch, seq=seq)

    x1 = _out_proj(x2d, attn.reshape(batch * seq, A), mlstm.reshape(batch * seq, MV), wo16)
    x2 = _ffn(x1, w_norm_ffn[None, :], wg16, wu16, wd16)
    x3 = _ple(x2, p2d, w_norm_ple[None, :], wpg16, w_ple_proj.astype(BF16), w_ple_post_norm[None, :])
    return x3


def kernel(x, p, w_norm_mix, w_in, fox_f_bias, q_norm_w, k_norm_w, mlstm_conv_w, mlstm_conv_b, mlstm_i_bias,
           mlstm_f_bias, mlstm_out_norm_w, w_out, w_norm_ffn, w_ffn_gate, w_ffn_up, w_ffn_down, w_norm_ple,
           w_ple_gate, w_ple_proj, w_ple_post_norm):
    B, S, D = x.shape
    depth = w_in.shape[0]
    x2d = x.reshape(B * S, D)
    for i in range(depth):
        x2d = _layer(x2d, p[i].reshape(B * S, p.shape[-1]), B, S, w_norm_mix[i], w_in[i], fox_f_bias[i],
                     q_norm_w[i], k_norm_w[i], mlstm_conv_w[i], mlstm_conv_b[i], mlstm_i_bias[i], mlstm_f_bias[i],
                     mlstm_out_norm_w[i], w_out[i], w_norm_ffn[i], w_ffn_gate[i], w_ffn_up[i], w_ffn_down[i],
                     w_norm_ple[i], w_ple_gate[i], w_ple_proj[i], w_ple_post_norm[i])
    return x2d.reshape(B, S, D)
```

```python
import functools
import math

import jax
import jax.numpy as jnp
from jax import lax
from jax.experimental import pallas as pl
from jax.experimental.pallas import tpu as pltpu

F32 = jnp.float32
BF16 = jnp.bfloat16
EPS = 1e-6

ATTN_HEADS = 8
ATTN_HEAD_DIM = 128
MLSTM_HEADS = 4
MLSTM_QK_DIM = 128
MLSTM_V_DIM = 256
CONV_WIDTH = 4
ATTN_WIDTH = ATTN_HEADS * ATTN_HEAD_DIM
MLSTM_QK_WIDTH = MLSTM_HEADS * MLSTM_QK_DIM
MLSTM_WIDTH = MLSTM_HEADS * MLSTM_V_DIM
N_GATES = ATTN_HEADS + 2 * MLSTM_HEADS
GATE_LANES = 128
CONV_HALO = 8
BF16_SUBLANES = 16

VMEM_LIMIT_BYTES = 56 * 1024 * 1024

ROW_TILE = 512
PROJ_COL_TILE = 256
MLSTM_CHUNK = 256
MLSTM_CHUNKS_PER_STEP = 2
ONES_LANES = 128
ATTN_Q_TILE = 1024
ATTN_KV_TILE = 256
ATTN_HEADS_PER_STEP = 2
FFN_COL_TILE = 512
FFN_ROW_TILE = 1024
LOG2E = math.log2(math.e)
FOX_MAX_BOUND = 40.0

_T_AQ, _T_AK, _T_AV, _T_MQ, _T_MK, _T_MV, _T_MO, _T_END = (
    w // PROJ_COL_TILE for w in (0, ATTN_WIDTH, 2 * ATTN_WIDTH, 3 * ATTN_WIDTH, 3 * ATTN_WIDTH + MLSTM_QK_WIDTH,
                                 3 * ATTN_WIDTH + 2 * MLSTM_QK_WIDTH, 3 * ATTN_WIDTH + 2 * MLSTM_QK_WIDTH + MLSTM_WIDTH,
                                 3 * ATTN_WIDTH + 2 * MLSTM_QK_WIDTH + 2 * MLSTM_WIDTH))


def _log_sigmoid(z):
    return jnp.minimum(z, 0.0) - jnp.log1p(jnp.exp(-jnp.abs(z)))


def _sigmoid(z):
    return 1.0 / (1.0 + jnp.exp(-z))


def _lane_prefix_sum(v, period):
    axis = v.ndim - 1
    pos = lax.broadcasted_iota(jnp.int32, v.shape, axis) & (period - 1)
    shift = 1
    while shift < period:
        v = v + jnp.where(pos >= shift, pltpu.roll(v, shift, axis=axis), 0.0)
        shift *= 2
    return v


def _in_proj_kernel(x_ref, wn_ref, wa_ref, wm_ref, wo_ref, wgt_ref, gbt_ref, qn_ref, kn_ref, cw_ref, cb_ref,
                    proj_ref, gcol_ref, grow_ref, h_scr, carry_scr, conv_scr, *, n_tiles, tiles_per_seq, chunk):
    i = pl.program_id(0)
    tm = x_ref.shape[0]
    tn = PROJ_COL_TILE
    seq_start = ((i - 1) % tiles_per_seq) == 0
    slot_in, slot_out = [0], [0]

    def tile_acc(t):
        w_ref, t0 = (wa_ref, _T_AQ) if t < _T_MQ else (wm_ref, _T_MQ) if t < _T_MO else (wo_ref, _T_MO)
        return lax.dot_general(h_scr[slot_in[0]], w_ref[(t - t0) * tn:(t - t0 + 1) * tn, :],
                               (((1,), (1,)), ((), ())), preferred_element_type=F32)

    def store(t, val):
        proj_ref[:, t * tn:(t + 1) * tn] = val.astype(BF16)

    def head_norm(acc, w):
        outs = []
        for hh in range(tn // ATTN_HEAD_DIM):
            a = acc[:, hh * ATTN_HEAD_DIM:(hh + 1) * ATTN_HEAD_DIM]
            ms = jnp.mean(a * a, axis=-1, keepdims=True)
            outs.append((a * lax.rsqrt(ms + EPS)) * w)
        return jnp.concatenate(outs, axis=-1)

    def conv_silu(acc, t):
        which = 0 if t < _T_MK else 1
        cols = pl.ds((t - (_T_MQ, _T_MK)[which]) * tn, tn)
        top = lax.broadcasted_iota(jnp.int32, (CONV_HALO, tn), 0) == 0
        z = cw_ref[which, 0:1, cols] * acc
        for tap in range(1, CONV_WIDTH):
            carried = jnp.where(seq_start, 0.0, conv_scr[which, tap - 1, :, cols])
            conv_scr[which, tap - 1, :, cols] = jnp.broadcast_to(z[tm - 1:tm, :], (CONV_HALO, tn))
            shifted = pltpu.roll(z, 1, axis=0)
            shifted = jnp.concatenate([jnp.where(top, carried, shifted[:CONV_HALO]), shifted[CONV_HALO:]], axis=0)
            z = cw_ref[which, tap:tap + 1, cols] * acc + shifted
        y = cb_ref[which, :, cols] + z
        y = y * _sigmoid(y)
        return y * MLSTM_QK_DIM ** -0.5 if which == 0 else y

    def norm_and_gates():
        xf = x_ref[...]
        ms = jnp.mean(xf * xf, axis=-1, keepdims=True)
        hb = ((xf * lax.rsqrt(ms + EPS)) * wn_ref[...]).astype(BF16)

        gr = lax.dot_general(wgt_ref[...], hb, (((1,), (1,)), ((), ())), preferred_element_type=F32) + gbt_ref[...]
        row = lax.broadcasted_iota(jnp.int32, gr.shape, 0)
        is_in_gate = (row >= ATTN_HEADS) & (row < ATTN_HEADS + MLSTM_HEADS)
        gr = jnp.where(is_in_gate, gr, _log_sigmoid(gr))
        cs_chunk = _lane_prefix_sum(gr, chunk)
        cs_full = _lane_prefix_sum(gr, tm)
        carry = jnp.where((i % tiles_per_seq) == 0, 0.0, carry_scr[...])
        cs_full = cs_full + carry[:, :1]
        b_on_in_rows = pltpu.roll(cs_chunk, N_GATES - MLSTM_HEADS, axis=0)
        gates = jnp.where(row < ATTN_HEADS, cs_full * LOG2E, jnp.where(is_in_gate, gr - b_on_in_rows, cs_chunk))
        padded = jnp.concatenate([gates, jnp.zeros((GATE_LANES - gates.shape[0], tm), F32)], axis=0)
        h_scr[slot_out[0]] = hb
        carry_scr[...] = jnp.broadcast_to(cs_full[:, tm - 1:tm], carry_scr.shape)
        grow_ref[...] = gates
        gcol_ref[...] = padded.T

    def project(between):
        qw = qn_ref[...] * (LOG2E * ATTN_HEAD_DIM ** -0.5)
        heavy = [functools.partial(lambda t: store(t, conv_silu(tile_acc(t), t)), t) for t in range(_T_MQ, _T_MV)]
        heavy += [functools.partial(lambda t: store(t, head_norm(tile_acc(t), qw)), t) for t in range(_T_AQ, _T_AK)]
        heavy += [functools.partial(lambda t: store(t, head_norm(tile_acc(t), kn_ref[...])), t)
                  for t in range(_T_AK, _T_AV)]
        plain = [functools.partial(lambda t: store(t, tile_acc(t)), t)
                 for t in list(range(_T_AV, _T_MQ)) + list(range(_T_MV, _T_END))]
        assert len(heavy) == len(plain)
        for n, (hv, pn) in enumerate(zip(heavy, plain)):
            hv()
            pn()
            if n == len(heavy) // 6:
                between()

    @pl.when(i == 0)
    def _():
        carry_scr[...] = jnp.zeros_like(carry_scr)
        conv_scr[...] = jnp.zeros_like(conv_scr)
        slot_out[0] = 0
        norm_and_gates()

    for parity in range(2):
        @pl.when((i > 0) & (i % 2 == parity))
        def _():
            slot_in[0], slot_out[0] = 1 - parity, parity
            project(norm_and_gates)


def _in_proj(x2d, wn, w_parts, wgt, gbt, qn, kn, cw, cb, *, batch, seq):
    T, D = x2d.shape
    tm, tn = ROW_TILE, PROJ_COL_TILE
    n_cols = sum(w.shape[0] for w in w_parts)
    assert [w.shape[0] // tn for w in w_parts] == [_T_MQ - _T_AQ, _T_MO - _T_MQ, _T_END - _T_MO]
    n_tiles = T // tm
    tiles_per_seq = seq // tm
    const = lambda i: (0, 0)
    const3 = lambda i: (0, 0, 0)
    tile = lambda i: jnp.minimum(i, n_tiles - 1)
    kern = functools.partial(_in_proj_kernel, n_tiles=n_tiles, tiles_per_seq=tiles_per_seq, chunk=MLSTM_CHUNK)
    return pl.pallas_call(
        kern,
        grid=(n_tiles + 1,),
        in_specs=[
            pl.BlockSpec((tm, D), lambda i: (tile(i), 0)),
            pl.BlockSpec((1, D), const),
            *[pl.BlockSpec(w.shape, const, pipeline_mode=pl.Buffered(1)) for w in w_parts],
            pl.BlockSpec((N_GATES, D), const),
            pl.BlockSpec((N_GATES, 1), const),
            pl.BlockSpec((1, ATTN_HEAD_DIM), const),
            pl.BlockSpec((1, ATTN_HEAD_DIM), const),
            pl.BlockSpec((2, CONV_WIDTH, MLSTM_QK_WIDTH), const3),
            pl.BlockSpec((2, 1, MLSTM_QK_WIDTH), const3),
        ],
        out_specs=[
            pl.BlockSpec((tm, n_cols), lambda i: (jnp.maximum(i - 1, 0), 0)),
            pl.BlockSpec((tm, GATE_LANES), lambda i: (i, 0)),
            pl.BlockSpec((None, N_GATES, tm), lambda i: (i, 0, 0)),
        ],
        out_shape=[
            jax.ShapeDtypeStruct((T, n_cols), BF16),
            jax.ShapeDtypeStruct((T + tm, GATE_LANES), F32),
            jax.ShapeDtypeStruct((n_tiles + 1, N_GATES, tm), F32),
        ],
        scratch_shapes=[
            pltpu.VMEM((2, tm, D), BF16),
            pltpu.VMEM((N_GATES, GATE_LANES), F32),
            pltpu.VMEM((2, CONV_WIDTH - 1, CONV_HALO, MLSTM_QK_WIDTH), F32),
        ],
        compiler_params=pltpu.CompilerParams(
            dimension_semantics=("arbitrary",), vmem_limit_bytes=VMEM_LIMIT_BYTES),
        name="in_proj",
    )(x2d, wn, *w_parts, wgt, gbt, qn, kn, cw, cb)


def _fox_kernel(q_ref, k_ref, v_ref, c_ref, ccol_ref, *rest, tk, cast_blocks, n_steps):
    n_cast = len(cast_blocks)
    w32_refs, (o_ref, *w16_refs), kmax_scr = rest[:n_cast], rest[n_cast:2 * n_cast + 1], rest[-1]
    hg = pl.program_id(1)
    qi = pl.program_id(2)
    step = (pl.program_id(0) * pl.num_programs(1) + hg) * pl.num_programs(2) + qi
    for w32, w16, n_blocks in zip(w32_refs, w16_refs, cast_blocks):
        if n_blocks == n_steps:
            w16[...] = w32[...].astype(w16.dtype)
        else:
            @pl.when(step < n_blocks)
            def _():
                w16[...] = w32[...].astype(w16.dtype)

    tq = q_ref.shape[0]
    d = ATTN_HEAD_DIM
    n_heads = q_ref.shape[1] // d
    n_diag = tq // tk
    qs = [q_ref[:, g * d:(g + 1) * d] for g in range(n_heads)]

    @pl.when(qi == 0)
    def _():
        for g in range(n_heads):
            kf = k_ref[:, g * d:(g + 1) * d].astype(F32)
            k2 = jnp.max(jnp.sum(kf * kf, axis=-1, keepdims=True), axis=0, keepdims=True)
            kmax_scr[g] = jnp.broadcast_to(k2, kmax_scr.shape[1:])

    lane = lax.broadcasted_iota(jnp.int32, ccol_ref.shape, 1)
    ccol = ccol_ref[...]
    mis, bounds = [], []
    for g in range(n_heads):
        qf = qs[g].astype(F32)
        bound = jnp.sqrt(jnp.sum(qf * qf, axis=-1, keepdims=True) * kmax_scr[g][:1, :1])
        ci = jnp.sum(jnp.where(lane == hg * n_heads + g, ccol, 0.0), axis=-1, keepdims=True)
        mis.append(bound - ci)
        bounds.append(jnp.max(bound))
    worst = functools.reduce(jnp.maximum, bounds)

    def load_block(j, g):
        start = j * tk if isinstance(j, int) else pl.multiple_of(j * tk, tk)
        k = k_ref[pl.ds(start, tk), g * d:(g + 1) * d]
        v = v_ref[pl.ds(start, tk), g * d:(g + 1) * d]
        per_tile = c_ref.shape[2] // tk
        cj = c_ref[j // per_tile, pl.ds(hg * n_heads + g, 1), pl.ds((j % per_tile) * tk, tk)]
        return k, v, cj

    def causal_mask(r0):
        r = lax.broadcasted_iota(jnp.int32, (tq - r0, tk), 0)
        c = lax.broadcasted_iota(jnp.int32, (tq - r0, tk), 1)
        return c <= r

    def rejoin(old, new, r0):
        return jnp.concatenate([old[:r0], new], axis=0) if r0 else new

    def finish(carry):
        for g in range(n_heads):
            l, acc = carry[g][-2:]
            l = jnp.sum(l, axis=-1, keepdims=True)
            o_ref[:, g * d:(g + 1) * d] = (acc / l).astype(o_ref.dtype)

    def scores(j, g, r0):
        k = load_block(j, g)[0]
        return lax.dot_general(qs[g][r0:], k, (((1,), (1,)), ((), ())), preferred_element_type=F32)

    def consume(j, g, s, l, acc, diag):
        r0 = 0 if diag is None else diag * tk
        _, v, cj = load_block(j, g)
        e = (s - cj) - mis[g][r0:]
        if diag is not None:
            e = jnp.where(causal_mask(r0), e, -jnp.inf)
        p = jnp.exp2(e)
        l_new = l[r0:] + functools.reduce(jnp.add, [p[:, c0:c0 + d] for c0 in range(0, tk, d)])
        acc_new = acc[r0:] + jnp.dot(p.astype(BF16), v, preferred_element_type=F32)
        return rejoin(l, l_new, r0), rejoin(acc, acc_new, r0)

    def bounded_step(j, carry, diag):
        r0 = 0 if diag is None else diag * tk
        return tuple(consume(j, g, scores(j, g, r0), *carry[g], diag) for g in range(n_heads))


    def online_step(j, carry, diag):
        r0 = 0 if diag is None else diag * tk
        out = []
        for g in range(n_heads):
            m, l, acc = carry[g]
            k, v, cj = load_block(j, g)
            s = lax.dot_general(qs[g][r0:], k, (((1,), (1,)), ((), ())), preferred_element_type=F32) - cj
            if diag is not None:
                s = jnp.where(causal_mask(r0), s, -jnp.inf)
            m_new = jnp.maximum(m[r0:], jnp.max(s, axis=-1, keepdims=True))
            alpha = jnp.exp2(m[r0:] - m_new)
            p = jnp.exp2(s - m_new)
            l_new = alpha * l[r0:] + jnp.sum(p, axis=-1, keepdims=True)
            acc_new = alpha * acc[r0:] + jnp.dot(p.astype(BF16), v, preferred_element_type=F32)
            out.append((rejoin(m, m_new, r0), rejoin(l, l_new, r0), rejoin(acc, acc_new, r0)))
        return tuple(out)

    def run(step, init):
        carry = lax.fori_loop(0, qi * n_diag, lambda j, c: step(j, c, None), init)
        for jj in range(n_diag):
            carry = step(qi * n_diag + jj, carry, jj)
        finish(carry)

    zeros = (jnp.zeros((tq, 1), F32), jnp.zeros((tq, d), F32))

    def run_unrolled(step, init):
        for c in range(k_ref.shape[0] // tq):
            @pl.when(qi == c)
            def _():
                carry = init
                for j in range(c * n_diag):
                    carry = step(j, carry, None)
                for jj in range(n_diag):
                    carry = step(c * n_diag + jj, carry, jj)
                finish(carry)

    @pl.when(worst <= FOX_MAX_BOUND)
    def _():
        run_unrolled(bounded_step,
                     tuple((jnp.zeros((tq, d), F32), jnp.zeros((tq, d), F32)) for _ in range(n_heads)))

    @pl.when(jnp.logical_not(worst <= FOX_MAX_BOUND))
    def _():
        run(online_step, tuple((jnp.full((tq, 1), -jnp.inf, F32),) + zeros for _ in range(n_heads)))


def _cast_block_rows(rows, n_steps):
    for rb in range(BF16_SUBLANES, rows + 1, BF16_SUBLANES):
        if rows % rb == 0 and rows // rb <= n_steps:
            return rb
    raise ValueError(f"cannot split {rows} rows over {n_steps} steps")


def _fox_attention(proj3, grow, gcol3, f32_weights, *, batch, seq):
    tq, tk = ATTN_Q_TILE, ATTN_KV_TILE
    assert tq % tk == 0
    dd = ATTN_HEAD_DIM * ATTN_HEADS_PER_STEP
    n_groups = ATTN_HEADS // ATTN_HEADS_PER_STEP
    nq = seq // tq
    grid = (batch, n_groups, nq)
    n_steps = batch * n_groups * nq
    cast_rows = [_cast_block_rows(w.shape[0], n_steps) for w in f32_weights]
    cast_blocks = tuple(w.shape[0] // rb for w, rb in zip(f32_weights, cast_rows))

    def cast_spec(w, rb, n_blocks):
        return pl.BlockSpec((rb, w.shape[1]), lambda b, h, qi: (jnp.minimum((b * n_groups + h) * nq + qi,
                                                                            n_blocks - 1), 0))

    cast_specs = [cast_spec(w, rb, nb) for w, rb, nb in zip(f32_weights, cast_rows, cast_blocks)]
    out = pl.pallas_call(
        functools.partial(_fox_kernel, tk=tk, cast_blocks=cast_blocks, n_steps=n_steps),
        grid=grid,
        in_specs=[
            pl.BlockSpec((None, tq, dd), lambda b, h, qi: (b, qi, h)),
            pl.BlockSpec((None, seq, dd), lambda b, h, qi: (b, 0, n_groups + h)),
            pl.BlockSpec((None, seq, dd), lambda b, h, qi: (b, 0, 2 * n_groups + h)),
            pl.BlockSpec((seq // ROW_TILE, N_GATES, ROW_TILE), lambda b, h, qi: (b, 0, 0)),
            pl.BlockSpec((tq, GATE_LANES), lambda b, h, qi: (b * (seq // tq) + qi, 0)),
            *cast_specs,
        ],
        out_specs=[pl.BlockSpec((None, tq, dd), lambda b, h, qi: (b, qi, h)), *cast_specs],
        out_shape=[jax.ShapeDtypeStruct((batch, seq, ATTN_WIDTH), BF16),
                   *[jax.ShapeDtypeStruct(w.shape, BF16) for w in f32_weights]],
        scratch_shapes=[pltpu.VMEM((ATTN_HEADS_PER_STEP, 8, 128), F32)],
        compiler_params=pltpu.CompilerParams(
            dimension_semantics=("arbitrary", "arbitrary", "arbitrary"), vmem_limit_bytes=VMEM_LIMIT_BYTES),
        name="fox_attention",
    )(proj3, proj3, proj3, grow, gcol3, *f32_weights)
    return out[0], out[1:]


def _mlstm_kernel(q_ref, k_ref, v_ref, og_ref, gcol_ref, grow_ref, nw_ref, o_ref, c_scr, m_scr):
    ci = pl.program_id(1)
    L = MLSTM_CHUNK
    dk, dv = MLSTM_QK_DIM, MLSTM_V_DIM

    @pl.when(ci == 0)
    def _():
        c_scr[...] = jnp.zeros_like(c_scr)
        m_scr[...] = jnp.zeros_like(m_scr)

    t_idx = lax.broadcasted_iota(jnp.int32, (L, L), 0)
    s_idx = lax.broadcasted_iota(jnp.int32, (L, L), 1)
    causal = s_idx <= t_idx
    nt = (((1,), (1,)), ((), ()))
    heads = range(MLSTM_HEADS)
    units = [(c, hh) for c in range(q_ref.shape[0] // L) for hh in heads]

    rows = {u: slice(u[0] * L, (u[0] + 1) * L) for u in units}
    q = {u: q_ref[rows[u], u[1] * dk:(u[1] + 1) * dk] for u in units}
    k = {u: k_ref[rows[u], u[1] * dk:(u[1] + 1) * dk] for u in units}
    v = {u: v_ref[rows[u], u[1] * dv:(u[1] + 1) * dv] for u in units}
    gi = {u: ATTN_HEADS + u[1] for u in units}
    gf = {u: ATTN_HEADS + MLSTM_HEADS + u[1] for u in units}
    gt = grow_ref.shape[2]
    grow_at = {u: (u[0] * L // gt, slice(u[0] * L % gt, u[0] * L % gt + L)) for u in units}
    b_row = {u: grow_ref[grow_at[u][0], gf[u]:gf[u] + 1, grow_at[u][1]] for u in units}
    b_col = {u: gcol_ref[rows[u], gf[u]:gf[u] + 1] for u in units}
    a_row = {u: grow_ref[grow_at[u][0], gi[u]:gi[u] + 1, grow_at[u][1]] for u in units}
    a_col = {u: gcol_ref[rows[u], gi[u]:gi[u] + 1] for u in units}
    ones = jnp.ones((L, ONES_LANES), BF16)
    v1 = {u: jnp.concatenate([v[u], ones], axis=-1) for u in units}
    qk = {u: lax.dot_general(q[u], k[u], nt, preferred_element_type=F32) for u in units}
    m_loc = {u: jnp.max(jnp.where(causal, a_row[u], -jnp.inf), axis=-1, keepdims=True) for u in units}
    s_loc = {u: qk[u] * jnp.exp(jnp.where(causal, a_row[u] - m_loc[u], -jnp.inf)) for u in units}
    pv1 = {u: jnp.dot(s_loc[u].astype(BF16), v1[u], preferred_element_type=F32) for u in units}
    pv_loc = {u: pv1[u][:, :dv] for u in units}
    d_loc = {u: pv1[u][:, dv:dv + 1] for u in units}
    a_max = {u: jnp.max(a_row[u], axis=-1, keepdims=True) for u in units}
    kw = {u: (k[u].astype(F32) * jnp.exp(a_col[u] - a_max[u])).astype(BF16) for u in units}
    upd1 = {u: lax.dot_general(kw[u], v1[u], (((0,), (0,)), ((), ())), preferred_element_type=F32)
            for u in units}

    m_prev = [m_scr[hh][:1, :1] for hh in heads]
    ct1 = [c_scr[hh] for hh in heads]
    for c in range(q_ref.shape[0] // L):
        us = [(c, hh) for hh in heads]
        qc1 = [jnp.dot(q[u], ct1[u[1]].astype(BF16), preferred_element_type=F32) for u in us]
        qc = [x[:, :dv] for x in qc1]
        qn = [x[:, dv:dv + 1] for x in qc1]
        big_m = [jnp.maximum(m_loc[u], m_prev[u[1]]) for u in us]
        alpha = [jnp.exp(m_loc[u] - big_m[hh]) for hh, u in enumerate(us)]
        beta = [jnp.exp(m_prev[hh] - big_m[hh]) for hh in heads]
        num = [alpha[hh] * pv_loc[u] + beta[hh] * qc[hh] for hh, u in enumerate(us)]
        den = [alpha[hh] * d_loc[u] + beta[hh] * qn[hh] for hh, u in enumerate(us)]
        dmax = [jnp.maximum(jnp.abs(den[hh]), jnp.exp(-(b_col[u] + big_m[hh]))) for hh, u in enumerate(us)]
        ms = [jnp.mean(num[hh] * num[hh], axis=-1, keepdims=True) for hh in heads]
        for hh, u in enumerate(us):
            hn = num[hh] * lax.rsqrt(ms[hh] + EPS * (dmax[hh] * dmax[hh]))
            og = og_ref[rows[u], hh * dv:(hh + 1) * dv].astype(F32)
            out = (hn * nw_ref[:, hh * dv:(hh + 1) * dv]) * _sigmoid(og)
            o_ref[rows[u], hh * dv:(hh + 1) * dv] = out.astype(o_ref.dtype)
        m_last = [jnp.maximum(a_max[u], m_prev[u[1]]) for u in us]
        wc = [jnp.exp(m_prev[hh] - m_last[hh]) for hh in heads]
        gamma = [jnp.exp(a_max[u] - m_last[hh]) for hh, u in enumerate(us)]
        ct1 = [wc[hh] * ct1[hh] + gamma[hh] * upd1[u] for hh, u in enumerate(us)]
        m_prev = [b_row[u][:, L - 1:L] + m_last[hh] for hh, u in enumerate(us)]

    for hh in heads:
        c_scr[hh] = ct1[hh]
        m_scr[hh] = jnp.broadcast_to(m_prev[hh], m_scr.shape[1:])


def _mlstm(proj3, gcol3, grow, nw, *, batch, seq):
    L = MLSTM_CHUNK * MLSTM_CHUNKS_PER_STEP
    assert L % ROW_TILE == 0
    grid = (batch, seq // L)
    qk_w, v_w = MLSTM_QK_WIDTH, MLSTM_WIDTH
    base = 3 * ATTN_WIDTH
    return pl.pallas_call(
        _mlstm_kernel,
        grid=grid,
        in_specs=[
            pl.BlockSpec((None, L, qk_w), lambda b, c: (b, c, base // qk_w)),
            pl.BlockSpec((None, L, qk_w), lambda b, c: (b, c, base // qk_w + 1)),
            pl.BlockSpec((None, L, v_w), lambda b, c: (b, c, (base + 2 * qk_w) // v_w)),
            pl.BlockSpec((None, L, v_w), lambda b, c: (b, c, (base + 2 * qk_w) // v_w + 1)),
            pl.BlockSpec((L, GATE_LANES), lambda b, c: (b * (seq // L) + c, 0)),
            pl.BlockSpec((L // ROW_TILE, N_GATES, ROW_TILE), lambda b, c: (b * (seq // L) + c, 0, 0)),
            pl.BlockSpec((1, v_w), lambda b, c: (0, 0)),
        ],
        out_specs=pl.BlockSpec((None, L, v_w), lambda b, c: (b, c, 0)),
        out_shape=jax.ShapeDtypeStruct((batch, seq, v_w), BF16),
        scratch_shapes=[
            pltpu.VMEM((MLSTM_HEADS, MLSTM_QK_DIM, MLSTM_V_DIM + ONES_LANES), F32),
            pltpu.VMEM((MLSTM_HEADS, 8, 128), F32),
        ],
        compiler_params=pltpu.CompilerParams(
            dimension_semantics=("arbitrary", "arbitrary"), vmem_limit_bytes=VMEM_LIMIT_BYTES),
        name="mlstm",
    )(proj3, proj3, proj3, proj3, gcol3, grow, nw)


def _out_proj_kernel(x_ref, a_ref, m_ref, wa_ref, wm_ref, o_ref):
    y = jnp.dot(a_ref[...], wa_ref[...], preferred_element_type=F32)
    y = y + jnp.dot(m_ref[...], wm_ref[...], preferred_element_type=F32)
    o_ref[...] = x_ref[...] + y


def _out_proj(x2d, attn2d, mlstm2d, w_out):
    T, D = x2d.shape
    tm = ROW_TILE
    half = attn2d.shape[1]
    assert mlstm2d.shape[1] == half and w_out.shape == (2 * half, D)
    return pl.pallas_call(
        _out_proj_kernel,
        grid=(T // tm,),
        in_specs=[
            pl.BlockSpec((tm, D), lambda i: (i, 0)),
            pl.BlockSpec((tm, attn2d.shape[1]), lambda i: (i, 0)),
            pl.BlockSpec((tm, mlstm2d.shape[1]), lambda i: (i, 0)),
            pl.BlockSpec((half, D), lambda i: (0, 0)),
            pl.BlockSpec((half, D), lambda i: (1, 0)),
        ],
        out_specs=pl.BlockSpec((tm, D), lambda i: (i, 0)),
        out_shape=jax.ShapeDtypeStruct((T, D), F32),
        compiler_params=pltpu.CompilerParams(
            dimension_semantics=("arbitrary",), vmem_limit_bytes=VMEM_LIMIT_BYTES),
        name="out_proj",
    )(x2d, attn2d, mlstm2d, w_out, w_out)


def _ffn_kernel(x_ref, wn_ref, wg_ref, wu_ref, wd_ref, o_ref, h_scr):
    f = pl.program_id(1)

    def block(hb):
        g = jnp.dot(hb, wg_ref[...], preferred_element_type=F32)
        u = jnp.dot(hb, wu_ref[...], preferred_element_type=F32)
        a = (g * _sigmoid(g)) * u
        return jnp.dot(a.astype(BF16), wd_ref[...], preferred_element_type=F32)

    @pl.when(f == 0)
    def _():
        xf = x_ref[...]
        ms = jnp.mean(xf * xf, axis=-1, keepdims=True)
        hb = ((xf * lax.rsqrt(ms + EPS)) * wn_ref[...]).astype(BF16)
        h_scr[...] = hb
        o_ref[...] = xf + block(hb)

    @pl.when(f > 0)
    def _():
        o_ref[...] += block(h_scr[...])


def _ffn(x2d, wn, wg, wu, wd):
    T, D = x2d.shape
    F = wg.shape[1]
    tm, tf = FFN_ROW_TILE, FFN_COL_TILE
    return pl.pallas_call(
        _ffn_kernel,
        grid=(T // tm, F // tf),
        in_specs=[
            pl.BlockSpec((tm, D), lambda i, f: (i, 0)),
            pl.BlockSpec((1, D), lambda i, f: (0, 0)),
            pl.BlockSpec((D, tf), lambda i, f: (0, f)),
            pl.BlockSpec((D, tf), lambda i, f: (0, f)),
            pl.BlockSpec((tf, D), lambda i, f: (f, 0)),
        ],
        out_specs=pl.BlockSpec((tm, D), lambda i, f: (i, 0)),
        out_shape=jax.ShapeDtypeStruct((T, D), F32),
        scratch_shapes=[pltpu.VMEM((tm, D), BF16)],
        compiler_params=pltpu.CompilerParams(
            dimension_semantics=("arbitrary", "arbitrary"), vmem_limit_bytes=VMEM_LIMIT_BYTES),
        name="ffn",
    )(x2d, wn, wg, wu, wd)


def _ple_kernel(x_ref, p_ref, wn_ref, wgate_ref, wproj_ref, wpost_ref, o_ref):
    xf = x_ref[...]
    ms = jnp.mean(xf * xf, axis=-1, keepdims=True)
    hb = ((xf * lax.rsqrt(ms + EPS)) * wn_ref[...]).astype(BF16)
    e = jnp.dot(p_ref[...].astype(BF16), wproj_ref[...], preferred_element_type=F32)
    ems = jnp.mean(e * e, axis=-1, keepdims=True)
    e = (e * lax.rsqrt(ems + EPS)) * wpost_ref[...]
    tn = PROJ_COL_TILE
    for c0 in range(0, o_ref.shape[1], tn):
        gate = _sigmoid(jnp.dot(hb, wgate_ref[:, c0:c0 + tn], preferred_element_type=F32))
        o_ref[:, c0:c0 + tn] = xf[:, c0:c0 + tn] + gate * e[:, c0:c0 + tn]


def _ple(x2d, p2d, wn, wgate, wproj, wpost):
    T, D = x2d.shape
    P = p2d.shape[1]
    tm = ROW_TILE
    const = lambda i: (0, 0)
    return pl.pallas_call(
        _ple_kernel,
        grid=(T // tm,),
        in_specs=[
            pl.BlockSpec((tm, D), lambda i: (i, 0)),
            pl.BlockSpec((tm, P), lambda i: (i, 0)),
            pl.BlockSpec((1, D), const),
            pl.BlockSpec((D, D), const),
            pl.BlockSpec((P, D), const),
            pl.BlockSpec((1, D), const),
        ],
        out_specs=pl.BlockSpec((tm, D), lambda i: (i, 0)),
        out_shape=jax.ShapeDtypeStruct((T, D), F32),
        compiler_params=pltpu.CompilerParams(
            dimension_semantics=("arbitrary",), vmem_limit_bytes=VMEM_LIMIT_BYTES),
        name="ple",
    )(x2d, p2d, wn, wgate, wproj, wpost)


def _layer(x2d, p2d, batch, seq, w_norm_mix, w_in, fox_f_bias, q_norm_w, k_norm_w, mlstm_conv_w, mlstm_conv_b,
           mlstm_i_bias, mlstm_f_bias, mlstm_out_norm_w, w_out, w_norm_ffn, w_ffn_gate, w_ffn_up, w_ffn_down,
           w_norm_ple, w_ple_gate, w_ple_proj, w_ple_post_norm):
    D = x2d.shape[1]
    A, QK, MV = ATTN_WIDTH, MLSTM_QK_WIDTH, MLSTM_WIDTH
    o_af = 3 * A
    o_m = o_af + ATTN_HEADS
    o_mi = o_m + 2 * QK + MV
    o_mo = o_mi + 2 * MLSTM_HEADS
    w_t = w_in.T.astype(BF16)
    w_parts = (w_t[:o_af], w_t[o_m:o_mi], w_t[o_mo:])
    wgt = jnp.concatenate([w_t[o_af:o_m], w_t[o_mi:o_mo]], axis=0)
    gbt = jnp.concatenate([fox_f_bias, mlstm_i_bias, mlstm_f_bias]).astype(F32)[:, None]
    cw = mlstm_conv_w.reshape(CONV_WIDTH, 2, QK).transpose(1, 0, 2)
    cb = mlstm_conv_b.reshape(2, 1, QK)

    proj, gcol, grow = _in_proj(x2d, w_norm_mix[None, :], w_parts, wgt, gbt, q_norm_w[None, :],
                                k_norm_w[None, :], cw, cb, batch=batch, seq=seq)
    proj3 = proj.reshape(batch, seq, proj.shape[1])
    attn, (wg16, wu16, wd16, wpg16, wo16) = _fox_attention(
        proj3, grow, gcol, (w_ffn_gate, w_ffn_up, w_ffn_down, w_ple_gate, w_out), batch=batch, seq=seq)
    mlstm = _mlstm(proj3, gcol, grow, mlstm_out_norm_w[None, :], batch=batch, seq=seq)

    x1 = _out_proj(x2d, attn.reshape(batch * seq, A), mlstm.reshape(batch * seq, MV), wo16)
    x2 = _ffn(x1, w_norm_ffn[None, :], wg16, wu16, wd16)
    x3 = _ple(x2, p2d, w_norm_ple[None, :], wpg16, w_ple_proj.astype(BF16), w_ple_post_norm[None, :])
    return x3


def kernel(x, p, w_norm_mix, w_in, fox_f_bias, q_norm_w, k_norm_w, mlstm_conv_w, mlstm_conv_b, mlstm_i_bias,
           mlstm_f_bias, mlstm_out_norm_w, w_out, w_norm_ffn, w_ffn_gate, w_ffn_up, w_ffn_down, w_norm_ple,
           w_ple_gate, w_ple_proj, w_ple_post_norm):
    B, S, D = x.shape
    depth = w_in.shape[0]
    x2d = x.reshape(B * S, D)
    for i in range(depth):
        x2d = _layer(x2d, p[i].reshape(B * S, p.shape[-1]), B, S, w_norm_mix[i], w_in[i], fox_f_bias[i],
                     q_norm_w[i], k_norm_w[i], mlstm_conv_w[i], mlstm_conv_b[i], mlstm_i_bias[i], mlstm_f_bias[i],
                     mlstm_out_norm_w[i], w_out[i], w_norm_ffn[i], w_ffn_gate[i], w_ffn_up[i], w_ffn_down[i],
                     w_norm_ple[i], w_ple_gate[i], w_ple_proj[i], w_ple_post_norm[i])
    return x2d.reshape(B, S, D)
```
